```python
import math
import jax, jax.numpy as jnp
from jax import lax
import numpy as np

D_MODEL = 1024
BATCH = 16
SEQ = 2048
DEPTH = 2

N_HEADS = 16
HEAD_DIM = D_MODEL // N_HEADS
N_A_LAYERS = DEPTH // 2
N_B_LAYERS = DEPTH - N_A_LAYERS
Q_BLOCK = 128
DILATED_GROUPS = ((128, 1), (512, 4), (2048, 16))
N_GROUPS = len(DILATED_GROUPS)
NUM_BUCKETS = 32
MAX_DISTANCE = 2048
D_FF = -(-8 * D_MODEL // (3 * 256)) * 256
RMS_EPS = 1e-6
NEG_INF = -1e30

kernel_name = "yoco_stickbreak_dilated_hybrid"


def rms_norm(x, g):
    xf = x.astype(jnp.float32)
    y = xf * lax.rsqrt(jnp.mean(xf * xf, axis=-1, keepdims=True) + RMS_EPS)
    return (y * g.astype(jnp.float32)).astype(x.dtype)


def swiglu(x, w_gate_up, w_down):
    gate, up = jnp.split(x @ w_gate_up, 2, axis=-1)
    return (jax.nn.silu(gate) * up) @ w_down


def stick_breaking_attention(q, k, v):
    b, h, s, dh = q.shape
    nb = s // Q_BLOCK
    scale = dh ** -0.5
    kf = k.astype(jnp.float32)
    vf = v.astype(jnp.float32)
    qb = q.reshape(b, h, nb, Q_BLOCK, dh).transpose(2, 0, 1, 3, 4)
    key_pos = jnp.arange(s)

    def one_block(args):
        q_blk, j = args
        z = jnp.einsum('bhqd,bhkd->bhqk', q_blk.astype(jnp.float32), kf) * scale
        q_pos = j * Q_BLOCK + jnp.arange(Q_BLOCK)
        strict = key_pos[None, :] < q_pos[:, None]
        log_stay = jnp.where(strict, jax.nn.log_sigmoid(-z), 0.0)
        after = jnp.concatenate([log_stay[..., 1:], jnp.zeros_like(log_stay[..., :1])], axis=-1)
        log_a = jax.nn.log_sigmoid(z) + lax.cumsum(after, axis=3, reverse=True)
        a = jnp.where(strict, jnp.exp(log_a), 0.0)
        return jnp.einsum('bhqk,bhkd->bhqd', a, vf)

    out = lax.map(one_block, (qb, jnp.arange(nb)))
    return out.transpose(1, 2, 0, 3, 4).reshape(b, h, s, dh).astype(q.dtype)


def mixer_a(xn, w_qkv, w_o):
    b, s, _ = xn.shape
    qkv = (xn @ w_qkv).reshape(b, s, 3, N_HEADS, HEAD_DIM)
    q = qkv[:, :, 0].transpose(0, 2, 1, 3)
    k = qkv[:, :, 1].transpose(0, 2, 1, 3)
    v = qkv[:, :, 2].transpose(0, 2, 1, 3)
    o = stick_breaking_attention(q, k, v)
    return o.transpose(0, 2, 1, 3).reshape(b, s, D_MODEL) @ w_o


def to_strided_blocks(t, dilation):
    b, s, h, dh = t.shape
    L = s // dilation
    nb = -(-L // Q_BLOCK)
    t = t.reshape(b, L, dilation, h, dh).transpose(0, 2, 3, 1, 4)
    t = jnp.pad(t, ((0, 0), (0, 0), (0, 0), (0, nb * Q_BLOCK - L), (0, 0)))
    return t.reshape(b, dilation, h, nb, Q_BLOCK, dh)


def banded_keys(kb):
    prev = jnp.concatenate([jnp.zeros_like(kb[:, :, :, :1]), kb[:, :, :, :-1]], axis=3)
    return jnp.concatenate([prev, kb], axis=4)


def from_strided_blocks(t, s):
    b, d, h, nb, blk = t.shape[:5]
    rest = t.shape[5:]
    L = s // d
    t = t.reshape((b, d, h, nb * blk) + rest)[:, :, :, :L]
    t = jnp.moveaxis(t, 3, 1)
    return t.reshape((b, s, h) + rest)


def relative_bucket(distance):
    max_exact = NUM_BUCKETS // 2
    n = jnp.maximum(distance, 0)
    large = max_exact + (jnp.log(jnp.maximum(n, 1).astype(jnp.float32) / max_exact)
                         / math.log(MAX_DISTANCE / max_exact)
                         * (NUM_BUCKETS - max_exact)).astype(jnp.int32)
    large = jnp.minimum(large, NUM_BUCKETS - 1)
    return jnp.where(n < max_exact, n, large)


def dilated_group_attention(qb, kk, vv, rel_bias, window, dilation):
    nb = qb.shape[3]
    w_sub = window // dilation
    scale = HEAD_DIM ** -0.5
    i = jnp.arange(Q_BLOCK)[:, None]
    m = jnp.arange(2 * Q_BLOCK)[None, :]
    rel = Q_BLOCK + i - m
    band = (rel >= 0) & (rel <= w_sub)
    blk = jnp.arange(nb)[:, None, None]
    valid = band[None] & (blk * Q_BLOCK - Q_BLOCK + m[None] >= 0)
    bias = rel_bias[relative_bucket(rel * dilation)]
    bias = jnp.transpose(bias, (2, 0, 1))[:, None].astype(jnp.float32)
    s = jnp.einsum('brhnqd,brhnkd->brhnqk', qb.astype(jnp.float32), kk.astype(jnp.float32)) * scale + bias
    s = jnp.where(valid, s, NEG_INF)
    lse = jax.nn.logsumexp(s, axis=-1)
    p = jnp.exp(s - lse[..., None])
    o = jnp.einsum('brhnqk,brhnkd->brhnqd', p, vv.astype(jnp.float32))
    return o, lse


def shared_kv_from(x, g_kv, w_kv):
    b, s, _ = x.shape
    kv = (rms_norm(x, g_kv) @ w_kv).reshape(b, s, N_GROUPS, 2, N_HEADS, HEAD_DIM)
    out = []
    for g, (_, dilation) in enumerate(DILATED_GROUPS):
        out.append(banded_keys(to_strided_blocks(kv[:, :, g, 0], dilation)))
        out.append(banded_keys(to_strided_blocks(kv[:, :, g, 1], dilation)))
    return out


def mixer_b(xn, shared_kv, w_q, w_o, rel_bias):
    b, s, _ = xn.shape
    q = (xn @ w_q).reshape(b, s, N_GROUPS, N_HEADS, HEAD_DIM)
    outs, lses = [], []
    for g, (window, dilation) in enumerate(DILATED_GROUPS):
        qb = to_strided_blocks(q[:, :, g], dilation)
        o, lse = dilated_group_attention(qb, shared_kv[2 * g], shared_kv[2 * g + 1], rel_bias, window, dilation)
        outs.append(from_strided_blocks(o, s))
        lses.append(from_strided_blocks(lse, s))
    o = jnp.stack(outs, axis=0)
    lse = jnp.stack(lses, axis=0)
    wts = jax.nn.softmax(lse, axis=0)
    o = jnp.sum(wts[..., None] * o, axis=0)
    return o.reshape(b, s, D_MODEL).astype(xn.dtype) @ w_o


def setup_inputs(seed: int = 0) -> dict:
    key = jax.random.key(seed)
    ks = jax.random.split(key, 12)
    f32 = jnp.float32
    d = D_MODEL
    x = jax.random.normal(ks[0], (BATCH, SEQ, d), f32)
    norm_gains = 1.0 + 0.1 * jax.random.normal(ks[1], (DEPTH, 4, d), f32)
    w_qkv_a = jax.random.normal(ks[2], (N_A_LAYERS, d, 3 * d), f32) * d ** -0.5
    w_o_a = jax.random.normal(ks[3], (N_A_LAYERS, d, d), f32) * d ** -0.5
    g_kv = 1.0 + 0.1 * jax.random.normal(ks[4], (d,), f32)
    w_kv_b = jax.random.normal(ks[5], (d, N_GROUPS * 2 * d), f32) * d ** -0.5
    w_q_b = jax.random.normal(ks[6], (N_B_LAYERS, d, N_GROUPS * d), f32) * d ** -0.5
    w_o_b = jax.random.normal(ks[7], (N_B_LAYERS, d, d), f32) * d ** -0.5
    rel_bias = 0.5 * jax.random.normal(ks[8], (NUM_BUCKETS, N_HEADS), f32)
    w_gate_up = jax.random.normal(ks[9], (DEPTH, d, 2 * D_FF), f32) * d ** -0.5
    w_down = jax.random.normal(ks[10], (DEPTH, D_FF, d), f32) * D_FF ** -0.5
    return {"x": x, "norm_gains": norm_gains, "w_qkv_a": w_qkv_a, "w_o_a": w_o_a,
            "g_kv": g_kv, "w_kv_b": w_kv_b, "w_q_b": w_q_b, "w_o_b": w_o_b,
            "rel_bias": rel_bias, "w_gate_up": w_gate_up, "w_down": w_down}


def reference(x, norm_gains, w_qkv_a, w_o_a, g_kv, w_kv_b, w_q_b, w_o_b, rel_bias, w_gate_up, w_down):
    shared_kv = None
    for layer in range(DEPTH):
        g = norm_gains[layer]
        h = rms_norm(x, g[0])
        if layer < N_A_LAYERS:
            h = mixer_a(h, w_qkv_a[layer], w_o_a[layer])
        else:
            if layer == N_A_LAYERS:
                shared_kv = shared_kv_from(x, g_kv, w_kv_b)
            j = layer - N_A_LAYERS
            h = mixer_b(h, shared_kv, w_q_b[j], w_o_b[j], rel_bias)
        x = x + rms_norm(h, g[1])
        h = swiglu(rms_norm(x, g[2]), w_gate_up[layer], w_down[layer])
        x = x + rms_norm(h, g[3])
    return x
```

```python
import functools
import math

import jax
import jax.numpy as jnp
from jax import lax
from jax.experimental import pallas as pl
from jax.experimental.pallas import tpu as pltpu

D_MODEL = 1024
N_HEADS = 16
HEAD_DIM = D_MODEL // N_HEADS
LANES = 128
N_PAIRS = D_MODEL // LANES
Q_BLOCK = 128
DILATED_GROUPS = ((128, 1), (512, 4), (2048, 16))
N_GROUPS = len(DILATED_GROUPS)
NUM_BUCKETS = 32
MAX_DISTANCE = 2048
D_FF = -(-8 * D_MODEL // (3 * 256)) * 256
RMS_EPS = 1e-6
NEG_INF = -1e30
QK_SCALE = HEAD_DIM ** -0.5

VMEM_LIMIT_BYTES = 48 * 1024 * 1024
PROJ_TOKENS = 512
PROJ_COLS = 512
FFN_TOKENS = 512
FFN_COLS = 256

F32 = jnp.float32
BF16 = jnp.bfloat16


def _rms_scale(x, gain_row):
    ms = jnp.mean(x * x, axis=-1, keepdims=True)
    return x * lax.rsqrt(ms + RMS_EPS) * gain_row


def _split_heads(tile):
    lane = lax.broadcasted_iota(jnp.int32, tile.shape, 1)
    zero = jnp.zeros_like(tile)
    return jnp.concatenate([jnp.where(lane < HEAD_DIM, tile, zero),
                            jnp.where(lane >= HEAD_DIM, tile, zero)], axis=0)


def _merge_heads(stacked):
    rows = stacked.shape[0] // 2
    lane = lax.broadcasted_iota(jnp.int32, (rows, LANES), 1)
    return jnp.where(lane < HEAD_DIM, stacked[:rows], stacked[rows:])


def _norm_proj_kernel(x_ref, g_ref, w_ref, o_ref, *, pair_major):
    xn = _rms_scale(x_ref[...], g_ref[...]).astype(BF16)
    n_out = w_ref.shape[1]
    for c in range(n_out // PROJ_COLS):
        lo = c * PROJ_COLS
        res = jnp.dot(xn, w_ref[:, lo:lo + PROJ_COLS], preferred_element_type=F32).astype(o_ref.dtype)
        if pair_major:
            for k in range(PROJ_COLS // LANES):
                o_ref[lo // LANES + k] = res[:, k * LANES:(k + 1) * LANES]
        else:
            o_ref[:, lo:lo + PROJ_COLS] = res


def _norm_proj(x2d, gain, w, *, pair_major):
    t, d = x2d.shape
    n = w.shape[1]
    assert t % PROJ_TOKENS == 0 and n % PROJ_COLS == 0
    if pair_major:
        out_shape = jax.ShapeDtypeStruct((n // LANES, t, LANES), BF16)
        out_spec = pl.BlockSpec((n // LANES, PROJ_TOKENS, LANES), lambda i: (0, i, 0))
    else:
        out_shape = jax.ShapeDtypeStruct((t, n), BF16)
        out_spec = pl.BlockSpec((PROJ_TOKENS, n), lambda i: (i, 0))
    return pl.pallas_call(
        functools.partial(_norm_proj_kernel, pair_major=pair_major),
        grid=(t // PROJ_TOKENS,),
        in_specs=[pl.BlockSpec((PROJ_TOKENS, d), lambda i: (i, 0)),
                  pl.BlockSpec((1, d), lambda i: (0, 0)),
                  pl.BlockSpec((d, n), lambda i: (0, 0), pipeline_mode=pl.Buffered(1))],
        out_specs=out_spec,
        out_shape=out_shape,
        compiler_params=pltpu.CompilerParams(dimension_semantics=("parallel",),
                                             vmem_limit_bytes=VMEM_LIMIT_BYTES),
        name="norm_proj",
    )(x2d, gain.reshape(1, d), w)


def _stick_kernel(q_ref, k_ref, v_ref, u_ref, o_ref, acc_ref, carry_ref):
    qi = pl.program_id(2)
    qs = _split_heads(q_ref[0, 0]) * QK_SCALE
    u2 = u_ref[...]
    rows = 2 * Q_BLOCK

    def tile(k, v, carry, diagonal):
        z = lax.dot_general(qs, k, (((1,), (1,)), ((), ())), preferred_element_type=F32)
        soft = jnp.log(1.0 + jnp.exp(-jnp.abs(z)))
        log_beta = jnp.minimum(z, 0.0) - soft
        log_stay = -jnp.maximum(z, 0.0) - soft
        if diagonal:
            qrow = lax.broadcasted_iota(jnp.int32, (rows, Q_BLOCK), 0) % Q_BLOCK
            kcol = lax.broadcasted_iota(jnp.int32, (rows, Q_BLOCK), 1)
            strict = kcol < qrow
            log_stay = jnp.where(strict, log_stay, 0.0)
        hi = log_stay.astype(BF16)
        lo = (log_stay - hi.astype(F32)).astype(BF16)
        sums = (jnp.dot(hi, u2, preferred_element_type=F32)
                + jnp.dot(lo, u2, preferred_element_type=F32))
        a = jnp.exp(log_beta + sums[:, :Q_BLOCK] + carry)
        if diagonal:
            a = jnp.where(strict, a, 0.0)
        pv = jnp.dot(a.astype(BF16), v, preferred_element_type=F32)
        return pv, carry + sums[:, Q_BLOCK:]

    start = pl.multiple_of(qi * Q_BLOCK, Q_BLOCK)
    pv, carry = tile(k_ref[0, 0, pl.ds(start, Q_BLOCK), :], v_ref[0, 0, pl.ds(start, Q_BLOCK), :],
                     jnp.zeros((rows, LANES), F32), True)
    acc_ref[...] = pv
    carry_ref[...] = carry

    def body(i, _):
        off = pl.multiple_of((qi - 1 - i) * Q_BLOCK, Q_BLOCK)
        pv, carry = tile(k_ref[0, 0, pl.ds(off, Q_BLOCK), :], v_ref[0, 0, pl.ds(off, Q_BLOCK), :],
                         carry_ref[...], False)
        acc_ref[...] += pv
        carry_ref[...] = carry
        return 0

    lax.fori_loop(0, qi, body, 0)
    o_ref[0] = _merge_heads(acc_ref[...]).astype(o_ref.dtype)


def _stick_attention(qkv, batch, seq):
    qkv = qkv.reshape(3 * N_PAIRS, batch, seq, LANES)
    j = jnp.arange(Q_BLOCK)[:, None]
    s = jnp.arange(2 * Q_BLOCK)[None, :]
    u2 = ((j > s) | (s >= Q_BLOCK)).astype(BF16)
    nq = seq // Q_BLOCK
    return pl.pallas_call(
        _stick_kernel,
        grid=(batch, N_PAIRS, nq),
        in_specs=[pl.BlockSpec((1, 1, Q_BLOCK, LANES), lambda b, p, i: (p, b, i, 0)),
                  pl.BlockSpec((1, 1, seq, LANES), lambda b, p, i: (N_PAIRS + p, b, 0, 0)),
                  pl.BlockSpec((1, 1, seq, LANES), lambda b, p, i: (2 * N_PAIRS + p, b, 0, 0)),
                  pl.BlockSpec((Q_BLOCK, 2 * Q_BLOCK), lambda b, p, i: (0, 0))],
        out_specs=pl.BlockSpec((1, Q_BLOCK, LANES), lambda b, p, i: (b, i, p)),
        out_shape=jax.ShapeDtypeStruct((batch, seq, D_MODEL), BF16),
        scratch_shapes=[pltpu.VMEM((2 * Q_BLOCK, LANES), F32),
                        pltpu.VMEM((2 * Q_BLOCK, LANES), F32)],
        compiler_params=pltpu.CompilerParams(dimension_semantics=("parallel", "parallel", "parallel"),
                                             vmem_limit_bytes=VMEM_LIMIT_BYTES),
        name="stick_attention",
    )(qkv, qkv, qkv, u2)


def _relative_bucket(distance):
    max_exact = NUM_BUCKETS // 2
    n = jnp.maximum(distance, 0)
    large = max_exact + (jnp.log(jnp.maximum(n, 1).astype(F32) / max_exact)
                         / math.log(MAX_DISTANCE / max_exact)
                         * (NUM_BUCKETS - max_exact)).astype(jnp.int32)
    large = jnp.minimum(large, NUM_BUCKETS - 1)
    return jnp.where(n < max_exact, n, large)


def _band_bias(rel_bias, window, dilation):
    i = jnp.arange(Q_BLOCK)[:, None]
    m = jnp.arange(2 * Q_BLOCK)[None, :]
    rel = Q_BLOCK + i - m
    band = (rel >= 0) & (rel <= window // dilation)
    bias = rel_bias[_relative_bucket(rel * dilation)].astype(F32)
    bias = jnp.where(band[:, :, None], bias, NEG_INF)
    return jnp.transpose(bias, (2, 0, 1)).reshape(N_PAIRS, 2 * Q_BLOCK, 2 * Q_BLOCK)


def _dilated_kernel(*refs, has_prev):
    if has_prev:
        q_ref, kp_ref, kc_ref, vp_ref, vc_ref, bias_ref, oprev_ref, sprev_ref, o_ref, s_ref = refs
    else:
        q_ref, kp_ref, kc_ref, vp_ref, vc_ref, bias_ref, o_ref, s_ref = refs
    blk = pl.program_id(2)
    lane = lax.broadcasted_iota(jnp.int32, (Q_BLOCK, LANES), 1)
    key_col = lax.broadcasted_iota(jnp.int32, (2 * Q_BLOCK, 2 * Q_BLOCK), 1)
    key_ok = (key_col >= Q_BLOCK) | (blk > 0)
    stats = jnp.zeros((Q_BLOCK, LANES), F32)
    for hp in range(N_PAIRS):
        sl = slice(hp * LANES, (hp + 1) * LANES)
        qs = _split_heads(q_ref[0, :, sl]) * QK_SCALE
        kk = jnp.concatenate([kp_ref[0, :, sl], kc_ref[0, :, sl]], axis=0)
        vv = jnp.concatenate([vp_ref[0, :, sl], vc_ref[0, :, sl]], axis=0)
        s = lax.dot_general(qs, kk, (((1,), (1,)), ((), ())), preferred_element_type=F32) + bias_ref[hp]
        s = jnp.where(key_ok, s, NEG_INF)
        m = jnp.max(s, axis=-1, keepdims=True)
        p = jnp.exp(s - m)
        l = jnp.sum(p, axis=-1, keepdims=True)
        o_cur = jnp.dot(p.astype(BF16), vv, preferred_element_type=F32) / l
        lse = m + jnp.log(l)
        if has_prev:
            sp = sprev_ref[0]
            lp = jnp.concatenate(
                [jnp.sum(jnp.where(lane == 2 * hp, sp, 0.0), axis=-1, keepdims=True),
                 jnp.sum(jnp.where(lane == 2 * hp + 1, sp, 0.0), axis=-1, keepdims=True)], axis=0)
            top = jnp.maximum(lp, lse)
            new = top + jnp.log(jnp.exp(lp - top) + jnp.exp(lse - top))
            op = oprev_ref[0, :, sl].astype(F32)
            o_cur = jnp.concatenate([op, op], axis=0) * jnp.exp(lp - new) + o_cur * jnp.exp(lse - new)
            lse = new
        o_ref[0, :, sl] = _merge_heads(o_cur).astype(o_ref.dtype)
        stats = jnp.where(lane == 2 * hp, lse[:Q_BLOCK], stats)
        stats = jnp.where(lane == 2 * hp + 1, lse[Q_BLOCK:], stats)
    s_ref[0] = stats


def _dilated_group(q, kv, bias, prev, group, batch, seq):
    _, dilation = DILATED_GROUPS[group]
    strided = seq // dilation
    assert strided % Q_BLOCK == 0
    nb = strided // Q_BLOCK
    d = D_MODEL
    qv = q.reshape(batch, strided, dilation * N_GROUPS * d)
    kvv = kv.reshape(batch, strided, dilation * N_GROUPS * 2 * d)
    blk_spec = lambda cols, idx: pl.BlockSpec((1, Q_BLOCK, cols), idx)
    prev_blk = lambda i: jnp.maximum(i - 1, 0)
    in_specs = [
        blk_spec(d, lambda b, r, i: (b, i, r * N_GROUPS + group)),
        blk_spec(d, lambda b, r, i: (b, prev_blk(i), r * 2 * N_GROUPS + 2 * group)),
        blk_spec(d, lambda b, r, i: (b, i, r * 2 * N_GROUPS + 2 * group)),
        blk_spec(d, lambda b, r, i: (b, prev_blk(i), r * 2 * N_GROUPS + 2 * group + 1)),
        blk_spec(d, lambda b, r, i: (b, i, r * 2 * N_GROUPS + 2 * group + 1)),
        pl.BlockSpec((N_PAIRS, 2 * Q_BLOCK, 2 * Q_BLOCK), lambda b, r, i: (0, 0, 0)),
    ]
    args = [qv, kvv, kvv, kvv, kvv, bias]
    if prev is not None:
        in_specs += [blk_spec(d, lambda b, r, i: (b, i, r)),
                     blk_spec(LANES, lambda b, r, i: (b, i, r))]
        args += [prev[0].reshape(batch, strided, dilation * d),
                 prev[1].reshape(batch, strided, dilation * LANES)]
    o, stats = pl.pallas_call(
        functools.partial(_dilated_kernel, has_prev=prev is not None),
        grid=(batch, dilation, nb),
        in_specs=in_specs,
        out_specs=[blk_spec(d, lambda b, r, i: (b, i, r)),
                   blk_spec(LANES, lambda b, r, i: (b, i, r))],
        out_shape=[jax.ShapeDtypeStruct((batch, strided, dilation * d), BF16),
                   jax.ShapeDtypeStruct((batch, strided, dilation * LANES), F32)],
        compiler_params=pltpu.CompilerParams(dimension_semantics=("parallel", "parallel", "parallel"),
                                             vmem_limit_bytes=VMEM_LIMIT_BYTES),
        name=f"dilated_group{group}",
    )(*args)
    return o.reshape(batch, seq, d), stats.reshape(batch, seq, LANES)


def _oproj_ffn_kernel(o_ref, x_ref, wo_ref, g_ref, wg_ref, wu_ref, wd_ref, out_ref,
                      xmid_ref, xn_ref, acc_ref):
    c = pl.program_id(1)

    @pl.when(c == 0)
    def _():
        h = jnp.dot(o_ref[...], wo_ref[...], preferred_element_type=F32)
        xm = x_ref[...] + _rms_scale(h, g_ref[1:2, :])
        xmid_ref[...] = xm
        xn_ref[...] = _rms_scale(xm, g_ref[2:3, :]).astype(BF16)
        acc_ref[...] = jnp.zeros_like(acc_ref)

    xn = xn_ref[...]
    gate = jnp.dot(xn, wg_ref[...], preferred_element_type=F32)
    up = jnp.dot(xn, wu_ref[...], preferred_element_type=F32)
    hidden = (gate * jax.nn.sigmoid(gate) * up).astype(BF16)
    acc_ref[...] += jnp.dot(hidden, wd_ref[...], preferred_element_type=F32)

    @pl.when(c == pl.num_programs(1) - 1)
    def _():
        out_ref[...] = xmid_ref[...] + _rms_scale(acc_ref[...], g_ref[3:4, :])


def _oproj_ffn(o2d, x2d, w_o, gains, w_gate_up, w_down):
    t, d = x2d.shape
    assert t % FFN_TOKENS == 0 and D_FF % FFN_COLS == 0
    n_chunks = D_FF // FFN_COLS
    return pl.pallas_call(
        _oproj_ffn_kernel,
        grid=(t // FFN_TOKENS, n_chunks),
        in_specs=[pl.BlockSpec((FFN_TOKENS, d), lambda i, c: (i, 0)),
                  pl.BlockSpec((FFN_TOKENS, d), lambda i, c: (i, 0)),
                  pl.BlockSpec((d, d), lambda i, c: (0, 0), pipeline_mode=pl.Buffered(1)),
                  pl.BlockSpec((4, d), lambda i, c: (0, 0)),
                  pl.BlockSpec((d, FFN_COLS), lambda i, c: (0, c)),
                  pl.BlockSpec((d, FFN_COLS), lambda i, c: (0, n_chunks + c)),
                  pl.BlockSpec((FFN_COLS, d), lambda i, c: (c, 0))],
        out_specs=pl.BlockSpec((FFN_TOKENS, d), lambda i, c: (i, 0)),
        out_shape=jax.ShapeDtypeStruct((t, d), F32),
        scratch_shapes=[pltpu.VMEM((FFN_TOKENS, d), F32),
                        pltpu.VMEM((FFN_TOKENS, d), BF16),
                        pltpu.VMEM((FFN_TOKENS, d), F32)],
        compiler_params=pltpu.CompilerParams(dimension_semantics=("parallel", "arbitrary"),
                                             vmem_limit_bytes=VMEM_LIMIT_BYTES),
        name="oproj_ffn",
    )(o2d, x2d, w_o, gains, w_gate_up, w_gate_up, w_down)


def kernel(x, norm_gains, w_qkv_a, w_o_a, g_kv, w_kv_b, w_q_b, w_o_b, rel_bias, w_gate_up, w_down):
    batch, seq, d = x.shape
    depth = norm_gains.shape[0]
    n_a = w_qkv_a.shape[0]
    t = batch * seq
    x2d = x.reshape(t, d)
    shared_kv = None
    biases = None
    for layer in range(depth):
        g = norm_gains[layer]
        if layer < n_a:
            qkv = _norm_proj(x2d, g[0], w_qkv_a[layer].astype(BF16), pair_major=True)
            o = _stick_attention(qkv, batch, seq)
            w_o = w_o_a[layer]
        else:
            if shared_kv is None:
                shared_kv = _norm_proj(x2d, g_kv, w_kv_b.astype(BF16), pair_major=False)
                shared_kv = shared_kv.reshape(batch, seq, -1)
                biases = [_band_bias(rel_bias, w, dil) for w, dil in DILATED_GROUPS]
            j = layer - n_a
            q = _norm_proj(x2d, g[0], w_q_b[j].astype(BF16), pair_major=False).reshape(batch, seq, -1)
            prev = None
            for group in range(N_GROUPS):
                prev = _dilated_group(q, shared_kv, biases[group], prev, group, batch, seq)
            o = prev[0]
            w_o = w_o_b[j]
        x2d = _oproj_ffn(o.reshape(t, d), x2d, w_o.astype(BF16), g,
                         w_gate_up[layer].astype(BF16), w_down[layer].astype(BF16))
    return x2d.reshape(batch, seq, d)
```

```python
import functools
import math

import jax
import jax.numpy as jnp
from jax import lax
from jax.experimental import pallas as pl
from jax.experimental.pallas import tpu as pltpu

D_MODEL = 1024
N_HEADS = 16
HEAD_DIM = D_MODEL // N_HEADS
LANES = 128
N_PAIRS = D_MODEL // LANES
Q_BLOCK = 128
DILATED_GROUPS = ((128, 1), (512, 4), (2048, 16))
N_GROUPS = len(DILATED_GROUPS)
NUM_BUCKETS = 32
MAX_DISTANCE = 2048
D_FF = -(-8 * D_MODEL // (3 * 256)) * 256
RMS_EPS = 1e-6
NEG_INF = -1e30
QK_SCALE = HEAD_DIM ** -0.5

VMEM_LIMIT_BYTES = 48 * 1024 * 1024
PROJ_TOKENS = 512
PROJ_COLS = 512
FFN_TOKENS = 512
FFN_COLS = 256
STICK_QUERIES = 256
STICK_KEYS = 256

F32 = jnp.float32
BF16 = jnp.bfloat16


def _rms_scale(x, gain_row):
    ms = jnp.mean(x * x, axis=-1, keepdims=True)
    return x * lax.rsqrt(ms + RMS_EPS) * gain_row


def _split_heads(tile):
    lane = lax.broadcasted_iota(jnp.int32, tile.shape, 1)
    zero = jnp.zeros_like(tile)
    return jnp.concatenate([jnp.where(lane < HEAD_DIM, tile, zero),
                            jnp.where(lane >= HEAD_DIM, tile, zero)], axis=0)


def _merge_heads(stacked):
    rows = stacked.shape[0] // 2
    lane = lax.broadcasted_iota(jnp.int32, (rows, LANES), 1)
    return jnp.where(lane < HEAD_DIM, stacked[:rows], stacked[rows:])


def _norm_proj_kernel(x_ref, g_ref, w_ref, o_ref, *, pair_major):
    xn = _rms_scale(x_ref[...], g_ref[...]).astype(BF16)
    n_out = w_ref.shape[1]
    for c in range(n_out // PROJ_COLS):
        lo = c * PROJ_COLS
        res = jnp.dot(xn, w_ref[:, lo:lo + PROJ_COLS], preferred_element_type=F32).astype(o_ref.dtype)
        if pair_major:
            for k in range(PROJ_COLS // LANES):
                o_ref[lo // LANES + k] = res[:, k * LANES:(k + 1) * LANES]
        else:
            o_ref[:, lo:lo + PROJ_COLS] = res


def _norm_proj(x2d, gain, w, *, pair_major):
    t, d = x2d.shape
    n = w.shape[1]
    assert t % PROJ_TOKENS == 0 and n % PROJ_COLS == 0
    if pair_major:
        out_shape = jax.ShapeDtypeStruct((n // LANES, t, LANES), BF16)
        out_spec = pl.BlockSpec((n // LANES, PROJ_TOKENS, LANES), lambda i: (0, i, 0))
    else:
        out_shape = jax.ShapeDtypeStruct((t, n), BF16)
        out_spec = pl.BlockSpec((PROJ_TOKENS, n), lambda i: (i, 0))
    return pl.pallas_call(
        functools.partial(_norm_proj_kernel, pair_major=pair_major),
        grid=(t // PROJ_TOKENS,),
        in_specs=[pl.BlockSpec((PROJ_TOKENS, d), lambda i: (i, 0)),
                  pl.BlockSpec((1, d), lambda i: (0, 0)),
                  pl.BlockSpec((d, n), lambda i: (0, 0), pipeline_mode=pl.Buffered(1))],
        out_specs=out_spec,
        out_shape=out_shape,
        compiler_params=pltpu.CompilerParams(dimension_semantics=("parallel",),
                                             vmem_limit_bytes=VMEM_LIMIT_BYTES),
        name="norm_proj",
    )(x2d, gain.reshape(1, d), w)


def _stick_kernel(q_ref, k_ref, v_ref, u_ref, o_ref, acc_ref, carry_ref, z_ref, a_ref):
    qi = pl.program_id(2)
    qs = _split_heads(q_ref[0, 0]) * QK_SCALE

    def keys(ref, span_idx):
        return ref[0, 0, pl.ds(pl.multiple_of(span_idx * STICK_KEYS, STICK_KEYS), STICK_KEYS), :]

    def scores(span_idx):
        return lax.dot_general(qs, keys(k_ref, span_idx), (((1,), (1,)), ((), ())),
                               preferred_element_type=F32)

    def weights(z, carry, diagonal):
        cost = jnp.maximum(z, 0.0) + jnp.log(1.0 + jnp.exp(-jnp.abs(z)))
        if diagonal:
            qrow = lax.broadcasted_iota(jnp.int32, z.shape, 0) % STICK_QUERIES
            kcol = lax.broadcasted_iota(jnp.int32, z.shape, 1)
            strict = kcol < qrow
            cost = jnp.where(strict, cost, 0.0)
        w = z - jnp.dot(cost.astype(BF16), u_ref[...], preferred_element_type=F32)
        a = jnp.concatenate([jnp.exp(w[:, :LANES] - carry), jnp.exp(w[:, LANES:] - carry)], axis=1)
        if diagonal:
            a = jnp.where(strict, a, 0.0)
        return a.astype(BF16), carry + jnp.sum(cost, axis=-1, keepdims=True)

    a, carry = weights(scores(qi), jnp.zeros(carry_ref.shape, F32), True)
    a_ref[...] = a
    carry_ref[...] = carry
    acc_ref[...] = jnp.zeros_like(acc_ref)
    z_ref[...] = scores(jnp.maximum(qi - 1, 0))

    def body(i, _):
        cur = qi - 1 - i
        acc_ref[...] += jnp.dot(a_ref[...], keys(v_ref, cur + 1), preferred_element_type=F32)
        z = z_ref[...]
        z_ref[...] = scores(jnp.maximum(cur - 1, 0))
        a, carry = weights(z, carry_ref[...], False)
        a_ref[...] = a
        carry_ref[...] = carry
        return 0

    lax.fori_loop(0, qi, body, 0)
    acc = acc_ref[...] + jnp.dot(a_ref[...], keys(v_ref, 0), preferred_element_type=F32)
    o_ref[0] = _merge_heads(acc).astype(o_ref.dtype)


def _stick_attention(qkv, batch, seq):
    assert STICK_QUERIES == STICK_KEYS and seq % STICK_QUERIES == 0
    qkv = qkv.reshape(3 * N_PAIRS, batch, seq, LANES)
    assert STICK_KEYS == 2 * LANES
    j = jnp.arange(STICK_KEYS)[:, None]
    s = jnp.arange(STICK_KEYS)[None, :]
    uu = (j >= s).astype(BF16)
    nq = seq // STICK_QUERIES
    return pl.pallas_call(
        _stick_kernel,
        grid=(batch, N_PAIRS, nq),
        in_specs=[pl.BlockSpec((1, 1, STICK_QUERIES, LANES), lambda b, p, i: (p, b, i, 0)),
                  pl.BlockSpec((1, 1, seq, LANES), lambda b, p, i: (N_PAIRS + p, b, 0, 0)),
                  pl.BlockSpec((1, 1, seq, LANES), lambda b, p, i: (2 * N_PAIRS + p, b, 0, 0)),
                  pl.BlockSpec((STICK_KEYS, STICK_KEYS), lambda b, p, i: (0, 0))],
        out_specs=pl.BlockSpec((1, STICK_QUERIES, LANES), lambda b, p, i: (b, i, p)),
        out_shape=jax.ShapeDtypeStruct((batch, seq, D_MODEL), BF16),
        scratch_shapes=[pltpu.VMEM((2 * STICK_QUERIES, LANES), F32),
                        pltpu.VMEM((2 * STICK_QUERIES, LANES), F32),
                        pltpu.VMEM((2 * STICK_QUERIES, STICK_KEYS), F32),
                        pltpu.VMEM((2 * STICK_QUERIES, STICK_KEYS), BF16)],
        compiler_params=pltpu.CompilerParams(dimension_semantics=("parallel", "parallel", "parallel"),
                                             vmem_limit_bytes=VMEM_LIMIT_BYTES),
        name="stick_attention",
    )(qkv, qkv, qkv, uu)


def _relative_bucket(distance):
    max_exact = NUM_BUCKETS // 2
    n = jnp.maximum(distance, 0)
    large = max_exact + (jnp.log(jnp.maximum(n, 1).astype(F32) / max_exact)
                         / math.log(MAX_DISTANCE / max_exact)
                         * (NUM_BUCKETS - max_exact)).astype(jnp.int32)
    large = jnp.minimum(large, NUM_BUCKETS - 1)
    return jnp.where(n < max_exact, n, large)


def _band_bias(rel_bias, window, dilation):
    i = jnp.arange(Q_BLOCK)[:, None]
    m = jnp.arange(2 * Q_BLOCK)[None, :]
    rel = Q_BLOCK + i - m
    band = (rel >= 0) & (rel <= window // dilation)
    bias = rel_bias[_relative_bucket(rel * dilation)].astype(F32)
    bias = jnp.where(band[:, :, None], bias, NEG_INF)
    return jnp.transpose(bias, (2, 0, 1)).reshape(N_PAIRS, 2 * Q_BLOCK, 2 * Q_BLOCK)


def _dilated_kernel(*refs, has_prev):
    if has_prev:
        q_ref, kp_ref, kc_ref, vp_ref, vc_ref, bias_ref, oprev_ref, sprev_ref, o_ref, s_ref = refs
    else:
        q_ref, kp_ref, kc_ref, vp_ref, vc_ref, bias_ref, o_ref, s_ref = refs
    blk = pl.program_id(2)
    lane = lax.broadcasted_iota(jnp.int32, (Q_BLOCK, LANES), 1)
    key_col = lax.broadcasted_iota(jnp.int32, (2 * Q_BLOCK, 2 * Q_BLOCK), 1)
    key_ok = (key_col >= Q_BLOCK) | (blk > 0)
    stats = jnp.zeros((Q_BLOCK, LANES), F32)
    for hp in range(N_PAIRS):
        sl = slice(hp * LANES, (hp + 1) * LANES)
        qs = _split_heads(q_ref[0, :, sl]) * QK_SCALE
        kk = jnp.concatenate([kp_ref[0, :, sl], kc_ref[0, :, sl]], axis=0)
        vv = jnp.concatenate([vp_ref[0, :, sl], vc_ref[0, :, sl]], axis=0)
        s = lax.dot_general(qs, kk, (((1,), (1,)), ((), ())), preferred_element_type=F32) + bias_ref[hp]
        s = jnp.where(key_ok, s, NEG_INF)
        m = jnp.max(s, axis=-1, keepdims=True)
        p = jnp.exp(s - m)
        l = jnp.sum(p, axis=-1, keepdims=True)
        o_cur = jnp.dot(p.astype(BF16), vv, preferred_element_type=F32) / l
        lse = m + jnp.log(l)
        if has_prev:
            sp = sprev_ref[0]
            lp = jnp.concatenate(
                [jnp.sum(jnp.where(lane == 2 * hp, sp, 0.0), axis=-1, keepdims=True),
                 jnp.sum(jnp.where(lane == 2 * hp + 1, sp, 0.0), axis=-1, keepdims=True)], axis=0)
            top = jnp.maximum(lp, lse)
            new = top + jnp.log(jnp.exp(lp - top) + jnp.exp(lse - top))
            op = oprev_ref[0, :, sl].astype(F32)
            o_cur = jnp.concatenate([op, op], axis=0) * jnp.exp(lp - new) + o_cur * jnp.exp(lse - new)
            lse = new
        o_ref[0, :, sl] = _merge_heads(o_cur).astype(o_ref.dtype)
        stats = jnp.where(lane == 2 * hp, lse[:Q_BLOCK], stats)
        stats = jnp.where(lane == 2 * hp + 1, lse[Q_BLOCK:], stats)
    s_ref[0] = stats


def _dilated_group(q, kv, bias, prev, group, batch, seq):
    _, dilation = DILATED_GROUPS[group]
    strided = seq // dilation
    assert strided % Q_BLOCK == 0
    nb = strided // Q_BLOCK
    d = D_MODEL
    qv = q.reshape(batch, strided, dilation * N_GROUPS * d)
    kvv = kv.reshape(batch, strided, dilation * N_GROUPS * 2 * d)
    blk_spec = lambda cols, idx: pl.BlockSpec((1, Q_BLOCK, cols), idx)
    prev_blk = lambda i: jnp.maximum(i - 1, 0)
    in_specs = [
        blk_spec(d, lambda b, r, i: (b, i, r * N_GROUPS + group)),
        blk_spec(d, lambda b, r, i: (b, prev_blk(i), r * 2 * N_GROUPS + 2 * group)),
        blk_spec(d, lambda b, r, i: (b, i, r * 2 * N_GROUPS + 2 * group)),
        blk_spec(d, lambda b, r, i: (b, prev_blk(i), r * 2 * N_GROUPS + 2 * group + 1)),
        blk_spec(d, lambda b, r, i: (b, i, r * 2 * N_GROUPS + 2 * group + 1)),
        pl.BlockSpec((N_PAIRS, 2 * Q_BLOCK, 2 * Q_BLOCK), lambda b, r, i: (0, 0, 0)),
    ]
    args = [qv, kvv, kvv, kvv, kvv, bias]
    if prev is not None:
        in_specs += [blk_spec(d, lambda b, r, i: (b, i, r)),
                     blk_spec(LANES, lambda b, r, i: (b, i, r))]
        args += [prev[0].reshape(batch, strided, dilation * d),
                 prev[1].reshape(batch, strided, dilation * LANES)]
    o, stats = pl.pallas_call(
        functools.partial(_dilated_kernel, has_prev=prev is not None),
        grid=(batch, dilation, nb),
        in_specs=in_specs,
        out_specs=[blk_spec(d, lambda b, r, i: (b, i, r)),
                   blk_spec(LANES, lambda b, r, i: (b, i, r))],
        out_shape=[jax.ShapeDtypeStruct((batch, strided, dilation * d), BF16),
                   jax.ShapeDtypeStruct((batch, strided, dilation * LANES), F32)],
        compiler_params=pltpu.CompilerParams(dimension_semantics=("parallel", "parallel", "parallel"),
                                             vmem_limit_bytes=VMEM_LIMIT_BYTES),
        name=f"dilated_group{group}",
    )(*args)
    return o.reshape(batch, seq, d), stats.reshape(batch, seq, LANES)


def _oproj_ffn_kernel(o_ref, x_ref, wo_ref, g_ref, wg_ref, wu_ref, wd_ref, out_ref,
                      xmid_ref, xn_ref, acc_ref):
    c = pl.program_id(1)

    @pl.when(c == 0)
    def _():
        h = jnp.dot(o_ref[...], wo_ref[...], preferred_element_type=F32)
        xm = x_ref[...] + _rms_scale(h, g_ref[1:2, :])
        xmid_ref[...] = xm
        xn_ref[...] = _rms_scale(xm, g_ref[2:3, :]).astype(BF16)
        acc_ref[...] = jnp.zeros_like(acc_ref)

    xn = xn_ref[...]
    gate = jnp.dot(xn, wg_ref[...], preferred_element_type=F32)
    up = jnp.dot(xn, wu_ref[...], preferred_element_type=F32)
    hidden = (gate * jax.nn.sigmoid(gate) * up).astype(BF16)
    acc_ref[...] += jnp.dot(hidden, wd_ref[...], preferred_element_type=F32)

    @pl.when(c == pl.num_programs(1) - 1)
    def _():
        out_ref[...] = xmid_ref[...] + _rms_scale(acc_ref[...], g_ref[3:4, :])


def _oproj_ffn(o2d, x2d, w_o, gains, w_gate_up, w_down):
    t, d = x2d.shape
    assert t % FFN_TOKENS == 0 and D_FF % FFN_COLS == 0
    n_chunks = D_FF // FFN_COLS
    return pl.pallas_call(
        _oproj_ffn_kernel,
        grid=(t // FFN_TOKENS, n_chunks),
        in_specs=[pl.BlockSpec((FFN_TOKENS, d), lambda i, c: (i, 0)),
                  pl.BlockSpec((FFN_TOKENS, d), lambda i, c: (i, 0)),
                  pl.BlockSpec((d, d), lambda i, c: (0, 0), pipeline_mode=pl.Buffered(1)),
                  pl.BlockSpec((4, d), lambda i, c: (0, 0)),
                  pl.BlockSpec((d, FFN_COLS), lambda i, c: (0, c)),
                  pl.BlockSpec((d, FFN_COLS), lambda i, c: (0, n_chunks + c)),
                  pl.BlockSpec((FFN_COLS, d), lambda i, c: (c, 0))],
        out_specs=pl.BlockSpec((FFN_TOKENS, d), lambda i, c: (i, 0)),
        out_shape=jax.ShapeDtypeStruct((t, d), F32),
        scratch_shapes=[pltpu.VMEM((FFN_TOKENS, d), F32),
                        pltpu.VMEM((FFN_TOKENS, d), BF16),
                        pltpu.VMEM((FFN_TOKENS, d), F32)],
        compiler_params=pltpu.CompilerParams(dimension_semantics=("parallel", "arbitrary"),
                                             vmem_limit_bytes=VMEM_LIMIT_BYTES),
        name="oproj_ffn",
    )(o2d, x2d, w_o, gains, w_gate_up, w_gate_up, w_down)


def kernel(x, norm_gains, w_qkv_a, w_o_a, g_kv, w_kv_b, w_q_b, w_o_b, rel_bias, w_gate_up, w_down):
    batch, seq, d = x.shape
    depth = norm_gains.shape[0]
    n_a = w_qkv_a.shape[0]
    t = batch * seq
    x2d = x.reshape(t, d)
    shared_kv = None
    biases = None
    for layer in range(depth):
        g = norm_gains[layer]
        if layer < n_a:
            qkv = _norm_proj(x2d, g[0], w_qkv_a[layer].astype(BF16), pair_major=True)
            o = _stick_attention(qkv, batch, seq)
            w_o = w_o_a[layer]
        else:
            if shared_kv is None:
                shared_kv = _norm_proj(x2d, g_kv, w_kv_b.astype(BF16), pair_major=False)
                shared_kv = shared_kv.reshape(batch, seq, -1)
                biases = [_band_bias(rel_bias, w, dil) for w, dil in DILATED_GROUPS]
            j = layer - n_a
            q = _norm_proj(x2d, g[0], w_q_b[j].astype(BF16), pair_major=False).reshape(batch, seq, -1)
            prev = None
            for group in range(N_GROUPS):
                prev = _dilated_group(q, shared_kv, biases[group], prev, group, batch, seq)
            o = prev[0]
            w_o = w_o_b[j]
        x2d = _oproj_ffn(o.reshape(t, d), x2d, w_o.astype(BF16), g,
                         w_gate_up[layer].astype(BF16), w_down[layer].astype(BF16))
    return x2d.reshape(batch, seq, d)
```

```python
import functools
import math

import jax
import jax.numpy as jnp
from jax import lax
from jax.experimental import pallas as pl
from jax.experimental.pallas import tpu as pltpu

D_MODEL = 1024
N_HEADS = 16
HEAD_DIM = D_MODEL // N_HEADS
LANES = 128
N_PAIRS = D_MODEL // LANES
Q_BLOCK = 128
DILATED_GROUPS = ((128, 1), (512, 4), (2048, 16))
N_GROUPS = len(DILATED_GROUPS)
NUM_BUCKETS = 32
MAX_DISTANCE = 2048
D_FF = -(-8 * D_MODEL // (3 * 256)) * 256
RMS_EPS = 1e-6
NEG_INF = -1e30
QK_SCALE = HEAD_DIM ** -0.5

VMEM_LIMIT_BYTES = 48 * 1024 * 1024
PROJ_TOKENS = 512
PROJ_COLS = 512
FFN_TOKENS = 512
FFN_COLS = 256
STICK_QUERIES = 256
STICK_KEYS = 256

F32 = jnp.float32
BF16 = jnp.bfloat16


def _rms_scale(x, gain_row):
    ms = jnp.mean(x * x, axis=-1, keepdims=True)
    return x * lax.rsqrt(ms + RMS_EPS) * gain_row


def _split_heads(tile):
    lane = lax.broadcasted_iota(jnp.int32, tile.shape, 1)
    zero = jnp.zeros_like(tile)
    return jnp.concatenate([jnp.where(lane < HEAD_DIM, tile, zero),
                            jnp.where(lane >= HEAD_DIM, tile, zero)], axis=0)


def _merge_heads(stacked):
    rows = stacked.shape[0] // 2
    lane = lax.broadcasted_iota(jnp.int32, (rows, LANES), 1)
    return jnp.where(lane < HEAD_DIM, stacked[:rows], stacked[rows:])


def _norm_proj_kernel(x_ref, g_ref, w_ref, o_ref, *, pair_major):
    xn = _rms_scale(x_ref[...], g_ref[...]).astype(BF16)
    n_out = w_ref.shape[1]
    for c in range(n_out // PROJ_COLS):
        lo = c * PROJ_COLS
        res = jnp.dot(xn, w_ref[:, lo:lo + PROJ_COLS], preferred_element_type=F32).astype(o_ref.dtype)
        if pair_major:
            for k in range(PROJ_COLS // LANES):
                o_ref[lo // LANES + k] = res[:, k * LANES:(k + 1) * LANES]
        else:
            o_ref[:, lo:lo + PROJ_COLS] = res


def _norm_proj(x2d, gain, w, *, pair_major):
    t, d = x2d.shape
    n = w.shape[1]
    assert t % PROJ_TOKENS == 0 and n % PROJ_COLS == 0
    if pair_major:
        out_shape = jax.ShapeDtypeStruct((n // LANES, t, LANES), BF16)
        out_spec = pl.BlockSpec((n // LANES, PROJ_TOKENS, LANES), lambda i: (0, i, 0))
    else:
        out_shape = jax.ShapeDtypeStruct((t, n), BF16)
        out_spec = pl.BlockSpec((PROJ_TOKENS, n), lambda i: (i, 0))
    return pl.pallas_call(
        functools.partial(_norm_proj_kernel, pair_major=pair_major),
        grid=(t // PROJ_TOKENS,),
        in_specs=[pl.BlockSpec((PROJ_TOKENS, d), lambda i: (i, 0)),
                  pl.BlockSpec((1, d), lambda i: (0, 0)),
                  pl.BlockSpec((d, n), lambda i: (0, 0), pipeline_mode=pl.Buffered(1))],
        out_specs=out_spec,
        out_shape=out_shape,
        compiler_params=pltpu.CompilerParams(dimension_semantics=("parallel",),
                                             vmem_limit_bytes=VMEM_LIMIT_BYTES),
        name="norm_proj",
    )(x2d, gain.reshape(1, d), w)


def _norm_proj_groups_kernel(x_ref, g_ref, w_ref, *rest, chunks):
    out_refs, (xs_ref, xp_ref) = rest[:-2], rest[-2:]
    tm = x_ref.shape[0]
    xs = _rms_scale(x_ref[...], g_ref[...])
    for cb in range(N_PAIRS):
        xs_ref[cb] = xs[:, cb * LANES:(cb + 1) * LANES]
    for g, (_, dil) in enumerate(DILATED_GROUPS):
        n = tm // dil
        if dil == 1:
            xn = xs.astype(BF16)
        else:
            for r in range(dil):
                for cb in range(N_PAIRS):
                    xp_ref[r * n:(r + 1) * n, cb * LANES:(cb + 1) * LANES] = (
                        xs_ref[cb, pl.ds(r, n, stride=dil), :])
            xn = xp_ref[...].astype(BF16)
        for c in range(chunks):
            col = (g * chunks + c) * D_MODEL
            res = jnp.dot(xn, w_ref[:, col:col + D_MODEL], preferred_element_type=F32).astype(BF16)
            out = out_refs[g * chunks + c]
            for r in range(dil):
                out[0, r] = res[r * n:(r + 1) * n]


def _norm_proj_groups(x2d, gain, w, batch, seq):
    t, d = x2d.shape
    chunks = w.shape[1] // (N_GROUPS * d)
    tiles = seq // PROJ_TOKENS
    assert seq % PROJ_TOKENS == 0 and all(PROJ_TOKENS % (16 * dil) == 0 for _, dil in DILATED_GROUPS)
    out_shape, out_specs = [], []
    for _, dil in DILATED_GROUPS:
        for _ in range(chunks):
            out_shape.append(jax.ShapeDtypeStruct((batch, dil, seq // dil, d), BF16))
            out_specs.append(pl.BlockSpec((1, dil, PROJ_TOKENS // dil, d), lambda b, j: (b, 0, j, 0)))
    return pl.pallas_call(
        functools.partial(_norm_proj_groups_kernel, chunks=chunks),
        grid=(batch, tiles),
        in_specs=[pl.BlockSpec((PROJ_TOKENS, d), lambda b, j: (b * tiles + j, 0)),
                  pl.BlockSpec((1, d), lambda b, j: (0, 0)),
                  pl.BlockSpec(w.shape, lambda b, j: (0, 0), pipeline_mode=pl.Buffered(1))],
        out_specs=out_specs,
        out_shape=out_shape,
        scratch_shapes=[pltpu.VMEM((d // LANES, PROJ_TOKENS, LANES), F32),
                        pltpu.VMEM((PROJ_TOKENS, d), F32)],
        compiler_params=pltpu.CompilerParams(dimension_semantics=("parallel", "parallel"),
                                             vmem_limit_bytes=VMEM_LIMIT_BYTES),
        name="norm_proj_groups",
    )(x2d, gain.reshape(1, d), w)


def _stick_kernel(q_ref, k_ref, v_ref, u_ref, o_ref, acc_ref, carry_ref, z_ref, a_ref):
    qi = pl.program_id(2)
    qs = _split_heads(q_ref[0, 0]) * QK_SCALE

    def keys(ref, span_idx):
        return ref[0, 0, pl.ds(pl.multiple_of(span_idx * STICK_KEYS, STICK_KEYS), STICK_KEYS), :]

    def scores(span_idx):
        return lax.dot_general(qs, keys(k_ref, span_idx), (((1,), (1,)), ((), ())),
                               preferred_element_type=F32)

    def weights(z, carry, diagonal):
        cost = jnp.maximum(z, 0.0) + jnp.log(1.0 + jnp.exp(-jnp.abs(z)))
        if diagonal:
            qrow = lax.broadcasted_iota(jnp.int32, z.shape, 0) % STICK_QUERIES
            kcol = lax.broadcasted_iota(jnp.int32, z.shape, 1)
            strict = kcol < qrow
            cost = jnp.where(strict, cost, 0.0)
        w = z - jnp.dot(cost.astype(BF16), u_ref[...], preferred_element_type=F32)
        a = jnp.concatenate([jnp.exp(w[:, :LANES] - carry), jnp.exp(w[:, LANES:] - carry)], axis=1)
        if diagonal:
            a = jnp.where(strict, a, 0.0)
        return a.astype(BF16), carry + jnp.sum(cost, axis=-1, keepdims=True)

    a, carry = weights(scores(qi), jnp.zeros(carry_ref.shape, F32), True)
    a_ref[...] = a
    carry_ref[...] = carry
    acc_ref[...] = jnp.zeros_like(acc_ref)
    z_ref[...] = scores(jnp.maximum(qi - 1, 0))

    def body(i, _):
        cur = qi - 1 - i
        acc_ref[...] += jnp.dot(a_ref[...], keys(v_ref, cur + 1), preferred_element_type=F32)
        z = z_ref[...]
        z_ref[...] = scores(jnp.maximum(cur - 1, 0))
        a, carry = weights(z, carry_ref[...], False)
        a_ref[...] = a
        carry_ref[...] = carry
        return 0

    lax.fori_loop(0, qi, body, 0)
    acc = acc_ref[...] + jnp.dot(a_ref[...], keys(v_ref, 0), preferred_element_type=F32)
    o_ref[0] = _merge_heads(acc).astype(o_ref.dtype)


def _stick_attention(qkv, batch, seq):
    assert STICK_QUERIES == STICK_KEYS and seq % STICK_QUERIES == 0
    qkv = qkv.reshape(3 * N_PAIRS, batch, seq, LANES)
    assert STICK_KEYS == 2 * LANES
    j = jnp.arange(STICK_KEYS)[:, None]
    s = jnp.arange(STICK_KEYS)[None, :]
    uu = (j >= s).astype(BF16)
    nq = seq // STICK_QUERIES
    return pl.pallas_call(
        _stick_kernel,
        grid=(batch, N_PAIRS, nq),
        in_specs=[pl.BlockSpec((1, 1, STICK_QUERIES, LANES), lambda b, p, i: (p, b, i, 0)),
                  pl.BlockSpec((1, 1, seq, LANES), lambda b, p, i: (N_PAIRS + p, b, 0, 0)),
                  pl.BlockSpec((1, 1, seq, LANES), lambda b, p, i: (2 * N_PAIRS + p, b, 0, 0)),
                  pl.BlockSpec((STICK_KEYS, STICK_KEYS), lambda b, p, i: (0, 0))],
        out_specs=pl.BlockSpec((1, STICK_QUERIES, LANES), lambda b, p, i: (b, i, p)),
        out_shape=jax.ShapeDtypeStruct((batch, seq, D_MODEL), BF16),
        scratch_shapes=[pltpu.VMEM((2 * STICK_QUERIES, LANES), F32),
                        pltpu.VMEM((2 * STICK_QUERIES, LANES), F32),
                        pltpu.VMEM((2 * STICK_QUERIES, STICK_KEYS), F32),
                        pltpu.VMEM((2 * STICK_QUERIES, STICK_KEYS), BF16)],
        compiler_params=pltpu.CompilerParams(dimension_semantics=("parallel", "parallel", "parallel"),
                                             vmem_limit_bytes=VMEM_LIMIT_BYTES),
        name="stick_attention",
    )(qkv, qkv, qkv, uu)


def _relative_bucket(distance):
    max_exact = NUM_BUCKETS // 2
    n = jnp.maximum(distance, 0)
    large = max_exact + (jnp.log(jnp.maximum(n, 1).astype(F32) / max_exact)
                         / math.log(MAX_DISTANCE / max_exact)
                         * (NUM_BUCKETS - max_exact)).astype(jnp.int32)
    large = jnp.minimum(large, NUM_BUCKETS - 1)
    return jnp.where(n < max_exact, n, large)


def _band_bias(rel_bias, window, dilation):
    i = jnp.arange(Q_BLOCK)[:, None]
    m = jnp.arange(2 * Q_BLOCK)[None, :]
    rel = Q_BLOCK + i - m
    band = (rel >= 0) & (rel <= window // dilation)
    onehot = jax.nn.one_hot(_relative_bucket(rel * dilation), NUM_BUCKETS, dtype=F32)
    bias = jnp.einsum('qkb,bh->hqk', onehot, rel_bias.astype(F32), precision=lax.Precision.HIGHEST)
    bias = jnp.where(band[None], bias, NEG_INF)
    return bias.reshape(N_PAIRS, 2 * Q_BLOCK, 2 * Q_BLOCK)


def _dilated_kernel(*refs, with_prev):
    if with_prev:
        q_ref, kp_ref, kc_ref, vp_ref, vc_ref, bias_ref, o_ref, s_ref = refs
        blk = pl.program_id(2)
        key_col = lax.broadcasted_iota(jnp.int32, (2 * Q_BLOCK, 2 * Q_BLOCK), 1)
        key_ok = (key_col >= Q_BLOCK) | (blk > 0)
    else:
        q_ref, kc_ref, vc_ref, bias_ref, o_ref, s_ref = refs
    lane = lax.broadcasted_iota(jnp.int32, (Q_BLOCK, LANES), 1)
    stats = jnp.zeros((Q_BLOCK, LANES), F32)
    for hp in range(N_PAIRS):
        sl = slice(hp * LANES, (hp + 1) * LANES)
        qs = _split_heads(q_ref[0, 0, :, sl]) * QK_SCALE
        if with_prev:
            kk = jnp.concatenate([kp_ref[0, 0, :, sl], kc_ref[0, 0, :, sl]], axis=0)
            vv = jnp.concatenate([vp_ref[0, 0, :, sl], vc_ref[0, 0, :, sl]], axis=0)
        else:
            kk, vv = kc_ref[0, 0, :, sl], vc_ref[0, 0, :, sl]
        s = lax.dot_general(qs, kk, (((1,), (1,)), ((), ())), preferred_element_type=F32) + bias_ref[hp]
        if with_prev:
            s = jnp.where(key_ok, s, NEG_INF)
        m = jnp.max(s, axis=-1, keepdims=True)
        p = jnp.exp(s - m)
        l = jnp.sum(p, axis=-1, keepdims=True)
        o_cur = jnp.dot(p.astype(BF16), vv, preferred_element_type=F32) / l
        lse = m + jnp.log(l)
        o_ref[0, 0, :, sl] = _merge_heads(o_cur).astype(o_ref.dtype)
        stats = jnp.where(lane == 2 * hp, lse[:Q_BLOCK], stats)
        stats = jnp.where(lane == 2 * hp + 1, lse[Q_BLOCK:], stats)
    s_ref[0, 0] = stats


def _dilated_group(q, k, v, rel_bias, group):
    window, dilation = DILATED_GROUPS[group]
    batch, _, strided, d = q.shape
    assert strided % Q_BLOCK == 0
    nb = strided // Q_BLOCK
    with_prev = nb > 1
    bias = _band_bias(rel_bias, window, dilation)
    if not with_prev:
        bias = bias[:, :, Q_BLOCK:]
    cur = pl.BlockSpec((1, 1, Q_BLOCK, d), lambda b, r, i: (b, r, i, 0))
    prev = pl.BlockSpec((1, 1, Q_BLOCK, d), lambda b, r, i: (b, r, jnp.maximum(i - 1, 0), 0))
    bias_spec = pl.BlockSpec(bias.shape, lambda b, r, i: (0, 0, 0))
    if with_prev:
        in_specs, args = [cur, prev, cur, prev, cur, bias_spec], [q, k, k, v, v, bias]
    else:
        in_specs, args = [cur, cur, cur, bias_spec], [q, k, v, bias]
    return pl.pallas_call(
        functools.partial(_dilated_kernel, with_prev=with_prev),
        grid=(batch, dilation, nb),
        in_specs=in_specs,
        out_specs=[cur, pl.BlockSpec((1, 1, Q_BLOCK, LANES), lambda b, r, i: (b, r, i, 0))],
        out_shape=[jax.ShapeDtypeStruct((batch, dilation, strided, d), BF16),
                   jax.ShapeDtypeStruct((batch, dilation, strided, LANES), F32)],
        compiler_params=pltpu.CompilerParams(dimension_semantics=("parallel", "parallel", "parallel"),
                                             vmem_limit_bytes=VMEM_LIMIT_BYTES),
        name=f"dilated_group{group}",
    )(*args)


def _merge_groups(o_refs, s_refs, e_ref, o_tok_refs, s_tok_refs):
    outs, lses = [], []
    for g, (_, dil) in enumerate(DILATED_GROUPS):
        if dil == 1:
            outs.append(o_refs[g][0, 0].astype(F32))
            lses.append(s_refs[g][0, 0])
        else:
            n = o_refs[g].shape[2]
            for r in range(dil):
                rows = pl.ds(r, n, stride=dil)
                part = o_refs[g][0, r].astype(F32)
                for cb in range(N_PAIRS):
                    o_tok_refs[g][cb, rows, :] = part[:, cb * LANES:(cb + 1) * LANES]
                s_tok_refs[g][rows, :] = s_refs[g][0, r]
            outs.append(jnp.concatenate([o_tok_refs[g][cb] for cb in range(N_PAIRS)], axis=1))
            lses.append(s_tok_refs[g][...])
    top = functools.reduce(jnp.maximum, lses)
    es = [jnp.exp(lse - top) for lse in lses]
    inv = 1.0 / functools.reduce(jnp.add, es)
    merged = None
    for out, e in zip(outs, es):
        wt = e * inv
        hi = wt.astype(BF16)
        lo = (wt - hi.astype(F32)).astype(BF16)
        spread = jnp.dot(jnp.concatenate([hi, lo], axis=1), e_ref[...], preferred_element_type=F32)
        merged = spread * out if merged is None else merged + spread * out
    return merged.astype(BF16)


def _oproj_ffn_kernel(*refs, merge):
    if merge:
        o_refs, s_refs, e_ref = refs[0:N_GROUPS], refs[N_GROUPS:2 * N_GROUPS], refs[2 * N_GROUPS]
        refs = refs[2 * N_GROUPS + 1:]
        strided_groups = [g for g, (_, dil) in enumerate(DILATED_GROUPS) if dil > 1]
        n_tok = len(strided_groups)
        o_tok_refs = dict(zip(strided_groups, refs[-2 * n_tok:-n_tok]))
        s_tok_refs = dict(zip(strided_groups, refs[-n_tok:]))
        refs = refs[:-2 * n_tok]
    else:
        o_ref, refs = refs[0], refs[1:]
    x_ref, wo_ref, g_ref, wg_ref, wu_ref, wd_ref, out_ref, xmid_ref, xn_ref, acc_ref = refs
    c = pl.program_id(1)

    @pl.when(c == 0)
    def _():
        if merge:
            attn = _merge_groups(o_refs, s_refs, e_ref, o_tok_refs, s_tok_refs)
        else:
            attn = o_ref[...]
        h = jnp.dot(attn, wo_ref[...], preferred_element_type=F32)
        xm = x_ref[...] + _rms_scale(h, g_ref[1:2, :])
        xmid_ref[...] = xm
        xn_ref[...] = _rms_scale(xm, g_ref[2:3, :]).astype(BF16)
        acc_ref[...] = jnp.zeros_like(acc_ref)

    xn = xn_ref[...]
    gate = jnp.dot(xn, wg_ref[...], preferred_element_type=F32)
    up = jnp.dot(xn, wu_ref[...], preferred_element_type=F32)
    hidden = (gate * jax.nn.sigmoid(gate) * up).astype(BF16)
    acc_ref[...] += jnp.dot(hidden, wd_ref[...], preferred_element_type=F32)

    @pl.when(c == pl.num_programs(1) - 1)
    def _():
        out_ref[...] = xmid_ref[...] + _rms_scale(acc_ref[...], g_ref[3:4, :])


def _oproj_ffn(attn, x2d, w_o, gains, w_gate_up, w_down, seq):
    t, d = x2d.shape
    tm = FFN_TOKENS
    assert t % tm == 0 and seq % tm == 0 and D_FF % FFN_COLS == 0
    n_chunks = D_FF // FFN_COLS
    merge = not isinstance(attn, jax.Array)
    scratch = [pltpu.VMEM((tm, d), F32), pltpu.VMEM((tm, d), BF16), pltpu.VMEM((tm, d), F32)]
    if merge:
        tiles = seq // tm
        assert all(tm % (16 * dil) == 0 for _, dil in DILATED_GROUPS)
        blk = lambda dil, cols: pl.BlockSpec((1, dil, tm // dil, cols),
                                             lambda i, c: (i // tiles, 0, i % tiles, 0))
        head = jnp.arange(2 * LANES)[:, None] % LANES
        col = jnp.arange(d)[None, :] // HEAD_DIM
        spread = (head == col).astype(BF16)
        attn_specs = ([blk(dil, d) for _, dil in DILATED_GROUPS]
                      + [blk(dil, LANES) for _, dil in DILATED_GROUPS]
                      + [pl.BlockSpec(spread.shape, lambda i, c: (0, 0))])
        attn_args = [o for o, _ in attn] + [s for _, s in attn] + [spread]
        n_strided = sum(dil > 1 for _, dil in DILATED_GROUPS)
        scratch += ([pltpu.VMEM((d // LANES, tm, LANES), F32)] * n_strided
                    + [pltpu.VMEM((tm, LANES), F32)] * n_strided)
    else:
        attn_specs = [pl.BlockSpec((tm, d), lambda i, c: (i, 0))]
        attn_args = [attn]
    return pl.pallas_call(
        functools.partial(_oproj_ffn_kernel, merge=merge),
        grid=(t // tm, n_chunks),
        in_specs=attn_specs + [
            pl.BlockSpec((tm, d), lambda i, c: (i, 0)),
            pl.BlockSpec((d, d), lambda i, c: (0, 0), pipeline_mode=pl.Buffered(1)),
            pl.BlockSpec((4, d), lambda i, c: (0, 0)),
            pl.BlockSpec((d, FFN_COLS), lambda i, c: (0, c)),
            pl.BlockSpec((d, FFN_COLS), lambda i, c: (0, n_chunks + c)),
            pl.BlockSpec((FFN_COLS, d), lambda i, c: (c, 0))],
        out_specs=pl.BlockSpec((tm, d), lambda i, c: (i, 0)),
        out_shape=jax.ShapeDtypeStruct((t, d), F32),
        scratch_shapes=scratch,
        compiler_params=pltpu.CompilerParams(dimension_semantics=("parallel", "arbitrary"),
                                             vmem_limit_bytes=VMEM_LIMIT_BYTES),
        name="oproj_ffn_merge" if merge else "oproj_ffn",
    )(*attn_args, x2d, w_o, gains, w_gate_up, w_gate_up, w_down)


def kernel(x, norm_gains, w_qkv_a, w_o_a, g_kv, w_kv_b, w_q_b, w_o_b, rel_bias, w_gate_up, w_down):
    batch, seq, d = x.shape
    depth = norm_gains.shape[0]
    n_a = w_qkv_a.shape[0]
    t = batch * seq
    x2d = x.reshape(t, d)
    shared_kv = None
    for layer in range(depth):
        g = norm_gains[layer]
        if layer < n_a:
            qkv = _norm_proj(x2d, g[0], w_qkv_a[layer].astype(BF16), pair_major=True)
            attn = _stick_attention(qkv, batch, seq).reshape(t, d)
            w_o = w_o_a[layer]
        else:
            if shared_kv is None:
                shared_kv = _norm_proj_groups(x2d, g_kv, w_kv_b.astype(BF16), batch, seq)
            j = layer - n_a
            q = _norm_proj_groups(x2d, g[0], w_q_b[j].astype(BF16), batch, seq)
            attn = [_dilated_group(q[grp], shared_kv[2 * grp], shared_kv[2 * grp + 1], rel_bias, grp)
                    for grp in range(N_GROUPS)]
            w_o = w_o_b[j]
        x2d = _oproj_ffn(attn, x2d, w_o.astype(BF16), g,
                         w_gate_up[layer].astype(BF16), w_down[layer].astype(BF16), seq)
    return x2d.reshape(batch, seq, d)
```

```python
import functools
import math

import jax
import jax.numpy as jnp
from jax import lax
from jax.experimental import pallas as pl
from jax.experimental.pallas import tpu as pltpu

D_MODEL = 1024
N_HEADS = 16
HEAD_DIM = D_MODEL // N_HEADS
LANES = 128
N_PAIRS = D_MODEL // LANES
Q_BLOCK = 128
DILATED_GROUPS = ((128, 1), (512, 4), (2048, 16))
N_GROUPS = len(DILATED_GROUPS)
NUM_BUCKETS = 32
MAX_DISTANCE = 2048
D_FF = -(-8 * D_MODEL // (3 * 256)) * 256
RMS_EPS = 1e-6
NEG_INF = -1e30
QK_SCALE = HEAD_DIM ** -0.5

VMEM_LIMIT_BYTES = 56 * 1024 * 1024
PROJ_TOKENS = 512
PROJ_COLS = 512
FFN_TOKENS = 512
FFN_COLS = 256
STICK_QUERIES = 256
STICK_KEYS = 256

F32 = jnp.float32
BF16 = jnp.bfloat16


def _rms_scale(x, gain_row):
    ms = jnp.mean(x * x, axis=-1, keepdims=True)
    return x * lax.rsqrt(ms + RMS_EPS) * gain_row


def _split_heads(tile):
    lane = lax.broadcasted_iota(jnp.int32, tile.shape, 1)
    zero = jnp.zeros_like(tile)
    return jnp.concatenate([jnp.where(lane < HEAD_DIM, tile, zero),
                            jnp.where(lane >= HEAD_DIM, tile, zero)], axis=0)


def _merge_heads(stacked):
    rows = stacked.shape[0] // 2
    lane = lax.broadcasted_iota(jnp.int32, (rows, LANES), 1)
    return jnp.where(lane < HEAD_DIM, stacked[:rows], stacked[rows:])


def _norm_proj_kernel(x_ref, g_ref, w_ref, o_ref, *, pair_major):
    xn = _rms_scale(x_ref[...], g_ref[...]).astype(BF16)
    n_out = w_ref.shape[1]
    for c in range(n_out // PROJ_COLS):
        lo = c * PROJ_COLS
        res = jnp.dot(xn, w_ref[:, lo:lo + PROJ_COLS], preferred_element_type=F32).astype(o_ref.dtype)
        if pair_major:
            for k in range(PROJ_COLS // LANES):
                o_ref[lo // LANES + k] = res[:, k * LANES:(k + 1) * LANES]
        else:
            o_ref[:, lo:lo + PROJ_COLS] = res


def _norm_proj(x2d, gain, w, *, pair_major):
    t, d = x2d.shape
    n = w.shape[1]
    assert t % PROJ_TOKENS == 0 and n % PROJ_COLS == 0
    if pair_major:
        out_shape = jax.ShapeDtypeStruct((n // LANES, t, LANES), BF16)
        out_spec = pl.BlockSpec((n // LANES, PROJ_TOKENS, LANES), lambda i: (0, i, 0))
    else:
        out_shape = jax.ShapeDtypeStruct((t, n), BF16)
        out_spec = pl.BlockSpec((PROJ_TOKENS, n), lambda i: (i, 0))
    return pl.pallas_call(
        functools.partial(_norm_proj_kernel, pair_major=pair_major),
        grid=(t // PROJ_TOKENS,),
        in_specs=[pl.BlockSpec((PROJ_TOKENS, d), lambda i: (i, 0)),
                  pl.BlockSpec((1, d), lambda i: (0, 0)),
                  pl.BlockSpec((d, n), lambda i: (0, 0), pipeline_mode=pl.Buffered(1))],
        out_specs=out_spec,
        out_shape=out_shape,
        compiler_params=pltpu.CompilerParams(dimension_semantics=("parallel",),
                                             vmem_limit_bytes=VMEM_LIMIT_BYTES),
        name="norm_proj",
    )(x2d, gain.reshape(1, d), w)


def _norm_proj_groups_kernel(x_ref, g_ref, w_ref, *rest, chunks):
    out_refs, (xs_ref, xp_ref) = rest[:-2], rest[-2:]
    tm = x_ref.shape[0]
    xs = _rms_scale(x_ref[...], g_ref[...])
    for cb in range(N_PAIRS):
        xs_ref[cb] = xs[:, cb * LANES:(cb + 1) * LANES]
    for g, (_, dil) in enumerate(DILATED_GROUPS):
        n = tm // dil
        if dil == 1:
            xn = xs.astype(BF16)
        else:
            for r in range(dil):
                for cb in range(N_PAIRS):
                    xp_ref[r * n:(r + 1) * n, cb * LANES:(cb + 1) * LANES] = (
                        xs_ref[cb, pl.ds(r, n, stride=dil), :])
            xn = xp_ref[...].astype(BF16)
        for c in range(chunks):
            col = (g * chunks + c) * D_MODEL
            res = jnp.dot(xn, w_ref[:, col:col + D_MODEL], preferred_element_type=F32).astype(BF16)
            out = out_refs[g * chunks + c]
            for r in range(dil):
                out[0, r] = res[r * n:(r + 1) * n]


def _norm_proj_groups(x2d, gain, w, batch, seq):
    t, d = x2d.shape
    chunks = w.shape[1] // (N_GROUPS * d)
    tiles = seq // PROJ_TOKENS
    assert seq % PROJ_TOKENS == 0 and all(PROJ_TOKENS % (16 * dil) == 0 for _, dil in DILATED_GROUPS)
    out_shape, out_specs = [], []
    for _, dil in DILATED_GROUPS:
        for _ in range(chunks):
            out_shape.append(jax.ShapeDtypeStruct((batch, dil, seq // dil, d), BF16))
            out_specs.append(pl.BlockSpec((1, dil, PROJ_TOKENS // dil, d), lambda b, j: (b, 0, j, 0)))
    return pl.pallas_call(
        functools.partial(_norm_proj_groups_kernel, chunks=chunks),
        grid=(batch, tiles),
        in_specs=[pl.BlockSpec((PROJ_TOKENS, d), lambda b, j: (b * tiles + j, 0)),
                  pl.BlockSpec((1, d), lambda b, j: (0, 0)),
                  pl.BlockSpec(w.shape, lambda b, j: (0, 0), pipeline_mode=pl.Buffered(1))],
        out_specs=out_specs,
        out_shape=out_shape,
        scratch_shapes=[pltpu.VMEM((d // LANES, PROJ_TOKENS, LANES), F32),
                        pltpu.VMEM((PROJ_TOKENS, d), F32)],
        compiler_params=pltpu.CompilerParams(dimension_semantics=("parallel", "parallel"),
                                             vmem_limit_bytes=VMEM_LIMIT_BYTES),
        name="norm_proj_groups",
    )(x2d, gain.reshape(1, d), w)


def _stick_kernel(q_ref, k_ref, v_ref, u_ref, o_ref):
    rows = 2 * STICK_QUERIES

    def span(ref, idx):
        return ref[0, 0, idx * STICK_KEYS:(idx + 1) * STICK_KEYS, :]

    def weights(z, carry, diagonal):
        cost = jnp.maximum(z, 0.0) + jnp.log(1.0 + jnp.exp(-jnp.abs(z)))
        if diagonal:
            qrow = lax.broadcasted_iota(jnp.int32, z.shape, 0) % STICK_QUERIES
            kcol = lax.broadcasted_iota(jnp.int32, z.shape, 1)
            strict = kcol < qrow
            cost = jnp.where(strict, cost, 0.0)
        w = z - jnp.dot(cost.astype(BF16), u_ref[...], preferred_element_type=F32)
        a = jnp.concatenate([jnp.exp(w[:, :LANES] - carry), jnp.exp(w[:, LANES:] - carry)], axis=1)
        if diagonal:
            a = jnp.where(strict, a, 0.0)
        return a.astype(BF16), carry + jnp.sum(cost, axis=-1, keepdims=True)

    for qi in range(q_ref.shape[2] // STICK_QUERIES):
        qs = _split_heads(span(q_ref, qi)) * QK_SCALE
        carry = jnp.zeros((rows, LANES), F32)
        acc = jnp.zeros((rows, LANES), F32)
        for j in range(qi, -1, -1):
            z = lax.dot_general(qs, span(k_ref, j), (((1,), (1,)), ((), ())),
                                preferred_element_type=F32)
            a, carry = weights(z, carry, j == qi)
            acc = acc + jnp.dot(a, span(v_ref, j), preferred_element_type=F32)
        o_ref[0, qi * STICK_QUERIES:(qi + 1) * STICK_QUERIES, :] = _merge_heads(acc).astype(o_ref.dtype)


def _stick_attention(qkv, batch, seq):
    assert STICK_QUERIES == STICK_KEYS and seq % STICK_QUERIES == 0
    qkv = qkv.reshape(3 * N_PAIRS, batch, seq, LANES)
    assert STICK_KEYS == 2 * LANES
    j = jnp.arange(STICK_KEYS)[:, None]
    s = jnp.arange(STICK_KEYS)[None, :]
    uu = (j >= s).astype(BF16)
    return pl.pallas_call(
        _stick_kernel,
        grid=(batch, N_PAIRS),
        in_specs=[pl.BlockSpec((1, 1, seq, LANES), lambda b, p: (p, b, 0, 0)),
                  pl.BlockSpec((1, 1, seq, LANES), lambda b, p: (N_PAIRS + p, b, 0, 0)),
                  pl.BlockSpec((1, 1, seq, LANES), lambda b, p: (2 * N_PAIRS + p, b, 0, 0)),
                  pl.BlockSpec((STICK_KEYS, STICK_KEYS), lambda b, p: (0, 0))],
        out_specs=pl.BlockSpec((1, seq, LANES), lambda b, p: (b, 0, p)),
        out_shape=jax.ShapeDtypeStruct((batch, seq, D_MODEL), BF16),
        compiler_params=pltpu.CompilerParams(dimension_semantics=("parallel", "parallel"),
                                             vmem_limit_bytes=VMEM_LIMIT_BYTES),
        name="stick_attention",
    )(qkv, qkv, qkv, uu)


def _relative_bucket(distance):
    max_exact = NUM_BUCKETS // 2
    n = jnp.maximum(distance, 0)
    large = max_exact + (jnp.log(jnp.maximum(n, 1).astype(F32) / max_exact)
                         / math.log(MAX_DISTANCE / max_exact)
                         * (NUM_BUCKETS - max_exact)).astype(jnp.int32)
    large = jnp.minimum(large, NUM_BUCKETS - 1)
    return jnp.where(n < max_exact, n, large)


def _band_bias(rel_bias, window, dilation):
    i = jnp.arange(Q_BLOCK)[:, None]
    m = jnp.arange(2 * Q_BLOCK)[None, :]
    rel = Q_BLOCK + i - m
    band = (rel >= 0) & (rel <= window // dilation)
    onehot = jax.nn.one_hot(_relative_bucket(rel * dilation), NUM_BUCKETS, dtype=F32)
    bias = jnp.einsum('qkb,bh->hqk', onehot, rel_bias.astype(F32), precision=lax.Precision.HIGHEST)
    bias = jnp.where(band[None], bias, NEG_INF)
    return bias.reshape(N_PAIRS, 2 * Q_BLOCK, 2 * Q_BLOCK)


def _band_attend(q_tile, kk, vv, bias):
    qs = _split_heads(q_tile) * QK_SCALE
    s = lax.dot_general(qs, kk, (((1,), (1,)), ((), ())), preferred_element_type=F32) + bias
    m = jnp.max(s, axis=-1, keepdims=True)
    p = jnp.exp(s - m)
    l = jnp.sum(p, axis=-1, keepdims=True)
    o = jnp.dot(p.astype(BF16), vv, preferred_element_type=F32) / l
    return o, m + jnp.log(l)


def _dilated_kernel(*refs, with_prev):
    if with_prev:
        q_ref, kp_ref, kc_ref, vp_ref, vc_ref, bias_first_ref, bias_ref, o_ref, s_ref = refs
    else:
        q_ref, kc_ref, vc_ref, bias_ref, o_ref, s_ref = refs
    lane = lax.broadcasted_iota(jnp.int32, (Q_BLOCK, LANES), 1)
    for sub in range(2):
        stats = jnp.zeros((Q_BLOCK, LANES), F32)
        rows = slice(sub * Q_BLOCK, (sub + 1) * Q_BLOCK)
        for hp in range(N_PAIRS):
            sl = slice(hp * LANES, (hp + 1) * LANES)
            if not with_prev:
                q_tile, kk, vv = q_ref[0, sub, :, sl], kc_ref[0, sub, :, sl], vc_ref[0, sub, :, sl]
                bias = bias_ref[hp]
            elif sub == 0:
                q_tile = q_ref[0, 0, rows, sl]
                kk = jnp.concatenate([kp_ref[0, 0, :, sl], kc_ref[0, 0, rows, sl]], axis=0)
                vv = jnp.concatenate([vp_ref[0, 0, :, sl], vc_ref[0, 0, rows, sl]], axis=0)
                bias = bias_first_ref[0, hp]
            else:
                q_tile, kk, vv = q_ref[0, 0, rows, sl], kc_ref[0, 0, :, sl], vc_ref[0, 0, :, sl]
                bias = bias_ref[0, hp]
            o, lse = _band_attend(q_tile, kk, vv, bias)
            if with_prev:
                o_ref[0, 0, rows, sl] = _merge_heads(o).astype(o_ref.dtype)
            else:
                o_ref[0, sub, :, sl] = _merge_heads(o).astype(o_ref.dtype)
            stats = jnp.where(lane == 2 * hp, lse[:Q_BLOCK], stats)
            stats = jnp.where(lane == 2 * hp + 1, lse[Q_BLOCK:], stats)
        if with_prev:
            s_ref[0, 0, rows, :] = stats
        else:
            s_ref[0, sub] = stats


def _dilated_group(q, k, v, rel_bias, group):
    window, dilation = DILATED_GROUPS[group]
    batch, _, strided, d = q.shape
    assert strided % Q_BLOCK == 0
    nb = strided // Q_BLOCK
    with_prev = nb > 1
    bias = _band_bias(rel_bias, window, dilation)
    if with_prev:
        assert nb % 2 == 0
        key = jnp.arange(2 * Q_BLOCK)[None, None, :]
        bias = jnp.stack([jnp.where(key < Q_BLOCK, NEG_INF, bias), bias])
        grid = (batch, dilation, nb // 2)
        two = pl.BlockSpec((1, 1, 2 * Q_BLOCK, d), lambda b, r, i: (b, r, i, 0))
        prev = pl.BlockSpec((1, 1, Q_BLOCK, d), lambda b, r, i: (b, r, jnp.maximum(2 * i - 1, 0), 0))
        table = (1,) + bias.shape[1:]
        in_specs = [two, prev, two, prev, two,
                    pl.BlockSpec(table, lambda b, r, i: (jnp.minimum(i, 1), 0, 0, 0)),
                    pl.BlockSpec(table, lambda b, r, i: (1, 0, 0, 0))]
        args = [q, k, k, v, v, bias, bias]
        out_specs = [two, pl.BlockSpec((1, 1, 2 * Q_BLOCK, LANES), lambda b, r, i: (b, r, i, 0))]
    else:
        assert dilation % 2 == 0
        bias = bias[:, :, Q_BLOCK:]
        grid = (batch, dilation // 2, 1)
        two = pl.BlockSpec((1, 2, Q_BLOCK, d), lambda b, r, i: (b, r, 0, 0))
        in_specs = [two, two, two, pl.BlockSpec(bias.shape, lambda b, r, i: (0, 0, 0))]
        args = [q, k, v, bias]
        out_specs = [two, pl.BlockSpec((1, 2, Q_BLOCK, LANES), lambda b, r, i: (b, r, 0, 0))]
    return pl.pallas_call(
        functools.partial(_dilated_kernel, with_prev=with_prev),
        grid=grid,
        in_specs=in_specs,
        out_specs=out_specs,
        out_shape=[jax.ShapeDtypeStruct((batch, dilation, strided, d), BF16),
                   jax.ShapeDtypeStruct((batch, dilation, strided, LANES), F32)],
        compiler_params=pltpu.CompilerParams(dimension_semantics=("parallel", "parallel", "parallel"),
                                             vmem_limit_bytes=VMEM_LIMIT_BYTES),
        name=f"dilated_group{group}",
    )(*args)


def _merge_groups(o_refs, s_refs, e_ref, o_tok_refs, s_tok_refs):
    outs, lses = [], []
    for g, (_, dil) in enumerate(DILATED_GROUPS):
        if dil == 1:
            outs.append(o_refs[g][0, 0].astype(F32))
            lses.append(s_refs[g][0, 0])
        else:
            n = o_refs[g].shape[2]
            for r in range(dil):
                rows = pl.ds(r, n, stride=dil)
                part = o_refs[g][0, r].astype(F32)
                for cb in range(N_PAIRS):
                    o_tok_refs[g][cb, rows, :] = part[:, cb * LANES:(cb + 1) * LANES]
                s_tok_refs[g][rows, :] = s_refs[g][0, r]
            outs.append(jnp.concatenate([o_tok_refs[g][cb] for cb in range(N_PAIRS)], axis=1))
            lses.append(s_tok_refs[g][...])
    top = functools.reduce(jnp.maximum, lses)
    es = [jnp.exp(lse - top) for lse in lses]
    inv = 1.0 / functools.reduce(jnp.add, es)
    merged = None
    for out, e in zip(outs, es):
        wt = e * inv
        hi = wt.astype(BF16)
        lo = (wt - hi.astype(F32)).astype(BF16)
        spread = jnp.dot(jnp.concatenate([hi, lo], axis=1), e_ref[...], preferred_element_type=F32)
        merged = spread * out if merged is None else merged + spread * out
    return merged.astype(BF16)


def _oproj_ffn_kernel(*refs, merge):
    if merge:
        o_refs, s_refs, e_ref = refs[0:N_GROUPS], refs[N_GROUPS:2 * N_GROUPS], refs[2 * N_GROUPS]
        refs = refs[2 * N_GROUPS + 1:]
        strided_groups = [g for g, (_, dil) in enumerate(DILATED_GROUPS) if dil > 1]
        n_tok = len(strided_groups)
        o_tok_refs = dict(zip(strided_groups, refs[-2 * n_tok:-n_tok]))
        s_tok_refs = dict(zip(strided_groups, refs[-n_tok:]))
        refs = refs[:-2 * n_tok]
    else:
        o_ref, refs = refs[0], refs[1:]
    x_ref, wo_ref, g_ref, wgu_ref, wd_ref, out_ref, xmid_ref, xn_ref, hid_ref = refs
    if merge:
        attn = _merge_groups(o_refs, s_refs, e_ref, o_tok_refs, s_tok_refs)
    else:
        attn = o_ref[...]
    h = jnp.dot(attn, wo_ref[...], preferred_element_type=F32)
    xm = x_ref[...] + _rms_scale(h, g_ref[1:2, :])
    xmid_ref[...] = xm
    xn_ref[...] = _rms_scale(xm, g_ref[2:3, :]).astype(BF16)
    for c in range(D_FF // FFN_COLS):
        lo = c * FFN_COLS
        xn = xn_ref[...]
        gate = jnp.dot(xn, wgu_ref[:, lo:lo + FFN_COLS], preferred_element_type=F32)
        up = jnp.dot(xn, wgu_ref[:, D_FF + lo:D_FF + lo + FFN_COLS], preferred_element_type=F32)
        hid_ref[:, lo:lo + FFN_COLS] = (gate * jax.nn.sigmoid(gate) * up).astype(BF16)
    down = jnp.dot(hid_ref[...], wd_ref[...], preferred_element_type=F32)
    out_ref[...] = xmid_ref[...] + _rms_scale(down, g_ref[3:4, :])


def _oproj_ffn(attn, x2d, w_o, gains, w_gate_up, w_down, seq):
    t, d = x2d.shape
    tm = FFN_TOKENS
    assert t % tm == 0 and seq % tm == 0 and D_FF % FFN_COLS == 0
    merge = not isinstance(attn, jax.Array)
    scratch = [pltpu.VMEM((tm, d), F32), pltpu.VMEM((tm, d), BF16), pltpu.VMEM((tm, D_FF), BF16)]
    resident = lambda shape: pl.BlockSpec(shape, lambda i: (0, 0), pipeline_mode=pl.Buffered(1))
    if merge:
        tiles = seq // tm
        assert all(tm % (16 * dil) == 0 for _, dil in DILATED_GROUPS)
        blk = lambda dil, cols: pl.BlockSpec((1, dil, tm // dil, cols),
                                             lambda i: (i // tiles, 0, i % tiles, 0))
        head = jnp.arange(2 * LANES)[:, None] % LANES
        col = jnp.arange(d)[None, :] // HEAD_DIM
        spread = (head == col).astype(BF16)
        attn_specs = ([blk(dil, d) for _, dil in DILATED_GROUPS]
                      + [blk(dil, LANES) for _, dil in DILATED_GROUPS]
                      + [resident(spread.shape)])
        attn_args = [o for o, _ in attn] + [s for _, s in attn] + [spread]
        n_strided = sum(dil > 1 for _, dil in DILATED_GROUPS)
        scratch += ([pltpu.VMEM((d // LANES, tm, LANES), F32)] * n_strided
                    + [pltpu.VMEM((tm, LANES), F32)] * n_strided)
    else:
        attn_specs = [pl.BlockSpec((tm, d), lambda i: (i, 0))]
        attn_args = [attn]
    return pl.pallas_call(
        functools.partial(_oproj_ffn_kernel, merge=merge),
        grid=(t // tm,),
        in_specs=attn_specs + [
            pl.BlockSpec((tm, d), lambda i: (i, 0)),
            resident(w_o.shape),
            resident(gains.shape),
            resident(w_gate_up.shape),
            resident(w_down.shape)],
        out_specs=pl.BlockSpec((tm, d), lambda i: (i, 0)),
        out_shape=jax.ShapeDtypeStruct((t, d), F32),
        scratch_shapes=scratch,
        compiler_params=pltpu.CompilerParams(dimension_semantics=("parallel",),
                                             vmem_limit_bytes=VMEM_LIMIT_BYTES),
        name="oproj_ffn_merge" if merge else "oproj_ffn",
    )(*attn_args, x2d, w_o, gains, w_gate_up, w_down)


def kernel(x, norm_gains, w_qkv_a, w_o_a, g_kv, w_kv_b, w_q_b, w_o_b, rel_bias, w_gate_up, w_down):
    batch, seq, d = x.shape
    depth = norm_gains.shape[0]
    n_a = w_qkv_a.shape[0]
    t = batch * seq
    x2d = x.reshape(t, d)
    shared_kv = None
    for layer in range(depth):
        g = norm_gains[layer]
        if layer < n_a:
            qkv = _norm_proj(x2d, g[0], w_qkv_a[layer].astype(BF16), pair_major=True)
            attn = _stick_attention(qkv, batch, seq).reshape(t, d)
            w_o = w_o_a[layer]
        else:
            if shared_kv is None:
                shared_kv = _norm_proj_groups(x2d, g_kv, w_kv_b.astype(BF16), batch, seq)
            j = layer - n_a
            q = _norm_proj_groups(x2d, g[0], w_q_b[j].astype(BF16), batch, seq)
            attn = [_dilated_group(q[grp], shared_kv[2 * grp], shared_kv[2 * grp + 1], rel_bias, grp)
                    for grp in range(N_GROUPS)]
            w_o = w_o_b[j]
        x2d = _oproj_ffn(attn, x2d, w_o.astype(BF16), g,
                         w_gate_up[layer].astype(BF16), w_down[layer].astype(BF16), seq)
    return x2d.reshape(batch, seq, d)
```

```python
import functools
import math

import jax
import jax.numpy as jnp
from jax import lax
from jax.experimental import pallas as pl
from jax.experimental.pallas import tpu as pltpu

D_MODEL = 1024
N_HEADS = 16
HEAD_DIM = D_MODEL // N_HEADS
LANES = 128
N_PAIRS = D_MODEL // LANES
Q_BLOCK = 128
DILATED_GROUPS = ((128, 1), (512, 4), (2048, 16))
N_GROUPS = len(DILATED_GROUPS)
NUM_BUCKETS = 32
MAX_DISTANCE = 2048
D_FF = -(-8 * D_MODEL // (3 * 256)) * 256
RMS_EPS = 1e-6
NEG_INF = -1e30
QK_SCALE = HEAD_DIM ** -0.5

VMEM_LIMIT_BYTES = 56 * 1024 * 1024
PROJ_TOKENS = 512
PROJ_COLS = 512
FFN_TOKENS = 512
FFN_COLS = 256
STICK_QUERIES = 256
STICK_KEYS = 256
STICK_UNDERFLOW = 128.0

F32 = jnp.float32
BF16 = jnp.bfloat16


def _rms_scale(x, gain_row):
    ms = jnp.mean(x * x, axis=-1, keepdims=True)
    return x * lax.rsqrt(ms + RMS_EPS) * gain_row


def _split_heads(tile):
    lane = lax.broadcasted_iota(jnp.int32, tile.shape, 1)
    zero = jnp.zeros_like(tile)
    return jnp.concatenate([jnp.where(lane < HEAD_DIM, tile, zero),
                            jnp.where(lane >= HEAD_DIM, tile, zero)], axis=0)


def _merge_heads(stacked):
    rows = stacked.shape[0] // 2
    lane = lax.broadcasted_iota(jnp.int32, (rows, LANES), 1)
    return jnp.where(lane < HEAD_DIM, stacked[:rows], stacked[rows:])


def _norm_proj_kernel(x_ref, g_ref, w_ref, o_ref, *, pair_major):
    xn = _rms_scale(x_ref[...], g_ref[...]).astype(BF16)
    n_out = w_ref.shape[1]
    for c in range(n_out // PROJ_COLS):
        lo = c * PROJ_COLS
        res = jnp.dot(xn, w_ref[:, lo:lo + PROJ_COLS], preferred_element_type=F32).astype(o_ref.dtype)
        if pair_major:
            for k in range(PROJ_COLS // LANES):
                o_ref[lo // LANES + k] = res[:, k * LANES:(k + 1) * LANES]
        else:
            o_ref[:, lo:lo + PROJ_COLS] = res


def _norm_proj(x2d, gain, w, *, pair_major):
    t, d = x2d.shape
    n = w.shape[1]
    assert t % PROJ_TOKENS == 0 and n % PROJ_COLS == 0
    if pair_major:
        out_shape = jax.ShapeDtypeStruct((n // LANES, t, LANES), BF16)
        out_spec = pl.BlockSpec((n // LANES, PROJ_TOKENS, LANES), lambda i: (0, i, 0))
    else:
        out_shape = jax.ShapeDtypeStruct((t, n), BF16)
        out_spec = pl.BlockSpec((PROJ_TOKENS, n), lambda i: (i, 0))
    return pl.pallas_call(
        functools.partial(_norm_proj_kernel, pair_major=pair_major),
        grid=(t // PROJ_TOKENS,),
        in_specs=[pl.BlockSpec((PROJ_TOKENS, d), lambda i: (i, 0)),
                  pl.BlockSpec((1, d), lambda i: (0, 0)),
                  pl.BlockSpec((d, n), lambda i: (0, 0), pipeline_mode=pl.Buffered(1))],
        out_specs=out_spec,
        out_shape=out_shape,
        compiler_params=pltpu.CompilerParams(dimension_semantics=("parallel",),
                                             vmem_limit_bytes=VMEM_LIMIT_BYTES),
        name="norm_proj",
    )(x2d, gain.reshape(1, d), w)


def _norm_proj_groups_kernel(x_ref, g_ref, w_ref, *rest, chunks):
    out_refs, (xs_ref, xp_ref) = rest[:-2], rest[-2:]
    tm = x_ref.shape[0]
    xs = _rms_scale(x_ref[...], g_ref[...])
    for cb in range(N_PAIRS):
        xs_ref[cb] = xs[:, cb * LANES:(cb + 1) * LANES]
    for g, (_, dil) in enumerate(DILATED_GROUPS):
        n = tm // dil
        if dil == 1:
            xn = xs.astype(BF16)
        else:
            for r in range(dil):
                for cb in range(N_PAIRS):
                    xp_ref[r * n:(r + 1) * n, cb * LANES:(cb + 1) * LANES] = (
                        xs_ref[cb, pl.ds(r, n, stride=dil), :])
            xn = xp_ref[...].astype(BF16)
        for c in range(chunks):
            col = (g * chunks + c) * D_MODEL
            res = jnp.dot(xn, w_ref[:, col:col + D_MODEL], preferred_element_type=F32).astype(BF16)
            out = out_refs[g * chunks + c]
            for r in range(dil):
                out[0, r] = res[r * n:(r + 1) * n]


def _norm_proj_groups(x2d, gain, w, batch, seq):
    t, d = x2d.shape
    chunks = w.shape[1] // (N_GROUPS * d)
    tiles = seq // PROJ_TOKENS
    assert seq % PROJ_TOKENS == 0 and all(PROJ_TOKENS % (16 * dil) == 0 for _, dil in DILATED_GROUPS)
    out_shape, out_specs = [], []
    for _, dil in DILATED_GROUPS:
        for _ in range(chunks):
            out_shape.append(jax.ShapeDtypeStruct((batch, dil, seq // dil, d), BF16))
            out_specs.append(pl.BlockSpec((1, dil, PROJ_TOKENS // dil, d), lambda b, j: (b, 0, j, 0)))
    return pl.pallas_call(
        functools.partial(_norm_proj_groups_kernel, chunks=chunks),
        grid=(batch, tiles),
        in_specs=[pl.BlockSpec((PROJ_TOKENS, d), lambda b, j: (b * tiles + j, 0)),
                  pl.BlockSpec((1, d), lambda b, j: (0, 0)),
                  pl.BlockSpec(w.shape, lambda b, j: (0, 0), pipeline_mode=pl.Buffered(1))],
        out_specs=out_specs,
        out_shape=out_shape,
        scratch_shapes=[pltpu.VMEM((d // LANES, PROJ_TOKENS, LANES), F32),
                        pltpu.VMEM((PROJ_TOKENS, d), F32)],
        compiler_params=pltpu.CompilerParams(dimension_semantics=("parallel", "parallel"),
                                             vmem_limit_bytes=VMEM_LIMIT_BYTES),
        name="norm_proj_groups",
    )(x2d, gain.reshape(1, d), w)


def _stick_kernel(q_ref, k_ref, v_ref, u_ref, o_ref, acc_ref, carry_ref):
    rows = 2 * STICK_QUERIES
    nq = q_ref.shape[2] // STICK_QUERIES

    def span(ref, idx):
        return ref[0, 0, idx * STICK_KEYS:(idx + 1) * STICK_KEYS, :]

    def weights(z, carry, diagonal):
        cost = jnp.maximum(z, 0.0) + jnp.log(1.0 + jnp.exp(-jnp.abs(z)))
        if diagonal:
            qrow = lax.broadcasted_iota(jnp.int32, z.shape, 0) % STICK_QUERIES
            kcol = lax.broadcasted_iota(jnp.int32, z.shape, 1)
            strict = kcol < qrow
            cost = jnp.where(strict, cost, 0.0)
        w = z - jnp.dot(cost.astype(BF16), u_ref[...], preferred_element_type=F32)
        a = jnp.concatenate([jnp.exp(w[:, :LANES] - carry), jnp.exp(w[:, LANES:] - carry)], axis=1)
        if diagonal:
            a = jnp.where(strict, a, 0.0)
        return a.astype(BF16), carry + jnp.sum(cost, axis=-1, keepdims=True)

    def walk(qi, spans, acc, carry):
        qs = _split_heads(span(q_ref, qi)) * QK_SCALE
        for j in spans:
            z = lax.dot_general(qs, span(k_ref, j), (((1,), (1,)), ((), ())),
                                preferred_element_type=F32)
            a, carry = weights(z, carry, j == qi)
            acc = acc + jnp.dot(a, span(v_ref, j), preferred_element_type=F32)
        return acc, carry

    alive = {}
    for qi in range(nq):
        zeros = jnp.zeros((rows, LANES), F32)
        acc, carry = walk(qi, range(qi, max(qi - 2, -1), -1), zeros, zeros)
        acc_ref[qi] = acc
        if qi >= 2:
            carry_ref[qi] = carry
            alive[qi] = jnp.min(carry, axis=0, keepdims=True)[0, 0] < STICK_UNDERFLOW

    for qi in range(2, nq):
        @pl.when(alive[qi])
        def _(qi=qi):
            acc, _ = walk(qi, range(qi - 2, -1, -1), acc_ref[qi], carry_ref[qi])
            acc_ref[qi] = acc

    for qi in range(nq):
        o_ref[0, qi * STICK_QUERIES:(qi + 1) * STICK_QUERIES, :] = (
            _merge_heads(acc_ref[qi]).astype(o_ref.dtype))


def _stick_attention(qkv, batch, seq):
    assert STICK_QUERIES == STICK_KEYS and seq % STICK_QUERIES == 0
    qkv = qkv.reshape(3 * N_PAIRS, batch, seq, LANES)
    assert STICK_KEYS == 2 * LANES
    j = jnp.arange(STICK_KEYS)[:, None]
    s = jnp.arange(STICK_KEYS)[None, :]
    uu = (j >= s).astype(BF16)
    return pl.pallas_call(
        _stick_kernel,
        grid=(batch, N_PAIRS),
        in_specs=[pl.BlockSpec((1, 1, seq, LANES), lambda b, p: (p, b, 0, 0)),
                  pl.BlockSpec((1, 1, seq, LANES), lambda b, p: (N_PAIRS + p, b, 0, 0)),
                  pl.BlockSpec((1, 1, seq, LANES), lambda b, p: (2 * N_PAIRS + p, b, 0, 0)),
                  pl.BlockSpec((STICK_KEYS, STICK_KEYS), lambda b, p: (0, 0))],
        out_specs=pl.BlockSpec((1, seq, LANES), lambda b, p: (b, 0, p)),
        out_shape=jax.ShapeDtypeStruct((batch, seq, D_MODEL), BF16),
        scratch_shapes=[pltpu.VMEM((seq // STICK_QUERIES, 2 * STICK_QUERIES, LANES), F32),
                        pltpu.VMEM((seq // STICK_QUERIES, 2 * STICK_QUERIES, LANES), F32)],
        compiler_params=pltpu.CompilerParams(dimension_semantics=("parallel", "parallel"),
                                             vmem_limit_bytes=VMEM_LIMIT_BYTES),
        name="stick_attention",
    )(qkv, qkv, qkv, uu)


def _relative_bucket(distance):
    max_exact = NUM_BUCKETS // 2
    n = jnp.maximum(distance, 0)
    large = max_exact + (jnp.log(jnp.maximum(n, 1).astype(F32) / max_exact)
                         / math.log(MAX_DISTANCE / max_exact)
                         * (NUM_BUCKETS - max_exact)).astype(jnp.int32)
    large = jnp.minimum(large, NUM_BUCKETS - 1)
    return jnp.where(n < max_exact, n, large)


def _band_bias(rel_bias, window, dilation):
    i = jnp.arange(Q_BLOCK)[:, None]
    m = jnp.arange(2 * Q_BLOCK)[None, :]
    rel = Q_BLOCK + i - m
    band = (rel >= 0) & (rel <= window // dilation)
    onehot = jax.nn.one_hot(_relative_bucket(rel * dilation), NUM_BUCKETS, dtype=F32)
    bias = jnp.einsum('qkb,bh->hqk', onehot, rel_bias.astype(F32), precision=lax.Precision.HIGHEST)
    bias = jnp.where(band[None], bias, NEG_INF)
    return bias.reshape(N_PAIRS, 2 * Q_BLOCK, 2 * Q_BLOCK)


def _band_attend(q_tile, kk, vv, bias):
    qs = _split_heads(q_tile) * QK_SCALE
    s = lax.dot_general(qs, kk, (((1,), (1,)), ((), ())), preferred_element_type=F32) + bias
    m = jnp.max(s, axis=-1, keepdims=True)
    p = jnp.exp(s - m)
    l = jnp.sum(p, axis=-1, keepdims=True)
    o = jnp.dot(p.astype(BF16), vv, preferred_element_type=F32) / l
    return o, m + jnp.log(l)


def _dilated_kernel(*refs, with_prev):
    if with_prev:
        q_ref, kp_ref, kc_ref, vp_ref, vc_ref, bias_first_ref, bias_ref, o_ref, s_ref = refs
    else:
        q_ref, kc_ref, vc_ref, bias_ref, o_ref, s_ref = refs
    lane = lax.broadcasted_iota(jnp.int32, (Q_BLOCK, LANES), 1)
    for sub in range(2):
        stats = jnp.zeros((Q_BLOCK, LANES), F32)
        rows = slice(sub * Q_BLOCK, (sub + 1) * Q_BLOCK)
        for hp in range(N_PAIRS):
            sl = slice(hp * LANES, (hp + 1) * LANES)
            if not with_prev:
                q_tile, kk, vv = q_ref[0, sub, :, sl], kc_ref[0, sub, :, sl], vc_ref[0, sub, :, sl]
                bias = bias_ref[hp]
            elif sub == 0:
                q_tile = q_ref[0, 0, rows, sl]
                kk = jnp.concatenate([kp_ref[0, 0, :, sl], kc_ref[0, 0, rows, sl]], axis=0)
                vv = jnp.concatenate([vp_ref[0, 0, :, sl], vc_ref[0, 0, rows, sl]], axis=0)
                bias = bias_first_ref[0, hp]
            else:
                q_tile, kk, vv = q_ref[0, 0, rows, sl], kc_ref[0, 0, :, sl], vc_ref[0, 0, :, sl]
                bias = bias_ref[0, hp]
            o, lse = _band_attend(q_tile, kk, vv, bias)
            if with_prev:
                o_ref[0, 0, rows, sl] = _merge_heads(o).astype(o_ref.dtype)
            else:
                o_ref[0, sub, :, sl] = _merge_heads(o).astype(o_ref.dtype)
            stats = jnp.where(lane == 2 * hp, lse[:Q_BLOCK], stats)
            stats = jnp.where(lane == 2 * hp + 1, lse[Q_BLOCK:], stats)
        if with_prev:
            s_ref[0, 0, rows, :] = stats
        else:
            s_ref[0, sub] = stats


def _dilated_group(q, k, v, rel_bias, group):
    window, dilation = DILATED_GROUPS[group]
    batch, _, strided, d = q.shape
    assert strided % Q_BLOCK == 0
    nb = strided // Q_BLOCK
    with_prev = nb > 1
    bias = _band_bias(rel_bias, window, dilation)
    if with_prev:
        assert nb % 2 == 0
        key = jnp.arange(2 * Q_BLOCK)[None, None, :]
        bias = jnp.stack([jnp.where(key < Q_BLOCK, NEG_INF, bias), bias])
        grid = (batch, dilation, nb // 2)
        two = pl.BlockSpec((1, 1, 2 * Q_BLOCK, d), lambda b, r, i: (b, r, i, 0))
        prev = pl.BlockSpec((1, 1, Q_BLOCK, d), lambda b, r, i: (b, r, jnp.maximum(2 * i - 1, 0), 0))
        table = (1,) + bias.shape[1:]
        in_specs = [two, prev, two, prev, two,
                    pl.BlockSpec(table, lambda b, r, i: (jnp.minimum(i, 1), 0, 0, 0)),
                    pl.BlockSpec(table, lambda b, r, i: (1, 0, 0, 0))]
        args = [q, k, k, v, v, bias, bias]
        out_specs = [two, pl.BlockSpec((1, 1, 2 * Q_BLOCK, LANES), lambda b, r, i: (b, r, i, 0))]
    else:
        assert dilation % 2 == 0
        bias = bias[:, :, Q_BLOCK:]
        grid = (batch, dilation // 2, 1)
        two = pl.BlockSpec((1, 2, Q_BLOCK, d), lambda b, r, i: (b, r, 0, 0))
        in_specs = [two, two, two, pl.BlockSpec(bias.shape, lambda b, r, i: (0, 0, 0))]
        args = [q, k, v, bias]
        out_specs = [two, pl.BlockSpec((1, 2, Q_BLOCK, LANES), lambda b, r, i: (b, r, 0, 0))]
    return pl.pallas_call(
        functools.partial(_dilated_kernel, with_prev=with_prev),
        grid=grid,
        in_specs=in_specs,
        out_specs=out_specs,
        out_shape=[jax.ShapeDtypeStruct((batch, dilation, strided, d), BF16),
                   jax.ShapeDtypeStruct((batch, dilation, strided, LANES), F32)],
        compiler_params=pltpu.CompilerParams(dimension_semantics=("parallel", "parallel", "parallel"),
                                             vmem_limit_bytes=VMEM_LIMIT_BYTES),
        name=f"dilated_group{group}",
    )(*args)


def _merge_groups(o_refs, s_refs, e_ref, o_tok_refs, s_tok_refs):
    outs, lses = [], []
    for g, (_, dil) in enumerate(DILATED_GROUPS):
        if dil == 1:
            outs.append(o_refs[g][0, 0].astype(F32))
            lses.append(s_refs[g][0, 0])
        else:
            n = o_refs[g].shape[2]
            for r in range(dil):
                rows = pl.ds(r, n, stride=dil)
                part = o_refs[g][0, r].astype(F32)
                for cb in range(N_PAIRS):
                    o_tok_refs[g][cb, rows, :] = part[:, cb * LANES:(cb + 1) * LANES]
                s_tok_refs[g][rows, :] = s_refs[g][0, r]
            outs.append(jnp.concatenate([o_tok_refs[g][cb] for cb in range(N_PAIRS)], axis=1))
            lses.append(s_tok_refs[g][...])
    top = functools.reduce(jnp.maximum, lses)
    es = [jnp.exp(lse - top) for lse in lses]
    inv = 1.0 / functools.reduce(jnp.add, es)
    merged = None
    for out, e in zip(outs, es):
        wt = e * inv
        hi = wt.astype(BF16)
        lo = (wt - hi.astype(F32)).astype(BF16)
        spread = jnp.dot(jnp.concatenate([hi, lo], axis=1), e_ref[...], preferred_element_type=F32)
        merged = spread * out if merged is None else merged + spread * out
    return merged.astype(BF16)


def _oproj_ffn_kernel(*refs, merge):
    if merge:
        o_refs, s_refs, e_ref = refs[0:N_GROUPS], refs[N_GROUPS:2 * N_GROUPS], refs[2 * N_GROUPS]
        refs = refs[2 * N_GROUPS + 1:]
        strided_groups = [g for g, (_, dil) in enumerate(DILATED_GROUPS) if dil > 1]
        n_tok = len(strided_groups)
        o_tok_refs = dict(zip(strided_groups, refs[-2 * n_tok:-n_tok]))
        s_tok_refs = dict(zip(strided_groups, refs[-n_tok:]))
        refs = refs[:-2 * n_tok]
    else:
        o_ref, refs = refs[0], refs[1:]
    x_ref, wo_ref, g_ref, wgu_ref, wd_ref, out_ref, xmid_ref, xn_ref, hid_ref = refs
    if merge:
        attn = _merge_groups(o_refs, s_refs, e_ref, o_tok_refs, s_tok_refs)
    else:
        attn = o_ref[...]
    h = jnp.dot(attn, wo_ref[...], preferred_element_type=F32)
    xm = x_ref[...] + _rms_scale(h, g_ref[1:2, :])
    xmid_ref[...] = xm
    xn_ref[...] = _rms_scale(xm, g_ref[2:3, :]).astype(BF16)
    for c in range(D_FF // FFN_COLS):
        lo = c * FFN_COLS
        xn = xn_ref[...]
        gate = jnp.dot(xn, wgu_ref[:, lo:lo + FFN_COLS], preferred_element_type=F32)
        up = jnp.dot(xn, wgu_ref[:, D_FF + lo:D_FF + lo + FFN_COLS], preferred_element_type=F32)
        hid_ref[:, lo:lo + FFN_COLS] = (gate * jax.nn.sigmoid(gate) * up).astype(BF16)
    down = jnp.dot(hid_ref[...], wd_ref[...], preferred_element_type=F32)
    out_ref[...] = xmid_ref[...] + _rms_scale(down, g_ref[3:4, :])


def _oproj_ffn(attn, x2d, w_o, gains, w_gate_up, w_down, seq):
    t, d = x2d.shape
    tm = FFN_TOKENS
    assert t % tm == 0 and seq % tm == 0 and D_FF % FFN_COLS == 0
    merge = not isinstance(attn, jax.Array)
    scratch = [pltpu.VMEM((tm, d), F32), pltpu.VMEM((tm, d), BF16), pltpu.VMEM((tm, D_FF), BF16)]
    resident = lambda shape: pl.BlockSpec(shape, lambda i: (0, 0), pipeline_mode=pl.Buffered(1))
    if merge:
        tiles = seq // tm
        assert all(tm % (16 * dil) == 0 for _, dil in DILATED_GROUPS)
        blk = lambda dil, cols: pl.BlockSpec((1, dil, tm // dil, cols),
                                             lambda i: (i // tiles, 0, i % tiles, 0))
        head = jnp.arange(2 * LANES)[:, None] % LANES
        col = jnp.arange(d)[None, :] // HEAD_DIM
        spread = (head == col).astype(BF16)
        attn_specs = ([blk(dil, d) for _, dil in DILATED_GROUPS]
                      + [blk(dil, LANES) for _, dil in DILATED_GROUPS]
                      + [resident(spread.shape)])
        attn_args = [o for o, _ in attn] + [s for _, s in attn] + [spread]
        n_strided = sum(dil > 1 for _, dil in DILATED_GROUPS)
        scratch += ([pltpu.VMEM((d // LANES, tm, LANES), F32)] * n_strided
                    + [pltpu.VMEM((tm, LANES), F32)] * n_strided)
    else:
        attn_specs = [pl.BlockSpec((tm, d), lambda i: (i, 0))]
        attn_args = [attn]
    return pl.pallas_call(
        functools.partial(_oproj_ffn_kernel, merge=merge),
        grid=(t // tm,),
        in_specs=attn_specs + [
            pl.BlockSpec((tm, d), lambda i: (i, 0)),
            resident(w_o.shape),
            resident(gains.shape),
            resident(w_gate_up.shape),
            resident(w_down.shape)],
        out_specs=pl.BlockSpec((tm, d), lambda i: (i, 0)),
        out_shape=jax.ShapeDtypeStruct((t, d), F32),
        scratch_shapes=scratch,
        compiler_params=pltpu.CompilerParams(dimension_semantics=("parallel",),
                                             vmem_limit_bytes=VMEM_LIMIT_BYTES),
        name="oproj_ffn_merge" if merge else "oproj_ffn",
    )(*attn_args, x2d, w_o, gains, w_gate_up, w_down)


def kernel(x, norm_gains, w_qkv_a, w_o_a, g_kv, w_kv_b, w_q_b, w_o_b, rel_bias, w_gate_up, w_down):
    batch, seq, d = x.shape
    depth = norm_gains.shape[0]
    n_a = w_qkv_a.shape[0]
    t = batch * seq
    x2d = x.reshape(t, d)
    shared_kv = None
    for layer in range(depth):
        g = norm_gains[layer]
        if layer < n_a:
            qkv = _norm_proj(x2d, g[0], w_qkv_a[layer].astype(BF16), pair_major=True)
            attn = _stick_attention(qkv, batch, seq).reshape(t, d)
            w_o = w_o_a[layer]
        else:
            if shared_kv is None:
                shared_kv = _norm_proj_groups(x2d, g_kv, w_kv_b.astype(BF16), batch, seq)
            j = layer - n_a
            q = _norm_proj_groups(x2d, g[0], w_q_b[j].astype(BF16), batch, seq)
            attn = [_dilated_group(q[grp], shared_kv[2 * grp], shared_kv[2 * grp + 1], rel_bias, grp)
                    for grp in range(N_GROUPS)]
            w_o = w_o_b[j]
        x2d = _oproj_ffn(attn, x2d, w_o.astype(BF16), g,
                         w_gate_up[layer].astype(BF16), w_down[layer].astype(BF16), seq)
    return x2d.reshape(batch, seq, d)
```

```python
import functools
import math

import jax
import jax.numpy as jnp
from jax import lax
from jax.experimental import pallas as pl
from jax.experimental.pallas import tpu as pltpu

D_MODEL = 1024
N_HEADS = 16
HEAD_DIM = D_MODEL // N_HEADS
LANES = 128
N_PAIRS = D_MODEL // LANES
Q_BLOCK = 128
DILATED_GROUPS = ((128, 1), (512, 4), (2048, 16))
N_GROUPS = len(DILATED_GROUPS)
NUM_BUCKETS = 32
MAX_DISTANCE = 2048
D_FF = -(-8 * D_MODEL // (3 * 256)) * 256
RMS_EPS = 1e-6
NEG_INF = -1e30
QK_SCALE = HEAD_DIM ** -0.5

VMEM_LIMIT_BYTES = 56 * 1024 * 1024
PROJ_TOKENS = 512
PROJ_COLS = 512
FFN_TOKENS = 512
FFN_COLS = 256
DILATED_SUBS = 4
STICK_QUERIES = 256
STICK_KEYS = 256
STICK_UNDERFLOW = 128.0

F32 = jnp.float32
BF16 = jnp.bfloat16


def _rms_scale(x, gain_row):
    ms = jnp.mean(x * x, axis=-1, keepdims=True)
    return x * lax.rsqrt(ms + RMS_EPS) * gain_row


def _split_heads(tile):
    lane = lax.broadcasted_iota(jnp.int32, tile.shape, 1)
    zero = jnp.zeros_like(tile)
    return jnp.concatenate([jnp.where(lane < HEAD_DIM, tile, zero),
                            jnp.where(lane >= HEAD_DIM, tile, zero)], axis=0)


def _merge_heads(stacked):
    rows = stacked.shape[0] // 2
    lane = lax.broadcasted_iota(jnp.int32, (rows, LANES), 1)
    return jnp.where(lane < HEAD_DIM, stacked[:rows], stacked[rows:])


def _norm_proj_kernel(x_ref, g_ref, w_ref, o_ref, *, pair_major):
    xn = _rms_scale(x_ref[...], g_ref[...]).astype(BF16)
    n_out = w_ref.shape[1]
    for c in range(n_out // PROJ_COLS):
        lo = c * PROJ_COLS
        res = jnp.dot(xn, w_ref[:, lo:lo + PROJ_COLS], preferred_element_type=F32).astype(o_ref.dtype)
        if pair_major:
            for k in range(PROJ_COLS // LANES):
                o_ref[lo // LANES + k] = res[:, k * LANES:(k + 1) * LANES]
        else:
            o_ref[:, lo:lo + PROJ_COLS] = res


def _norm_proj(x2d, gain, w, *, pair_major):
    t, d = x2d.shape
    n = w.shape[1]
    assert t % PROJ_TOKENS == 0 and n % PROJ_COLS == 0
    if pair_major:
        out_shape = jax.ShapeDtypeStruct((n // LANES, t, LANES), BF16)
        out_spec = pl.BlockSpec((n // LANES, PROJ_TOKENS, LANES), lambda i: (0, i, 0))
    else:
        out_shape = jax.ShapeDtypeStruct((t, n), BF16)
        out_spec = pl.BlockSpec((PROJ_TOKENS, n), lambda i: (i, 0))
    return pl.pallas_call(
        functools.partial(_norm_proj_kernel, pair_major=pair_major),
        grid=(t // PROJ_TOKENS,),
        in_specs=[pl.BlockSpec((PROJ_TOKENS, d), lambda i: (i, 0)),
                  pl.BlockSpec((1, d), lambda i: (0, 0)),
                  pl.BlockSpec((d, n), lambda i: (0, 0), pipeline_mode=pl.Buffered(1))],
        out_specs=out_spec,
        out_shape=out_shape,
        compiler_params=pltpu.CompilerParams(dimension_semantics=("parallel",),
                                             vmem_limit_bytes=VMEM_LIMIT_BYTES),
        name="norm_proj",
    )(x2d, gain.reshape(1, d), w)


def _norm_proj_groups_kernel(x_ref, g_ref, w_ref, *rest, chunks):
    out_refs, (xs_ref, xp_ref) = rest[:-2], rest[-2:]
    tm = x_ref.shape[0]
    xs = _rms_scale(x_ref[...], g_ref[...])
    for cb in range(N_PAIRS):
        xs_ref[cb] = xs[:, cb * LANES:(cb + 1) * LANES]
    for g, (_, dil) in enumerate(DILATED_GROUPS):
        n = tm // dil
        if dil == 1:
            xn = xs.astype(BF16)
        else:
            for r in range(dil):
                for cb in range(N_PAIRS):
                    xp_ref[r * n:(r + 1) * n, cb * LANES:(cb + 1) * LANES] = (
                        xs_ref[cb, pl.ds(r, n, stride=dil), :])
            xn = xp_ref[...].astype(BF16)
        for c in range(chunks):
            col = (g * chunks + c) * D_MODEL
            res = jnp.dot(xn, w_ref[:, col:col + D_MODEL], preferred_element_type=F32).astype(BF16)
            out = out_refs[g * chunks + c]
            for r in range(dil):
                out[0, r] = res[r * n:(r + 1) * n]


def _norm_proj_groups(x2d, gain, w, batch, seq):
    t, d = x2d.shape
    chunks = w.shape[1] // (N_GROUPS * d)
    tiles = seq // PROJ_TOKENS
    assert seq % PROJ_TOKENS == 0 and all(PROJ_TOKENS % (16 * dil) == 0 for _, dil in DILATED_GROUPS)
    out_shape, out_specs = [], []
    for _, dil in DILATED_GROUPS:
        for _ in range(chunks):
            out_shape.append(jax.ShapeDtypeStruct((batch, dil, seq // dil, d), BF16))
            out_specs.append(pl.BlockSpec((1, dil, PROJ_TOKENS // dil, d), lambda b, j: (b, 0, j, 0)))
    return pl.pallas_call(
        functools.partial(_norm_proj_groups_kernel, chunks=chunks),
        grid=(batch, tiles),
        in_specs=[pl.BlockSpec((PROJ_TOKENS, d), lambda b, j: (b * tiles + j, 0)),
                  pl.BlockSpec((1, d), lambda b, j: (0, 0)),
                  pl.BlockSpec(w.shape, lambda b, j: (0, 0), pipeline_mode=pl.Buffered(1))],
        out_specs=out_specs,
        out_shape=out_shape,
        scratch_shapes=[pltpu.VMEM((d // LANES, PROJ_TOKENS, LANES), F32),
                        pltpu.VMEM((PROJ_TOKENS, d), F32)],
        compiler_params=pltpu.CompilerParams(dimension_semantics=("parallel", "parallel"),
                                             vmem_limit_bytes=VMEM_LIMIT_BYTES),
        name="norm_proj_groups",
    )(x2d, gain.reshape(1, d), w)


def _stick_kernel(q_ref, k_ref, v_ref, u_ref, o_ref, acc_ref, carry_ref):
    rows = 2 * STICK_QUERIES
    nq = q_ref.shape[2] // STICK_QUERIES

    def span(ref, idx):
        return ref[0, 0, idx * STICK_KEYS:(idx + 1) * STICK_KEYS, :]

    def weights(z, carry, diagonal):
        cost = jnp.maximum(z, 0.0) + jnp.log(1.0 + jnp.exp(-jnp.abs(z)))
        if diagonal:
            qrow = lax.broadcasted_iota(jnp.int32, z.shape, 0) % STICK_QUERIES
            kcol = lax.broadcasted_iota(jnp.int32, z.shape, 1)
            strict = kcol < qrow
            cost = jnp.where(strict, cost, 0.0)
        w = z - jnp.dot(cost.astype(BF16), u_ref[...], preferred_element_type=F32)
        a = jnp.concatenate([jnp.exp(w[:, :LANES] - carry), jnp.exp(w[:, LANES:] - carry)], axis=1)
        if diagonal:
            a = jnp.where(strict, a, 0.0)
        return a.astype(BF16), carry + jnp.sum(cost, axis=-1, keepdims=True)

    def walk(qi, spans, acc, carry):
        qs = _split_heads(span(q_ref, qi)) * QK_SCALE
        for j in spans:
            z = lax.dot_general(qs, span(k_ref, j), (((1,), (1,)), ((), ())),
                                preferred_element_type=F32)
            a, carry = weights(z, carry, j == qi)
            acc = acc + jnp.dot(a, span(v_ref, j), preferred_element_type=F32)
        return acc, carry

    alive = {}
    for qi in range(nq):
        zeros = jnp.zeros((rows, LANES), F32)
        acc, carry = walk(qi, range(qi, max(qi - 2, -1), -1), zeros, zeros)
        acc_ref[qi] = acc
        if qi >= 2:
            carry_ref[qi] = carry
            alive[qi] = jnp.min(carry, axis=0, keepdims=True)[0, 0] < STICK_UNDERFLOW

    for qi in range(2, nq):
        @pl.when(alive[qi])
        def _(qi=qi):
            acc, _ = walk(qi, range(qi - 2, -1, -1), acc_ref[qi], carry_ref[qi])
            acc_ref[qi] = acc

    for qi in range(nq):
        o_ref[0, qi * STICK_QUERIES:(qi + 1) * STICK_QUERIES, :] = (
            _merge_heads(acc_ref[qi]).astype(o_ref.dtype))


def _stick_attention(qkv, batch, seq):
    assert STICK_QUERIES == STICK_KEYS and seq % STICK_QUERIES == 0
    qkv = qkv.reshape(3 * N_PAIRS, batch, seq, LANES)
    assert STICK_KEYS == 2 * LANES
    j = jnp.arange(STICK_KEYS)[:, None]
    s = jnp.arange(STICK_KEYS)[None, :]
    uu = (j >= s).astype(BF16)
    return pl.pallas_call(
        _stick_kernel,
        grid=(batch, N_PAIRS),
        in_specs=[pl.BlockSpec((1, 1, seq, LANES), lambda b, p: (p, b, 0, 0)),
                  pl.BlockSpec((1, 1, seq, LANES), lambda b, p: (N_PAIRS + p, b, 0, 0)),
                  pl.BlockSpec((1, 1, seq, LANES), lambda b, p: (2 * N_PAIRS + p, b, 0, 0)),
                  pl.BlockSpec((STICK_KEYS, STICK_KEYS), lambda b, p: (0, 0))],
        out_specs=pl.BlockSpec((1, seq, LANES), lambda b, p: (b, 0, p)),
        out_shape=jax.ShapeDtypeStruct((batch, seq, D_MODEL), BF16),
        scratch_shapes=[pltpu.VMEM((seq // STICK_QUERIES, 2 * STICK_QUERIES, LANES), F32),
                        pltpu.VMEM((seq // STICK_QUERIES, 2 * STICK_QUERIES, LANES), F32)],
        compiler_params=pltpu.CompilerParams(dimension_semantics=("parallel", "parallel"),
                                             vmem_limit_bytes=VMEM_LIMIT_BYTES),
        name="stick_attention",
    )(qkv, qkv, qkv, uu)


def _relative_bucket(distance):
    max_exact = NUM_BUCKETS // 2
    n = jnp.maximum(distance, 0)
    large = max_exact + (jnp.log(jnp.maximum(n, 1).astype(F32) / max_exact)
                         / math.log(MAX_DISTANCE / max_exact)
                         * (NUM_BUCKETS - max_exact)).astype(jnp.int32)
    large = jnp.minimum(large, NUM_BUCKETS - 1)
    return jnp.where(n < max_exact, n, large)


def _band_bias(rel_bias, window, dilation):
    i = jnp.arange(Q_BLOCK)[:, None]
    m = jnp.arange(2 * Q_BLOCK)[None, :]
    rel = Q_BLOCK + i - m
    band = (rel >= 0) & (rel <= window // dilation)
    onehot = jax.nn.one_hot(_relative_bucket(rel * dilation), NUM_BUCKETS, dtype=F32)
    bias = jnp.einsum('qkb,bh->hqk', onehot, rel_bias.astype(F32), precision=lax.Precision.HIGHEST)
    bias = jnp.where(band[None], bias, NEG_INF)
    return bias.reshape(N_PAIRS, 2 * Q_BLOCK, 2 * Q_BLOCK)


def _band_attend(q_tile, kk, vv, bias):
    qs = _split_heads(q_tile) * QK_SCALE
    s = lax.dot_general(qs, kk, (((1,), (1,)), ((), ())), preferred_element_type=F32) + bias
    m = jnp.max(s, axis=-1, keepdims=True)
    p = jnp.exp(s - m)
    l = jnp.sum(p, axis=-1, keepdims=True)
    return jnp.dot(p.astype(BF16), vv, preferred_element_type=F32), m, l


def _dilated_kernel(*refs, with_prev):
    if with_prev:
        q_ref, kp_ref, kc_ref, vp_ref, vc_ref, bias_first_ref, bias_ref, o_ref, s_ref = refs
    else:
        q_ref, kc_ref, vc_ref, bias_ref, o_ref, s_ref = refs
    lane = lax.broadcasted_iota(jnp.int32, (Q_BLOCK, LANES), 1)
    for sub in range(DILATED_SUBS):
        m_stats = jnp.zeros((Q_BLOCK, LANES), F32)
        l_stats = jnp.ones((Q_BLOCK, LANES), F32)
        rows = slice(sub * Q_BLOCK, (sub + 1) * Q_BLOCK)
        for hp in range(N_PAIRS):
            sl = slice(hp * LANES, (hp + 1) * LANES)
            if not with_prev:
                q_tile, kk, vv = q_ref[0, sub, :, sl], kc_ref[0, sub, :, sl], vc_ref[0, sub, :, sl]
                bias = bias_ref[hp]
            elif sub == 0:
                q_tile = q_ref[0, 0, rows, sl]
                kk = jnp.concatenate([kp_ref[0, 0, :, sl], kc_ref[0, 0, rows, sl]], axis=0)
                vv = jnp.concatenate([vp_ref[0, 0, :, sl], vc_ref[0, 0, rows, sl]], axis=0)
                bias = bias_first_ref[0, hp]
            else:
                keys = slice((sub - 1) * Q_BLOCK, (sub + 1) * Q_BLOCK)
                q_tile, kk, vv = q_ref[0, 0, rows, sl], kc_ref[0, 0, keys, sl], vc_ref[0, 0, keys, sl]
                bias = bias_ref[0, hp]
            o, m, l = _band_attend(q_tile, kk, vv, bias)
            if with_prev:
                o_ref[0, 0, rows, sl] = _merge_heads(o).astype(o_ref.dtype)
            else:
                o_ref[0, sub, :, sl] = _merge_heads(o).astype(o_ref.dtype)
            for head, half in ((2 * hp, slice(0, Q_BLOCK)), (2 * hp + 1, slice(Q_BLOCK, None))):
                m_stats = jnp.where(lane == head, m[half], m_stats)
                l_stats = jnp.where(lane == head, l[half], l_stats)
        stats = jnp.concatenate([m_stats, l_stats], axis=1)
        if with_prev:
            s_ref[0, 0, rows, :] = stats
        else:
            s_ref[0, sub] = stats


def _dilated_group(q, k, v, rel_bias, group):
    window, dilation = DILATED_GROUPS[group]
    batch, _, strided, d = q.shape
    assert strided % Q_BLOCK == 0
    nb = strided // Q_BLOCK
    with_prev = nb > 1
    bias = _band_bias(rel_bias, window, dilation)
    if with_prev:
        assert nb % DILATED_SUBS == 0
        key = jnp.arange(2 * Q_BLOCK)[None, None, :]
        bias = jnp.stack([jnp.where(key < Q_BLOCK, NEG_INF, bias), bias])
        grid = (batch, dilation, nb // DILATED_SUBS)
        two = pl.BlockSpec((1, 1, DILATED_SUBS * Q_BLOCK, d), lambda b, r, i: (b, r, i, 0))
        prev = pl.BlockSpec((1, 1, Q_BLOCK, d),
                            lambda b, r, i: (b, r, jnp.maximum(DILATED_SUBS * i - 1, 0), 0))
        table = (1,) + bias.shape[1:]
        in_specs = [two, prev, two, prev, two,
                    pl.BlockSpec(table, lambda b, r, i: (jnp.minimum(i, 1), 0, 0, 0)),
                    pl.BlockSpec(table, lambda b, r, i: (1, 0, 0, 0))]
        args = [q, k, k, v, v, bias, bias]
        out_specs = [two, pl.BlockSpec((1, 1, DILATED_SUBS * Q_BLOCK, 2 * LANES),
                                       lambda b, r, i: (b, r, i, 0))]
    else:
        assert dilation % DILATED_SUBS == 0
        bias = bias[:, :, Q_BLOCK:]
        grid = (batch, dilation // DILATED_SUBS, 1)
        two = pl.BlockSpec((1, DILATED_SUBS, Q_BLOCK, d), lambda b, r, i: (b, r, 0, 0))
        in_specs = [two, two, two, pl.BlockSpec(bias.shape, lambda b, r, i: (0, 0, 0))]
        args = [q, k, v, bias]
        out_specs = [two, pl.BlockSpec((1, DILATED_SUBS, Q_BLOCK, 2 * LANES), lambda b, r, i: (b, r, 0, 0))]
    return pl.pallas_call(
        functools.partial(_dilated_kernel, with_prev=with_prev),
        grid=grid,
        in_specs=in_specs,
        out_specs=out_specs,
        out_shape=[jax.ShapeDtypeStruct((batch, dilation, strided, d), BF16),
                   jax.ShapeDtypeStruct((batch, dilation, strided, 2 * LANES), F32)],
        compiler_params=pltpu.CompilerParams(dimension_semantics=("parallel", "parallel", "parallel"),
                                             vmem_limit_bytes=VMEM_LIMIT_BYTES),
        name=f"dilated_group{group}",
    )(*args)


def _merge_groups(o_refs, s_refs, e_ref, o_tok_refs, s_tok_refs):
    outs, tops, dens = [], [], []
    for g, (_, dil) in enumerate(DILATED_GROUPS):
        if dil == 1:
            outs.append(o_refs[g][0, 0].astype(F32))
            tops.append(s_refs[g][0, 0, :, :LANES])
            dens.append(s_refs[g][0, 0, :, LANES:])
        else:
            n = o_refs[g].shape[2]
            for r in range(dil):
                rows = pl.ds(r, n, stride=dil)
                part = o_refs[g][0, r].astype(F32)
                for cb in range(N_PAIRS):
                    o_tok_refs[g][cb, rows, :] = part[:, cb * LANES:(cb + 1) * LANES]
                s_tok_refs[g][0, rows, :] = s_refs[g][0, r, :, :LANES]
                s_tok_refs[g][1, rows, :] = s_refs[g][0, r, :, LANES:]
            outs.append(jnp.concatenate([o_tok_refs[g][cb] for cb in range(N_PAIRS)], axis=1))
            tops.append(s_tok_refs[g][0])
            dens.append(s_tok_refs[g][1])
    top = functools.reduce(jnp.maximum, tops)
    es = [jnp.exp(m - top) for m in tops]
    inv = 1.0 / functools.reduce(jnp.add, [e * l for e, l in zip(es, dens)])
    merged = None
    for out, e in zip(outs, es):
        wt = e * inv
        hi = wt.astype(BF16)
        lo = (wt - hi.astype(F32)).astype(BF16)
        spread = jnp.dot(jnp.concatenate([hi, lo], axis=1), e_ref[...], preferred_element_type=F32)
        merged = spread * out if merged is None else merged + spread * out
    return merged.astype(BF16)


def _oproj_ffn_kernel(*refs, merge):
    if merge:
        o_refs, s_refs, e_ref = refs[0:N_GROUPS], refs[N_GROUPS:2 * N_GROUPS], refs[2 * N_GROUPS]
        refs = refs[2 * N_GROUPS + 1:]
        strided_groups = [g for g, (_, dil) in enumerate(DILATED_GROUPS) if dil > 1]
        n_tok = len(strided_groups)
        o_tok_refs = dict(zip(strided_groups, refs[-2 * n_tok:-n_tok]))
        s_tok_refs = dict(zip(strided_groups, refs[-n_tok:]))
        refs = refs[:-2 * n_tok]
    else:
        o_ref, refs = refs[0], refs[1:]
    x_ref, wo_ref, g_ref, wgu_ref, wd_ref, out_ref, xmid_ref, xn_ref, hid_ref = refs
    if merge:
        attn = _merge_groups(o_refs, s_refs, e_ref, o_tok_refs, s_tok_refs)
    else:
        attn = o_ref[...]
    h = jnp.dot(attn, wo_ref[...], preferred_element_type=F32)
    xm = x_ref[...] + _rms_scale(h, g_ref[1:2, :])
    xmid_ref[...] = xm
    xn_ref[...] = _rms_scale(xm, g_ref[2:3, :]).astype(BF16)
    for c in range(D_FF // FFN_COLS):
        lo = c * FFN_COLS
        xn = xn_ref[...]
        gate = jnp.dot(xn, wgu_ref[:, lo:lo + FFN_COLS], preferred_element_type=F32)
        up = jnp.dot(xn, wgu_ref[:, D_FF + lo:D_FF + lo + FFN_COLS], preferred_element_type=F32)
        hid_ref[:, lo:lo + FFN_COLS] = (gate * jax.nn.sigmoid(gate) * up).astype(BF16)
    down = jnp.dot(hid_ref[...], wd_ref[...], preferred_element_type=F32)
    out_ref[...] = xmid_ref[...] + _rms_scale(down, g_ref[3:4, :])


def _oproj_ffn(attn, x2d, w_o, gains, w_gate_up, w_down, seq):
    t, d = x2d.shape
    tm = FFN_TOKENS
    assert t % tm == 0 and seq % tm == 0 and D_FF % FFN_COLS == 0
    merge = not isinstance(attn, jax.Array)
    scratch = [pltpu.VMEM((tm, d), F32), pltpu.VMEM((tm, d), BF16), pltpu.VMEM((tm, D_FF), BF16)]
    resident = lambda shape: pl.BlockSpec(shape, lambda i: (0, 0), pipeline_mode=pl.Buffered(1))
    if merge:
        tiles = seq // tm
        assert all(tm % (16 * dil) == 0 for _, dil in DILATED_GROUPS)
        blk = lambda dil, cols: pl.BlockSpec((1, dil, tm // dil, cols),
                                             lambda i: (i // tiles, 0, i % tiles, 0))
        head = jnp.arange(2 * LANES)[:, None] % LANES
        col = jnp.arange(d)[None, :] // HEAD_DIM
        spread = (head == col).astype(BF16)
        attn_specs = ([blk(dil, d) for _, dil in DILATED_GROUPS]
                      + [blk(dil, 2 * LANES) for _, dil in DILATED_GROUPS]
                      + [resident(spread.shape)])
        attn_args = [o for o, _ in attn] + [s for _, s in attn] + [spread]
        n_strided = sum(dil > 1 for _, dil in DILATED_GROUPS)
        scratch += ([pltpu.VMEM((d // LANES, tm, LANES), F32)] * n_strided
                    + [pltpu.VMEM((2, tm, LANES), F32)] * n_strided)
    else:
        attn_specs = [pl.BlockSpec((tm, d), lambda i: (i, 0))]
        attn_args = [attn]
    return pl.pallas_call(
        functools.partial(_oproj_ffn_kernel, merge=merge),
        grid=(t // tm,),
        in_specs=attn_specs + [
            pl.BlockSpec((tm, d), lambda i: (i, 0)),
            resident(w_o.shape),
            resident(gains.shape),
            resident(w_gate_up.shape),
            resident(w_down.shape)],
        out_specs=pl.BlockSpec((tm, d), lambda i: (i, 0)),
        out_shape=jax.ShapeDtypeStruct((t, d), F32),
        scratch_shapes=scratch,
        compiler_params=pltpu.CompilerParams(dimension_semantics=("parallel",),
                                             vmem_limit_bytes=VMEM_LIMIT_BYTES),
        name="oproj_ffn_merge" if merge else "oproj_ffn",
    )(*attn_args, x2d, w_o, gains, w_gate_up, w_down)


def kernel(x, norm_gains, w_qkv_a, w_o_a, g_kv, w_kv_b, w_q_b, w_o_b, rel_bias, w_gate_up, w_down):
    batch, seq, d = x.shape
    depth = norm_gains.shape[0]
    n_a = w_qkv_a.shape[0]
    t = batch * seq
    x2d = x.reshape(t, d)
    shared_kv = None
    for layer in range(depth):
        g = norm_gains[layer]
        if layer < n_a:
            qkv = _norm_proj(x2d, g[0], w_qkv_a[layer].astype(BF16), pair_major=True)
            attn = _stick_attention(qkv, batch, seq).reshape(t, d)
            w_o = w_o_a[layer]
        else:
            if shared_kv is None:
                shared_kv = _norm_proj_groups(x2d, g_kv, w_kv_b.astype(BF16), batch, seq)
            j = layer - n_a
            q = _norm_proj_groups(x2d, g[0], w_q_b[j].astype(BF16), batch, seq)
            attn = [_dilated_group(q[grp], shared_kv[2 * grp], shared_kv[2 * grp + 1], rel_bias, grp)
                    for grp in range(N_GROUPS)]
            w_o = w_o_b[j]
        x2d = _oproj_ffn(attn, x2d, w_o.astype(BF16), g,
                         w_gate_up[layer].astype(BF16), w_down[layer].astype(BF16), seq)
    return x2d.reshape(batch, seq, d)
```

```python
import functools
import math

import jax
import jax.numpy as jnp
from jax import lax
from jax.experimental import pallas as pl
from jax.experimental.pallas import tpu as pltpu

D_MODEL = 1024
N_HEADS = 16
HEAD_DIM = D_MODEL // N_HEADS
LANES = 128
N_PAIRS = D_MODEL // LANES
Q_BLOCK = 128
DILATED_GROUPS = ((128, 1), (512, 4), (2048, 16))
N_GROUPS = len(DILATED_GROUPS)
NUM_BUCKETS = 32
MAX_DISTANCE = 2048
D_FF = -(-8 * D_MODEL // (3 * 256)) * 256
RMS_EPS = 1e-6
NEG_INF = -1e30
QK_SCALE = HEAD_DIM ** -0.5

VMEM_LIMIT_BYTES = 56 * 1024 * 1024
PROJ_TOKENS = 512
PROJ_COLS = 512
FFN_TOKENS = 512
FFN_COLS = 256
DILATED_SUBS = 4
STICK_QUERIES = 256
STICK_KEYS = 256
STICK_UNDERFLOW = 128.0

F32 = jnp.float32
BF16 = jnp.bfloat16


def _rms_scale(x, gain_row):
    ms = jnp.mean(x * x, axis=-1, keepdims=True)
    return x * lax.rsqrt(ms + RMS_EPS) * gain_row


def _split_heads(tile):
    lane = lax.broadcasted_iota(jnp.int32, tile.shape, 1)
    zero = jnp.zeros_like(tile)
    return jnp.concatenate([jnp.where(lane < HEAD_DIM, tile, zero),
                            jnp.where(lane >= HEAD_DIM, tile, zero)], axis=0)


def _merge_heads(stacked):
    rows = stacked.shape[0] // 2
    lane = lax.broadcasted_iota(jnp.int32, (rows, LANES), 1)
    return jnp.where(lane < HEAD_DIM, stacked[:rows], stacked[rows:])


def _norm_proj_kernel(x_ref, g_ref, w_ref, o_ref, *, pair_major):
    xn = _rms_scale(x_ref[...], g_ref[...]).astype(BF16)
    n_out = w_ref.shape[1]
    for c in range(n_out // PROJ_COLS):
        lo = c * PROJ_COLS
        res = jnp.dot(xn, w_ref[:, lo:lo + PROJ_COLS], preferred_element_type=F32).astype(o_ref.dtype)
        if pair_major:
            for k in range(PROJ_COLS // LANES):
                o_ref[lo // LANES + k] = res[:, k * LANES:(k + 1) * LANES]
        else:
            o_ref[:, lo:lo + PROJ_COLS] = res


def _norm_proj(x2d, gain, w, *, pair_major):
    t, d = x2d.shape
    n = w.shape[1]
    assert t % PROJ_TOKENS == 0 and n % PROJ_COLS == 0
    if pair_major:
        out_shape = jax.ShapeDtypeStruct((n // LANES, t, LANES), BF16)
        out_spec = pl.BlockSpec((n // LANES, PROJ_TOKENS, LANES), lambda i: (0, i, 0))
    else:
        out_shape = jax.ShapeDtypeStruct((t, n), BF16)
        out_spec = pl.BlockSpec((PROJ_TOKENS, n), lambda i: (i, 0))
    return pl.pallas_call(
        functools.partial(_norm_proj_kernel, pair_major=pair_major),
        grid=(t // PROJ_TOKENS,),
        in_specs=[pl.BlockSpec((PROJ_TOKENS, d), lambda i: (i, 0)),
                  pl.BlockSpec((1, d), lambda i: (0, 0)),
                  pl.BlockSpec((d, n), lambda i: (0, 0), pipeline_mode=pl.Buffered(1))],
        out_specs=out_spec,
        out_shape=out_shape,
        compiler_params=pltpu.CompilerParams(dimension_semantics=("parallel",),
                                             vmem_limit_bytes=VMEM_LIMIT_BYTES),
        name="norm_proj",
    )(x2d, gain.reshape(1, d), w)


def _norm_proj_groups_kernel(x_ref, g_ref, *rest, chunks):
    n_proj = len(chunks)
    w_refs, out_refs, (xs_ref, xp_ref) = rest[:n_proj], rest[n_proj:-2], rest[-2:]
    tm = x_ref.shape[0]
    x = x_ref[...]
    xs = x * lax.rsqrt(jnp.mean(x * x, axis=-1, keepdims=True) + RMS_EPS)
    for cb in range(N_PAIRS):
        xs_ref[cb] = xs[:, cb * LANES:(cb + 1) * LANES]
    first_out = [sum(chunks[:p]) * N_GROUPS for p in range(n_proj)]
    for g, (_, dil) in enumerate(DILATED_GROUPS):
        n = tm // dil
        if dil > 1:
            for r in range(dil):
                for cb in range(N_PAIRS):
                    xp_ref[r * n:(r + 1) * n, cb * LANES:(cb + 1) * LANES] = (
                        xs_ref[cb, pl.ds(r, n, stride=dil), :])
        for p in range(n_proj):
            rows = xs if dil == 1 else xp_ref[...]
            xn = (rows * g_ref[p:p + 1, :]).astype(BF16)
            for c in range(chunks[p]):
                col = (g * chunks[p] + c) * D_MODEL
                res = jnp.dot(xn, w_refs[p][:, col:col + D_MODEL],
                              preferred_element_type=F32).astype(BF16)
                out = out_refs[first_out[p] + g * chunks[p] + c]
                for r in range(dil):
                    out[0, r] = res[r * n:(r + 1) * n]


def _norm_proj_groups(x2d, gains, ws, batch, seq):
    t, d = x2d.shape
    chunks = tuple(w.shape[1] // (N_GROUPS * d) for w in ws)
    tiles = seq // PROJ_TOKENS
    assert seq % PROJ_TOKENS == 0 and all(PROJ_TOKENS % (16 * dil) == 0 for _, dil in DILATED_GROUPS)
    out_shape, out_specs = [], []
    for n_chunks in chunks:
        for _, dil in DILATED_GROUPS:
            for _ in range(n_chunks):
                out_shape.append(jax.ShapeDtypeStruct((batch, dil, seq // dil, d), BF16))
                out_specs.append(pl.BlockSpec((1, dil, PROJ_TOKENS // dil, d), lambda b, j: (b, 0, j, 0)))
    outs = pl.pallas_call(
        functools.partial(_norm_proj_groups_kernel, chunks=chunks),
        grid=(batch, tiles),
        in_specs=([pl.BlockSpec((PROJ_TOKENS, d), lambda b, j: (b * tiles + j, 0)),
                   pl.BlockSpec((len(ws), d), lambda b, j: (0, 0))]
                  + [pl.BlockSpec(w.shape, lambda b, j: (0, 0), pipeline_mode=pl.Buffered(1)) for w in ws]),
        out_specs=out_specs,
        out_shape=out_shape,
        scratch_shapes=[pltpu.VMEM((d // LANES, PROJ_TOKENS, LANES), F32),
                        pltpu.VMEM((PROJ_TOKENS, d), F32)],
        compiler_params=pltpu.CompilerParams(dimension_semantics=("parallel", "parallel"),
                                             vmem_limit_bytes=VMEM_LIMIT_BYTES),
        name="norm_proj_groups",
    )(x2d, jnp.stack(gains), *ws)
    split, start = [], 0
    for n_chunks in chunks:
        split.append(outs[start:start + n_chunks * N_GROUPS])
        start += n_chunks * N_GROUPS
    return split


def _stick_kernel(q_ref, k_ref, v_ref, u_ref, o_ref, acc_ref, carry_ref):
    rows = 2 * STICK_QUERIES
    nq = q_ref.shape[2] // STICK_QUERIES

    def span(ref, idx):
        return ref[0, 0, idx * STICK_KEYS:(idx + 1) * STICK_KEYS, :]

    def weights(z, carry, diagonal):
        cost = jnp.maximum(z, 0.0) + jnp.log(1.0 + jnp.exp(-jnp.abs(z)))
        if diagonal:
            qrow = lax.broadcasted_iota(jnp.int32, z.shape, 0) % STICK_QUERIES
            kcol = lax.broadcasted_iota(jnp.int32, z.shape, 1)
            strict = kcol < qrow
            cost = jnp.where(strict, cost, 0.0)
        w = z - jnp.dot(cost.astype(BF16), u_ref[...], preferred_element_type=F32)
        a = jnp.concatenate([jnp.exp(w[:, :LANES] - carry), jnp.exp(w[:, LANES:] - carry)], axis=1)
        if diagonal:
            a = jnp.where(strict, a, 0.0)
        return a.astype(BF16), carry + jnp.sum(cost, axis=-1, keepdims=True)

    def walk(qi, spans, acc, carry):
        qs = _split_heads(span(q_ref, qi)) * QK_SCALE
        for j in spans:
            z = lax.dot_general(qs, span(k_ref, j), (((1,), (1,)), ((), ())),
                                preferred_element_type=F32)
            a, carry = weights(z, carry, j == qi)
            acc = acc + jnp.dot(a, span(v_ref, j), preferred_element_type=F32)
        return acc, carry

    alive = {}
    for qi in range(nq):
        zeros = jnp.zeros((rows, LANES), F32)
        acc, carry = walk(qi, range(qi, max(qi - 2, -1), -1), zeros, zeros)
        acc_ref[qi] = acc
        if qi >= 2:
            carry_ref[qi] = carry
            alive[qi] = jnp.min(carry, axis=0, keepdims=True)[0, 0] < STICK_UNDERFLOW

    for qi in range(2, nq):
        @pl.when(alive[qi])
        def _(qi=qi):
            acc, _ = walk(qi, range(qi - 2, -1, -1), acc_ref[qi], carry_ref[qi])
            acc_ref[qi] = acc

    for qi in range(nq):
        o_ref[0, qi * STICK_QUERIES:(qi + 1) * STICK_QUERIES, :] = (
            _merge_heads(acc_ref[qi]).astype(o_ref.dtype))


def _stick_attention(qkv, batch, seq):
    assert STICK_QUERIES == STICK_KEYS and seq % STICK_QUERIES == 0
    qkv = qkv.reshape(3 * N_PAIRS, batch, seq, LANES)
    assert STICK_KEYS == 2 * LANES
    j = jnp.arange(STICK_KEYS)[:, None]
    s = jnp.arange(STICK_KEYS)[None, :]
    uu = (j >= s).astype(BF16)
    return pl.pallas_call(
        _stick_kernel,
        grid=(batch, N_PAIRS),
        in_specs=[pl.BlockSpec((1, 1, seq, LANES), lambda b, p: (p, b, 0, 0)),
                  pl.BlockSpec((1, 1, seq, LANES), lambda b, p: (N_PAIRS + p, b, 0, 0)),
                  pl.BlockSpec((1, 1, seq, LANES), lambda b, p: (2 * N_PAIRS + p, b, 0, 0)),
                  pl.BlockSpec((STICK_KEYS, STICK_KEYS), lambda b, p: (0, 0))],
        out_specs=pl.BlockSpec((1, seq, LANES), lambda b, p: (b, 0, p)),
        out_shape=jax.ShapeDtypeStruct((batch, seq, D_MODEL), BF16),
        scratch_shapes=[pltpu.VMEM((seq // STICK_QUERIES, 2 * STICK_QUERIES, LANES), F32),
                        pltpu.VMEM((seq // STICK_QUERIES, 2 * STICK_QUERIES, LANES), F32)],
        compiler_params=pltpu.CompilerParams(dimension_semantics=("parallel", "parallel"),
                                             vmem_limit_bytes=VMEM_LIMIT_BYTES),
        name="stick_attention",
    )(qkv, qkv, qkv, uu)


def _relative_bucket(distance):
    max_exact = NUM_BUCKETS // 2
    n = jnp.maximum(distance, 0)
    large = max_exact + (jnp.log(jnp.maximum(n, 1).astype(F32) / max_exact)
                         / math.log(MAX_DISTANCE / max_exact)
                         * (NUM_BUCKETS - max_exact)).astype(jnp.int32)
    large = jnp.minimum(large, NUM_BUCKETS - 1)
    return jnp.where(n < max_exact, n, large)


def _band_bias(rel_bias, window, dilation):
    i = jnp.arange(Q_BLOCK)[:, None]
    m = jnp.arange(2 * Q_BLOCK)[None, :]
    rel = Q_BLOCK + i - m
    band = (rel >= 0) & (rel <= window // dilation)
    onehot = jax.nn.one_hot(_relative_bucket(rel * dilation), NUM_BUCKETS, dtype=F32)
    bias = jnp.einsum('qkb,bh->hqk', onehot, rel_bias.astype(F32), precision=lax.Precision.HIGHEST)
    bias = jnp.where(band[None], bias, NEG_INF)
    return bias.reshape(N_PAIRS, 2 * Q_BLOCK, 2 * Q_BLOCK)


def _band_attend(q_tile, kk, vv, bias):
    qs = _split_heads(q_tile) * QK_SCALE
    s = lax.dot_general(qs, kk, (((1,), (1,)), ((), ())), preferred_element_type=F32) + bias
    m = jnp.max(s, axis=-1, keepdims=True)
    p = jnp.exp(s - m)
    l = jnp.sum(p, axis=-1, keepdims=True)
    return jnp.dot(p.astype(BF16), vv, preferred_element_type=F32), m, l


def _dilated_kernel(*refs, with_prev):
    if with_prev:
        q_ref, kp_ref, kc_ref, vp_ref, vc_ref, bias_first_ref, bias_ref, o_ref, s_ref = refs
    else:
        q_ref, kc_ref, vc_ref, bias_ref, o_ref, s_ref = refs
    lane = lax.broadcasted_iota(jnp.int32, (Q_BLOCK, LANES), 1)
    for sub in range(DILATED_SUBS):
        m_stats = jnp.zeros((Q_BLOCK, LANES), F32)
        l_stats = jnp.ones((Q_BLOCK, LANES), F32)
        rows = slice(sub * Q_BLOCK, (sub + 1) * Q_BLOCK)
        for hp in range(N_PAIRS):
            sl = slice(hp * LANES, (hp + 1) * LANES)
            if not with_prev:
                q_tile, kk, vv = q_ref[0, sub, :, sl], kc_ref[0, sub, :, sl], vc_ref[0, sub, :, sl]
                bias = bias_ref[hp]
            elif sub == 0:
                q_tile = q_ref[0, 0, rows, sl]
                kk = jnp.concatenate([kp_ref[0, 0, :, sl], kc_ref[0, 0, rows, sl]], axis=0)
                vv = jnp.concatenate([vp_ref[0, 0, :, sl], vc_ref[0, 0, rows, sl]], axis=0)
                bias = bias_first_ref[0, hp]
            else:
                keys = slice((sub - 1) * Q_BLOCK, (sub + 1) * Q_BLOCK)
                q_tile, kk, vv = q_ref[0, 0, rows, sl], kc_ref[0, 0, keys, sl], vc_ref[0, 0, keys, sl]
                bias = bias_ref[0, hp]
            o, m, l = _band_attend(q_tile, kk, vv, bias)
            if with_prev:
                o_ref[0, 0, rows, sl] = _merge_heads(o).astype(o_ref.dtype)
            else:
                o_ref[0, sub, :, sl] = _merge_heads(o).astype(o_ref.dtype)
            for head, half in ((2 * hp, slice(0, Q_BLOCK)), (2 * hp + 1, slice(Q_BLOCK, None))):
                m_stats = jnp.where(lane == head, m[half], m_stats)
                l_stats = jnp.where(lane == head, l[half], l_stats)
        stats = jnp.concatenate([m_stats, l_stats], axis=1)
        if with_prev:
            s_ref[0, 0, rows, :] = stats
        else:
            s_ref[0, sub] = stats


def _dilated_group(q, k, v, rel_bias, group):
    window, dilation = DILATED_GROUPS[group]
    batch, _, strided, d = q.shape
    assert strided % Q_BLOCK == 0
    nb = strided // Q_BLOCK
    with_prev = nb > 1
    bias = _band_bias(rel_bias, window, dilation)
    if with_prev:
        assert nb % DILATED_SUBS == 0
        key = jnp.arange(2 * Q_BLOCK)[None, None, :]
        bias = jnp.stack([jnp.where(key < Q_BLOCK, NEG_INF, bias), bias])
        grid = (batch, dilation, nb // DILATED_SUBS)
        two = pl.BlockSpec((1, 1, DILATED_SUBS * Q_BLOCK, d), lambda b, r, i: (b, r, i, 0))
        prev = pl.BlockSpec((1, 1, Q_BLOCK, d),
                            lambda b, r, i: (b, r, jnp.maximum(DILATED_SUBS * i - 1, 0), 0))
        table = (1,) + bias.shape[1:]
        in_specs = [two, prev, two, prev, two,
                    pl.BlockSpec(table, lambda b, r, i: (jnp.minimum(i, 1), 0, 0, 0)),
                    pl.BlockSpec(table, lambda b, r, i: (1, 0, 0, 0))]
        args = [q, k, k, v, v, bias, bias]
        out_specs = [two, pl.BlockSpec((1, 1, DILATED_SUBS * Q_BLOCK, 2 * LANES),
                                       lambda b, r, i: (b, r, i, 0))]
    else:
        assert dilation % DILATED_SUBS == 0
        bias = bias[:, :, Q_BLOCK:]
        grid = (batch, dilation // DILATED_SUBS, 1)
        two = pl.BlockSpec((1, DILATED_SUBS, Q_BLOCK, d), lambda b, r, i: (b, r, 0, 0))
        in_specs = [two, two, two, pl.BlockSpec(bias.shape, lambda b, r, i: (0, 0, 0))]
        args = [q, k, v, bias]
        out_specs = [two, pl.BlockSpec((1, DILATED_SUBS, Q_BLOCK, 2 * LANES), lambda b, r, i: (b, r, 0, 0))]
    return pl.pallas_call(
        functools.partial(_dilated_kernel, with_prev=with_prev),
        grid=grid,
        in_specs=in_specs,
        out_specs=out_specs,
        out_shape=[jax.ShapeDtypeStruct((batch, dilation, strided, d), BF16),
                   jax.ShapeDtypeStruct((batch, dilation, strided, 2 * LANES), F32)],
        compiler_params=pltpu.CompilerParams(dimension_semantics=("parallel", "parallel", "parallel"),
                                             vmem_limit_bytes=VMEM_LIMIT_BYTES),
        name=f"dilated_group{group}",
    )(*args)


def _merge_groups(o_refs, s_refs, e_ref, o_tok_refs, s_tok_refs):
    outs, tops, dens = [], [], []
    for g, (_, dil) in enumerate(DILATED_GROUPS):
        if dil == 1:
            outs.append(o_refs[g][0, 0].astype(F32))
            tops.append(s_refs[g][0, 0, :, :LANES])
            dens.append(s_refs[g][0, 0, :, LANES:])
        else:
            n = o_refs[g].shape[2]
            for r in range(dil):
                rows = pl.ds(r, n, stride=dil)
                part = o_refs[g][0, r].astype(F32)
                for cb in range(N_PAIRS):
                    o_tok_refs[g][cb, rows, :] = part[:, cb * LANES:(cb + 1) * LANES]
                s_tok_refs[g][0, rows, :] = s_refs[g][0, r, :, :LANES]
                s_tok_refs[g][1, rows, :] = s_refs[g][0, r, :, LANES:]
            outs.append(jnp.concatenate([o_tok_refs[g][cb] for cb in range(N_PAIRS)], axis=1))
            tops.append(s_tok_refs[g][0])
            dens.append(s_tok_refs[g][1])
    top = functools.reduce(jnp.maximum, tops)
    es = [jnp.exp(m - top) for m in tops]
    inv = 1.0 / functools.reduce(jnp.add, [e * l for e, l in zip(es, dens)])
    merged = None
    for out, e in zip(outs, es):
        wt = e * inv
        hi = wt.astype(BF16)
        lo = (wt - hi.astype(F32)).astype(BF16)
        spread = jnp.dot(jnp.concatenate([hi, lo], axis=1), e_ref[...], preferred_element_type=F32)
        merged = spread * out if merged is None else merged + spread * out
    return merged.astype(BF16)


def _oproj_ffn_kernel(*refs, merge):
    if merge:
        o_refs, s_refs, e_ref = refs[0:N_GROUPS], refs[N_GROUPS:2 * N_GROUPS], refs[2 * N_GROUPS]
        refs = refs[2 * N_GROUPS + 1:]
        strided_groups = [g for g, (_, dil) in enumerate(DILATED_GROUPS) if dil > 1]
        n_tok = len(strided_groups)
        o_tok_refs = dict(zip(strided_groups, refs[-2 * n_tok:-n_tok]))
        s_tok_refs = dict(zip(strided_groups, refs[-n_tok:]))
        refs = refs[:-2 * n_tok]
    else:
        o_ref, refs = refs[0], refs[1:]
    x_ref, wo_ref, g_ref, wgu_ref, wd_ref, out_ref, xmid_ref, xn_ref, hid_ref = refs
    if merge:
        attn = _merge_groups(o_refs, s_refs, e_ref, o_tok_refs, s_tok_refs)
    else:
        attn = o_ref[...]
    h = jnp.dot(attn, wo_ref[...], preferred_element_type=F32)
    xm = x_ref[...] + _rms_scale(h, g_ref[1:2, :])
    xmid_ref[...] = xm
    xn_ref[...] = _rms_scale(xm, g_ref[2:3, :]).astype(BF16)
    for c in range(D_FF // FFN_COLS):
        lo = c * FFN_COLS
        xn = xn_ref[...]
        gate = jnp.dot(xn, wgu_ref[:, lo:lo + FFN_COLS], preferred_element_type=F32)
        up = jnp.dot(xn, wgu_ref[:, D_FF + lo:D_FF + lo + FFN_COLS], preferred_element_type=F32)
        hid_ref[:, lo:lo + FFN_COLS] = (gate * jax.nn.sigmoid(gate) * up).astype(BF16)
    down = jnp.dot(hid_ref[...], wd_ref[...], preferred_element_type=F32)
    out_ref[...] = xmid_ref[...] + _rms_scale(down, g_ref[3:4, :])


def _oproj_ffn(attn, x2d, w_o, gains, w_gate_up, w_down, seq):
    t, d = x2d.shape
    tm = FFN_TOKENS
    assert t % tm == 0 and seq % tm == 0 and D_FF % FFN_COLS == 0
    merge = not isinstance(attn, jax.Array)
    scratch = [pltpu.VMEM((tm, d), F32), pltpu.VMEM((tm, d), BF16), pltpu.VMEM((tm, D_FF), BF16)]
    resident = lambda shape: pl.BlockSpec(shape, lambda i: (0, 0), pipeline_mode=pl.Buffered(1))
    tok = pl.BlockSpec((tm, d), lambda i: (i, 0))
    if merge:
        tiles = seq // tm
        assert all(tm % (16 * dil) == 0 for _, dil in DILATED_GROUPS)
        blk = lambda dil, cols: pl.BlockSpec((1, dil, tm // dil, cols),
                                             lambda i: (i // tiles, 0, i % tiles, 0))
        head = jnp.arange(2 * LANES)[:, None] % LANES
        col = jnp.arange(d)[None, :] // HEAD_DIM
        spread = (head == col).astype(BF16)
        attn_specs = ([blk(dil, d) for _, dil in DILATED_GROUPS]
                      + [blk(dil, 2 * LANES) for _, dil in DILATED_GROUPS]
                      + [resident(spread.shape)])
        attn_args = [o for o, _ in attn] + [s for _, s in attn] + [spread]
        n_strided = sum(dil > 1 for _, dil in DILATED_GROUPS)
        scratch += ([pltpu.VMEM((d // LANES, tm, LANES), F32)] * n_strided
                    + [pltpu.VMEM((2, tm, LANES), F32)] * n_strided)
    else:
        attn_specs = [tok]
        attn_args = [attn]
    return pl.pallas_call(
        functools.partial(_oproj_ffn_kernel, merge=merge),
        grid=(t // tm,),
        in_specs=attn_specs + [
            tok,
            resident(w_o.shape),
            resident(gains.shape),
            resident(w_gate_up.shape),
            resident(w_down.shape)],
        out_specs=tok,
        out_shape=jax.ShapeDtypeStruct((t, d), F32),
        scratch_shapes=scratch,
        compiler_params=pltpu.CompilerParams(dimension_semantics=("parallel",),
                                             vmem_limit_bytes=VMEM_LIMIT_BYTES),
        name="oproj_ffn_merge" if merge else "oproj_ffn",
    )(*attn_args, x2d, w_o, gains, w_gate_up, w_down)


def kernel(x, norm_gains, w_qkv_a, w_o_a, g_kv, w_kv_b, w_q_b, w_o_b, rel_bias, w_gate_up, w_down):
    batch, seq, d = x.shape
    depth = norm_gains.shape[0]
    n_a = w_qkv_a.shape[0]
    t = batch * seq
    x2d = x.reshape(t, d)
    shared_kv = None
    for layer in range(depth):
        g = norm_gains[layer]
        if layer < n_a:
            qkv = _norm_proj(x2d, g[0], w_qkv_a[layer].astype(BF16), pair_major=True)
            attn = _stick_attention(qkv, batch, seq).reshape(t, d)
            w_o = w_o_a[layer]
        else:
            j = layer - n_a
            if shared_kv is None:
                q, shared_kv = _norm_proj_groups(x2d, [g[0], g_kv],
                                                 [w_q_b[j].astype(BF16), w_kv_b.astype(BF16)], batch, seq)
            else:
                (q,) = _norm_proj_groups(x2d, [g[0]], [w_q_b[j].astype(BF16)], batch, seq)
            attn = [_dilated_group(q[grp], shared_kv[2 * grp], shared_kv[2 * grp + 1], rel_bias, grp)
                    for grp in range(N_GROUPS)]
            w_o = w_o_b[j]
        x2d = _oproj_ffn(attn, x2d, w_o.astype(BF16), g,
                         w_gate_up[layer].astype(BF16), w_down[layer].astype(BF16), seq)
    return x2d.reshape(batch, seq, d)
```

```python
import functools
import math

import jax
import jax.numpy as jnp
from jax import lax
from jax.experimental import pallas as pl
from jax.experimental.pallas import tpu as pltpu

D_MODEL = 1024
N_HEADS = 16
HEAD_DIM = D_MODEL // N_HEADS
LANES = 128
N_PAIRS = D_MODEL // LANES
Q_BLOCK = 128
DILATED_GROUPS = ((128, 1), (512, 4), (2048, 16))
N_GROUPS = len(DILATED_GROUPS)
NUM_BUCKETS = 32
MAX_DISTANCE = 2048
D_FF = -(-8 * D_MODEL // (3 * 256)) * 256
RMS_EPS = 1e-6
NEG_INF = -1e30
QK_SCALE = HEAD_DIM ** -0.5

VMEM_LIMIT_BYTES = 56 * 1024 * 1024
PROJ_TOKENS = 512
PROJ_COLS = 512
FFN_TOKENS = 512
FFN_COLS = 256
DILATED_SUBS = 4
STICK_QUERIES = 256
STICK_KEYS = 256
STICK_UNDERFLOW = 128.0

F32 = jnp.float32
BF16 = jnp.bfloat16


def _rms_scale(x, gain_row):
    ms = jnp.mean(x * x, axis=-1, keepdims=True)
    return x * lax.rsqrt(ms + RMS_EPS) * gain_row


def _split_heads(tile):
    lane = lax.broadcasted_iota(jnp.int32, tile.shape, 1)
    zero = jnp.zeros_like(tile)
    return jnp.concatenate([jnp.where(lane < HEAD_DIM, tile, zero),
                            jnp.where(lane >= HEAD_DIM, tile, zero)], axis=0)


def _merge_heads(stacked):
    rows = stacked.shape[0] // 2
    lane = lax.broadcasted_iota(jnp.int32, (rows, LANES), 1)
    return jnp.where(lane < HEAD_DIM, stacked[:rows], stacked[rows:])


def _norm_proj_kernel(x_ref, g_ref, w_ref, o_ref, *, pair_major):
    xn = _rms_scale(x_ref[...], g_ref[...]).astype(BF16)
    n_out = w_ref.shape[1]
    for c in range(n_out // PROJ_COLS):
        lo = c * PROJ_COLS
        res = jnp.dot(xn, w_ref[:, lo:lo + PROJ_COLS], preferred_element_type=F32).astype(o_ref.dtype)
        if pair_major:
            for k in range(PROJ_COLS // LANES):
                o_ref[lo // LANES + k] = res[:, k * LANES:(k + 1) * LANES]
        else:
            o_ref[:, lo:lo + PROJ_COLS] = res


def _norm_proj(x2d, gain, w, *, pair_major):
    t, d = x2d.shape
    n = w.shape[1]
    assert t % PROJ_TOKENS == 0 and n % PROJ_COLS == 0
    if pair_major:
        out_shape = jax.ShapeDtypeStruct((n // LANES, t, LANES), BF16)
        out_spec = pl.BlockSpec((n // LANES, PROJ_TOKENS, LANES), lambda i: (0, i, 0))
    else:
        out_shape = jax.ShapeDtypeStruct((t, n), BF16)
        out_spec = pl.BlockSpec((PROJ_TOKENS, n), lambda i: (i, 0))
    return pl.pallas_call(
        functools.partial(_norm_proj_kernel, pair_major=pair_major),
        grid=(t // PROJ_TOKENS,),
        in_specs=[pl.BlockSpec((PROJ_TOKENS, d), lambda i: (i, 0)),
                  pl.BlockSpec((1, d), lambda i: (0, 0)),
                  pl.BlockSpec((d, n), lambda i: (0, 0), pipeline_mode=pl.Buffered(1))],
        out_specs=out_spec,
        out_shape=out_shape,
        compiler_params=pltpu.CompilerParams(dimension_semantics=("parallel",),
                                             vmem_limit_bytes=VMEM_LIMIT_BYTES),
        name="norm_proj",
    )(x2d, gain.reshape(1, d), w)


def _norm_proj_groups_kernel(x_ref, g_ref, *rest, chunks):
    n_proj = len(chunks)
    w_refs, out_refs, (xs_ref, xp_ref) = rest[:n_proj], rest[n_proj:-2], rest[-2:]
    tm = x_ref.shape[0]
    x = x_ref[...]
    xs = x * lax.rsqrt(jnp.mean(x * x, axis=-1, keepdims=True) + RMS_EPS)
    for cb in range(N_PAIRS):
        xs_ref[cb] = xs[:, cb * LANES:(cb + 1) * LANES]
    first_out = [sum(chunks[:p]) * N_GROUPS for p in range(n_proj)]
    for g, (_, dil) in enumerate(DILATED_GROUPS):
        n = tm // dil
        if dil > 1:
            for r in range(dil):
                for cb in range(N_PAIRS):
                    xp_ref[r * n:(r + 1) * n, cb * LANES:(cb + 1) * LANES] = (
                        xs_ref[cb, pl.ds(r, n, stride=dil), :])
        for p in range(n_proj):
            rows = xs if dil == 1 else xp_ref[...]
            xn = (rows * g_ref[p:p + 1, :]).astype(BF16)
            for c in range(chunks[p]):
                col = (g * chunks[p] + c) * D_MODEL
                res = jnp.dot(xn, w_refs[p][:, col:col + D_MODEL],
                              preferred_element_type=F32).astype(BF16)
                out = out_refs[first_out[p] + g * chunks[p] + c]
                for r in range(dil):
                    out[0, r] = res[r * n:(r + 1) * n]


def _norm_proj_groups(x2d, gains, ws, batch, seq):
    t, d = x2d.shape
    chunks = tuple(w.shape[1] // (N_GROUPS * d) for w in ws)
    tiles = seq // PROJ_TOKENS
    assert seq % PROJ_TOKENS == 0 and all(PROJ_TOKENS % (16 * dil) == 0 for _, dil in DILATED_GROUPS)
    out_shape, out_specs = [], []
    for n_chunks in chunks:
        for _, dil in DILATED_GROUPS:
            for _ in range(n_chunks):
                out_shape.append(jax.ShapeDtypeStruct((batch, dil, seq // dil, d), BF16))
                out_specs.append(pl.BlockSpec((1, dil, PROJ_TOKENS // dil, d), lambda b, j: (b, 0, j, 0)))
    outs = pl.pallas_call(
        functools.partial(_norm_proj_groups_kernel, chunks=chunks),
        grid=(batch, tiles),
        in_specs=([pl.BlockSpec((PROJ_TOKENS, d), lambda b, j: (b * tiles + j, 0)),
                   pl.BlockSpec((len(ws), d), lambda b, j: (0, 0))]
                  + [pl.BlockSpec(w.shape, lambda b, j: (0, 0), pipeline_mode=pl.Buffered(1)) for w in ws]),
        out_specs=out_specs,
        out_shape=out_shape,
        scratch_shapes=[pltpu.VMEM((d // LANES, PROJ_TOKENS, LANES), F32),
                        pltpu.VMEM((PROJ_TOKENS, d), F32)],
        compiler_params=pltpu.CompilerParams(dimension_semantics=("parallel", "parallel"),
                                             vmem_limit_bytes=VMEM_LIMIT_BYTES),
        name="norm_proj_groups",
    )(x2d, jnp.stack(gains), *ws)
    split, start = [], 0
    for n_chunks in chunks:
        split.append(outs[start:start + n_chunks * N_GROUPS])
        start += n_chunks * N_GROUPS
    return split


def _stick_kernel(q_ref, k_ref, v_ref, u_ref, o_ref, acc_ref, carry_ref):
    rows = 2 * STICK_QUERIES
    nq = q_ref.shape[2] // STICK_QUERIES

    def span(ref, idx):
        return ref[0, 0, idx * STICK_KEYS:(idx + 1) * STICK_KEYS, :]

    def weights(z, carry, diagonal):
        cost = jnp.maximum(z, 0.0) + jnp.log(1.0 + jnp.exp(-jnp.abs(z)))
        if diagonal:
            qrow = lax.broadcasted_iota(jnp.int32, z.shape, 0) % STICK_QUERIES
            kcol = lax.broadcasted_iota(jnp.int32, z.shape, 1)
            strict = kcol < qrow
            cost = jnp.where(strict, cost, 0.0)
        w = z - jnp.dot(cost.astype(BF16), u_ref[...], preferred_element_type=F32)
        a = jnp.concatenate([jnp.exp(w[:, :LANES] - carry), jnp.exp(w[:, LANES:] - carry)], axis=1)
        if diagonal:
            a = jnp.where(strict, a, 0.0)
        return a.astype(BF16), carry + jnp.sum(cost, axis=-1, keepdims=True)

    def walk(qi, spans, acc, carry):
        qs = _split_heads(span(q_ref, qi)) * QK_SCALE
        for j in spans:
            z = lax.dot_general(qs, span(k_ref, j), (((1,), (1,)), ((), ())),
                                preferred_element_type=F32)
            a, carry = weights(z, carry, j == qi)
            acc = acc + jnp.dot(a, span(v_ref, j), preferred_element_type=F32)
        return acc, carry

    alive = {}
    for qi in range(nq):
        zeros = jnp.zeros((rows, LANES), F32)
        acc, carry = walk(qi, range(qi, max(qi - 2, -1), -1), zeros, zeros)
        acc_ref[qi] = acc
        if qi >= 2:
            carry_ref[qi] = carry
            alive[qi] = jnp.min(carry, axis=0, keepdims=True)[0, 0] < STICK_UNDERFLOW

    for qi in range(2, nq):
        @pl.when(alive[qi])
        def _(qi=qi):
            acc, _ = walk(qi, range(qi - 2, -1, -1), acc_ref[qi], carry_ref[qi])
            acc_ref[qi] = acc

    for qi in range(nq):
        o_ref[0, qi * STICK_QUERIES:(qi + 1) * STICK_QUERIES, :] = (
            _merge_heads(acc_ref[qi]).astype(o_ref.dtype))


def _stick_attention(qkv, batch, seq):
    assert STICK_QUERIES == STICK_KEYS and seq % STICK_QUERIES == 0
    qkv = qkv.reshape(3 * N_PAIRS, batch, seq, LANES)
    assert STICK_KEYS == 2 * LANES
    j = jnp.arange(STICK_KEYS)[:, None]
    s = jnp.arange(STICK_KEYS)[None, :]
    uu = (j >= s).astype(BF16)
    return pl.pallas_call(
        _stick_kernel,
        grid=(batch, N_PAIRS),
        in_specs=[pl.BlockSpec((1, 1, seq, LANES), lambda b, p: (p, b, 0, 0)),
                  pl.BlockSpec((1, 1, seq, LANES), lambda b, p: (N_PAIRS + p, b, 0, 0)),
                  pl.BlockSpec((1, 1, seq, LANES), lambda b, p: (2 * N_PAIRS + p, b, 0, 0)),
                  pl.BlockSpec((STICK_KEYS, STICK_KEYS), lambda b, p: (0, 0))],
        out_specs=pl.BlockSpec((1, seq, LANES), lambda b, p: (b, 0, p)),
        out_shape=jax.ShapeDtypeStruct((batch, seq, D_MODEL), BF16),
        scratch_shapes=[pltpu.VMEM((seq // STICK_QUERIES, 2 * STICK_QUERIES, LANES), F32),
                        pltpu.VMEM((seq // STICK_QUERIES, 2 * STICK_QUERIES, LANES), F32)],
        compiler_params=pltpu.CompilerParams(dimension_semantics=("parallel", "parallel"),
                                             vmem_limit_bytes=VMEM_LIMIT_BYTES),
        name="stick_attention",
    )(qkv, qkv, qkv, uu)


def _relative_bucket(distance):
    max_exact = NUM_BUCKETS // 2
    n = jnp.maximum(distance, 0)
    large = max_exact + (jnp.log(jnp.maximum(n, 1).astype(F32) / max_exact)
                         / math.log(MAX_DISTANCE / max_exact)
                         * (NUM_BUCKETS - max_exact)).astype(jnp.int32)
    large = jnp.minimum(large, NUM_BUCKETS - 1)
    return jnp.where(n < max_exact, n, large)


def _band_bias(rel_bias, window, dilation):
    i = jnp.arange(Q_BLOCK)[:, None]
    m = jnp.arange(2 * Q_BLOCK)[None, :]
    rel = Q_BLOCK + i - m
    band = (rel >= 0) & (rel <= window // dilation)
    onehot = jax.nn.one_hot(_relative_bucket(rel * dilation), NUM_BUCKETS, dtype=F32)
    bias = jnp.einsum('qkb,bh->hqk', onehot, rel_bias.astype(F32), precision=lax.Precision.HIGHEST)
    bias = jnp.where(band[None], bias, NEG_INF)
    return bias.reshape(N_PAIRS, 2 * Q_BLOCK, 2 * Q_BLOCK)


def _band_attend(q_tile, kk, vv, bias):
    qs = _split_heads(q_tile) * QK_SCALE
    s = lax.dot_general(qs, kk, (((1,), (1,)), ((), ())), preferred_element_type=F32) + bias
    m = jnp.max(s, axis=-1, keepdims=True)
    p = jnp.exp(s - m)
    l = jnp.sum(p, axis=-1, keepdims=True)
    return jnp.dot(p.astype(BF16), vv, preferred_element_type=F32), m, l


def _dilated_kernel(*refs, with_prev):
    if with_prev:
        q_ref, kp_ref, kc_ref, vp_ref, vc_ref, bias_first_ref, bias_ref, o_ref, s_ref = refs
    else:
        q_ref, kc_ref, vc_ref, bias_ref, o_ref, s_ref = refs
    lane = lax.broadcasted_iota(jnp.int32, (Q_BLOCK, LANES), 1)
    for sub in range(DILATED_SUBS):
        m_stats = jnp.zeros((Q_BLOCK, LANES), F32)
        l_stats = jnp.ones((Q_BLOCK, LANES), F32)
        rows = slice(sub * Q_BLOCK, (sub + 1) * Q_BLOCK)
        for hp in range(N_PAIRS):
            sl = slice(hp * LANES, (hp + 1) * LANES)
            if not with_prev:
                q_tile, kk, vv = q_ref[0, sub, :, sl], kc_ref[0, sub, :, sl], vc_ref[0, sub, :, sl]
                bias = bias_ref[hp]
            elif sub == 0:
                q_tile = q_ref[0, 0, rows, sl]
                kk = jnp.concatenate([kp_ref[0, 0, :, sl], kc_ref[0, 0, rows, sl]], axis=0)
                vv = jnp.concatenate([vp_ref[0, 0, :, sl], vc_ref[0, 0, rows, sl]], axis=0)
                bias = bias_first_ref[0, hp]
            else:
                keys = slice((sub - 1) * Q_BLOCK, (sub + 1) * Q_BLOCK)
                q_tile, kk, vv = q_ref[0, 0, rows, sl], kc_ref[0, 0, keys, sl], vc_ref[0, 0, keys, sl]
                bias = bias_ref[0, hp]
            o, m, l = _band_attend(q_tile, kk, vv, bias)
            if with_prev:
                o_ref[0, 0, rows, sl] = _merge_heads(o).astype(o_ref.dtype)
            else:
                o_ref[0, sub, :, sl] = _merge_heads(o).astype(o_ref.dtype)
            for head, half in ((2 * hp, slice(0, Q_BLOCK)), (2 * hp + 1, slice(Q_BLOCK, None))):
                m_stats = jnp.where(lane == head, m[half], m_stats)
                l_stats = jnp.where(lane == head, l[half], l_stats)
        stats = jnp.concatenate([m_stats, l_stats], axis=1)
        if with_prev:
            s_ref[0, 0, rows, :] = stats
        else:
            s_ref[0, sub] = stats


def _dilated_group(q, k, v, rel_bias, group):
    window, dilation = DILATED_GROUPS[group]
    batch, _, strided, d = q.shape
    assert strided % Q_BLOCK == 0
    nb = strided // Q_BLOCK
    with_prev = nb > 1
    bias = _band_bias(rel_bias, window, dilation)
    if with_prev:
        assert nb % DILATED_SUBS == 0
        key = jnp.arange(2 * Q_BLOCK)[None, None, :]
        bias = jnp.stack([jnp.where(key < Q_BLOCK, NEG_INF, bias), bias])
        grid = (batch, dilation, nb // DILATED_SUBS)
        two = pl.BlockSpec((1, 1, DILATED_SUBS * Q_BLOCK, d), lambda b, r, i: (b, r, i, 0))
        prev = pl.BlockSpec((1, 1, Q_BLOCK, d),
                            lambda b, r, i: (b, r, jnp.maximum(DILATED_SUBS * i - 1, 0), 0))
        table = (1,) + bias.shape[1:]
        in_specs = [two, prev, two, prev, two,
                    pl.BlockSpec(table, lambda b, r, i: (jnp.minimum(i, 1), 0, 0, 0)),
                    pl.BlockSpec(table, lambda b, r, i: (1, 0, 0, 0))]
        args = [q, k, k, v, v, bias, bias]
        out_specs = [two, pl.BlockSpec((1, 1, DILATED_SUBS * Q_BLOCK, 2 * LANES),
                                       lambda b, r, i: (b, r, i, 0))]
    else:
        assert dilation % DILATED_SUBS == 0
        bias = bias[:, :, Q_BLOCK:]
        grid = (batch, dilation // DILATED_SUBS, 1)
        two = pl.BlockSpec((1, DILATED_SUBS, Q_BLOCK, d), lambda b, r, i: (b, r, 0, 0))
        in_specs = [two, two, two, pl.BlockSpec(bias.shape, lambda b, r, i: (0, 0, 0))]
        args = [q, k, v, bias]
        out_specs = [two, pl.BlockSpec((1, DILATED_SUBS, Q_BLOCK, 2 * LANES), lambda b, r, i: (b, r, 0, 0))]
    return pl.pallas_call(
        functools.partial(_dilated_kernel, with_prev=with_prev),
        grid=grid,
        in_specs=in_specs,
        out_specs=out_specs,
        out_shape=[jax.ShapeDtypeStruct((batch, dilation, strided, d), BF16),
                   jax.ShapeDtypeStruct((batch, dilation, strided, 2 * LANES), F32)],
        compiler_params=pltpu.CompilerParams(dimension_semantics=("parallel", "parallel", "parallel"),
                                             vmem_limit_bytes=VMEM_LIMIT_BYTES),
        name=f"dilated_group{group}",
    )(*args)


def _merge_groups(o_refs, s_refs, e_ref, o_tok_refs, s_tok_refs):
    outs, tops, dens = [], [], []
    for g, (_, dil) in enumerate(DILATED_GROUPS):
        if dil == 1:
            outs.append(o_refs[g][0, 0].astype(F32))
            tops.append(s_refs[g][0, 0, :, :LANES])
            dens.append(s_refs[g][0, 0, :, LANES:])
        else:
            n = o_refs[g].shape[2]
            for r in range(dil):
                rows = pl.ds(r, n, stride=dil)
                part = o_refs[g][0, r].astype(F32)
                for cb in range(N_PAIRS):
                    o_tok_refs[g][cb, rows, :] = part[:, cb * LANES:(cb + 1) * LANES]
                s_tok_refs[g][0, rows, :] = s_refs[g][0, r, :, :LANES]
                s_tok_refs[g][1, rows, :] = s_refs[g][0, r, :, LANES:]
            outs.append(jnp.concatenate([o_tok_refs[g][cb] for cb in range(N_PAIRS)], axis=1))
            tops.append(s_tok_refs[g][0])
            dens.append(s_tok_refs[g][1])
    top = functools.reduce(jnp.maximum, tops)
    es = [jnp.exp(m - top) for m in tops]
    inv = 1.0 / functools.reduce(jnp.add, [e * l for e, l in zip(es, dens)])
    merged = None
    for out, e in zip(outs, es):
        wt = e * inv
        hi = wt.astype(BF16)
        lo = (wt - hi.astype(F32)).astype(BF16)
        spread = jnp.dot(jnp.concatenate([hi, lo], axis=1), e_ref[...], preferred_element_type=F32)
        merged = spread * out if merged is None else merged + spread * out
    return merged.astype(BF16)


def _oproj_ffn_kernel(*refs, merge, n_cast):
    if merge:
        o_refs, s_refs, e_ref = refs[0:N_GROUPS], refs[N_GROUPS:2 * N_GROUPS], refs[2 * N_GROUPS]
        refs = refs[2 * N_GROUPS + 1:]
        strided_groups = [g for g, (_, dil) in enumerate(DILATED_GROUPS) if dil > 1]
        n_tok = len(strided_groups)
        o_tok_refs = dict(zip(strided_groups, refs[-2 * n_tok:-n_tok]))
        s_tok_refs = dict(zip(strided_groups, refs[-n_tok:]))
        refs = refs[:-2 * n_tok]
    else:
        o_ref, refs = refs[0], refs[1:]
    x_ref, wo_ref, g_ref, wgu_ref, wd_ref = refs[:5]
    cast_in, out_ref, cast_out = refs[5:5 + n_cast], refs[5 + n_cast], refs[6 + n_cast:6 + 2 * n_cast]
    xmid_ref, xn_ref, hid_ref = refs[6 + 2 * n_cast:]
    for src, dst in zip(cast_in, cast_out):
        dst[...] = src[...].astype(BF16)
    if merge:
        attn = _merge_groups(o_refs, s_refs, e_ref, o_tok_refs, s_tok_refs)
    else:
        attn = o_ref[...]
    h = jnp.dot(attn, wo_ref[...], preferred_element_type=F32)
    xm = x_ref[...] + _rms_scale(h, g_ref[1:2, :])
    xmid_ref[...] = xm
    xn_ref[...] = _rms_scale(xm, g_ref[2:3, :]).astype(BF16)
    for c in range(D_FF // FFN_COLS):
        lo = c * FFN_COLS
        xn = xn_ref[...]
        gate = jnp.dot(xn, wgu_ref[:, lo:lo + FFN_COLS], preferred_element_type=F32)
        up = jnp.dot(xn, wgu_ref[:, D_FF + lo:D_FF + lo + FFN_COLS], preferred_element_type=F32)
        hid_ref[:, lo:lo + FFN_COLS] = (gate * jax.nn.sigmoid(gate) * up).astype(BF16)
    down = jnp.dot(hid_ref[...], wd_ref[...], preferred_element_type=F32)
    out_ref[...] = xmid_ref[...] + _rms_scale(down, g_ref[3:4, :])


def _cast_specs(to_cast, n_steps):
    in_specs, out_specs, out_shape, args = [], [], [], []
    for w, layer in to_cast:
        rows, cols = w.shape[-2:]
        n_blocks = n_steps
        while n_blocks > 1 and (rows % n_blocks or (rows // n_blocks) % 16):
            n_blocks //= 2
        assert n_steps % n_blocks == 0
        steps_per_block = n_steps // n_blocks
        block = (rows // n_blocks, cols)
        if layer is None:
            in_specs.append(pl.BlockSpec(block, lambda i, spb=steps_per_block: (i // spb, 0)))
        else:
            in_specs.append(pl.BlockSpec((None,) + block,
                                         lambda i, spb=steps_per_block, li=layer: (li, i // spb, 0)))
        out_specs.append(pl.BlockSpec(block, lambda i, spb=steps_per_block: (i // spb, 0)))
        out_shape.append(jax.ShapeDtypeStruct((rows, cols), BF16))
        args.append(w)
    return in_specs, out_specs, out_shape, args


def _oproj_ffn(attn, x2d, w_o, gains, w_gate_up, w_down, seq, to_cast=()):
    t, d = x2d.shape
    tm = FFN_TOKENS
    assert t % tm == 0 and seq % tm == 0 and D_FF % FFN_COLS == 0
    merge = not isinstance(attn, jax.Array)
    cast_in, cast_out, cast_shape, cast_args = _cast_specs(to_cast, t // tm)
    scratch = [pltpu.VMEM((tm, d), F32), pltpu.VMEM((tm, d), BF16), pltpu.VMEM((tm, D_FF), BF16)]
    resident = lambda shape: pl.BlockSpec(shape, lambda i: (0, 0), pipeline_mode=pl.Buffered(1))
    tok = pl.BlockSpec((tm, d), lambda i: (i, 0))
    if merge:
        tiles = seq // tm
        assert all(tm % (16 * dil) == 0 for _, dil in DILATED_GROUPS)
        blk = lambda dil, cols: pl.BlockSpec((1, dil, tm // dil, cols),
                                             lambda i: (i // tiles, 0, i % tiles, 0))
        head = jnp.arange(2 * LANES)[:, None] % LANES
        col = jnp.arange(d)[None, :] // HEAD_DIM
        spread = (head == col).astype(BF16)
        attn_specs = ([blk(dil, d) for _, dil in DILATED_GROUPS]
                      + [blk(dil, 2 * LANES) for _, dil in DILATED_GROUPS]
                      + [resident(spread.shape)])
        attn_args = [o for o, _ in attn] + [s for _, s in attn] + [spread]
        n_strided = sum(dil > 1 for _, dil in DILATED_GROUPS)
        scratch += ([pltpu.VMEM((d // LANES, tm, LANES), F32)] * n_strided
                    + [pltpu.VMEM((2, tm, LANES), F32)] * n_strided)
    else:
        attn_specs = [tok]
        attn_args = [attn]
    out, *cast = pl.pallas_call(
        functools.partial(_oproj_ffn_kernel, merge=merge, n_cast=len(cast_args)),
        grid=(t // tm,),
        in_specs=attn_specs + [
            tok,
            resident(w_o.shape),
            resident(gains.shape),
            resident(w_gate_up.shape),
            resident(w_down.shape)] + cast_in,
        out_specs=[tok] + cast_out,
        out_shape=[jax.ShapeDtypeStruct((t, d), F32)] + cast_shape,
        scratch_shapes=scratch,
        compiler_params=pltpu.CompilerParams(dimension_semantics=("arbitrary",),
                                             vmem_limit_bytes=VMEM_LIMIT_BYTES),
        name="oproj_ffn_merge" if merge else "oproj_ffn",
    )(*attn_args, x2d, w_o, gains, w_gate_up, w_down, *cast_args)
    return out, cast


def kernel(x, norm_gains, w_qkv_a, w_o_a, g_kv, w_kv_b, w_q_b, w_o_b, rel_bias, w_gate_up, w_down):
    batch, seq, d = x.shape
    depth = norm_gains.shape[0]
    n_a = w_qkv_a.shape[0]
    t = batch * seq
    x2d = x.reshape(t, d)

    def layer_weights(layer):
        if layer < n_a:
            own = {"qkv": (w_qkv_a, layer), "o": (w_o_a, layer)}
        else:
            own = {"q": (w_q_b, layer - n_a), "o": (w_o_b, layer - n_a)}
            if layer == n_a:
                own["kv"] = (w_kv_b, None)
        return {**own, "gate_up": (w_gate_up, layer), "down": (w_down, layer)}

    bf16 = {name: (w if idx is None else w[idx]).astype(BF16)
            for name, (w, idx) in layer_weights(0).items()}
    shared_kv = None
    for layer in range(depth):
        g = norm_gains[layer]
        if layer < n_a:
            qkv = _norm_proj(x2d, g[0], bf16["qkv"], pair_major=True)
            attn = _stick_attention(qkv, batch, seq).reshape(t, d)
        else:
            if shared_kv is None:
                q, shared_kv = _norm_proj_groups(x2d, [g[0], g_kv], [bf16["q"], bf16["kv"]], batch, seq)
            else:
                (q,) = _norm_proj_groups(x2d, [g[0]], [bf16["q"]], batch, seq)
            attn = [_dilated_group(q[grp], shared_kv[2 * grp], shared_kv[2 * grp + 1], rel_bias, grp)
                    for grp in range(N_GROUPS)]
        following = layer_weights(layer + 1) if layer + 1 < depth else {}
        x2d, cast = _oproj_ffn(attn, x2d, bf16["o"], g, bf16["gate_up"], bf16["down"], seq,
                               to_cast=list(following.values()))
        bf16 = dict(zip(following, cast))
    return x2d.reshape(batch, seq, d)
```

```python
import functools
import math

import jax
import jax.numpy as jnp
from jax import lax
from jax.experimental import pallas as pl
from jax.experimental.pallas import tpu as pltpu

D_MODEL = 1024
N_HEADS = 16
HEAD_DIM = D_MODEL // N_HEADS
LANES = 128
N_PAIRS = D_MODEL // LANES
Q_BLOCK = 128
DILATED_GROUPS = ((128, 1), (512, 4), (2048, 16))
N_GROUPS = len(DILATED_GROUPS)
NUM_BUCKETS = 32
MAX_DISTANCE = 2048
D_FF = -(-8 * D_MODEL // (3 * 256)) * 256
RMS_EPS = 1e-6
NEG_INF = -1e30
QK_SCALE = HEAD_DIM ** -0.5

VMEM_LIMIT_BYTES = 56 * 1024 * 1024
PROJ_TOKENS = 512
PROJ_COLS = 512
FFN_TOKENS = 512
FFN_COLS = 256
DILATED_SUBS = 4
STICK_QUERIES = 256
STICK_KEYS = 256
STICK_UNDERFLOW = 128.0

F32 = jnp.float32
BF16 = jnp.bfloat16


def _rms_scale(x, gain_row):
    ms = jnp.mean(x * x, axis=-1, keepdims=True)
    return x * lax.rsqrt(ms + RMS_EPS) * gain_row


def _split_heads(tile):
    lane = lax.broadcasted_iota(jnp.int32, tile.shape, 1)
    zero = jnp.zeros_like(tile)
    return jnp.concatenate([jnp.where(lane < HEAD_DIM, tile, zero),
                            jnp.where(lane >= HEAD_DIM, tile, zero)], axis=0)


def _merge_heads(stacked):
    rows = stacked.shape[0] // 2
    lane = lax.broadcasted_iota(jnp.int32, (rows, LANES), 1)
    return jnp.where(lane < HEAD_DIM, stacked[:rows], stacked[rows:])


def _norm_proj_kernel(x_ref, g_ref, w_ref, *rest, n_cast):
    cast_in, o_ref, cast_out = rest[:n_cast], rest[n_cast], rest[n_cast + 1:]
    for src, dst in zip(cast_in, cast_out):
        dst[...] = src[...].astype(BF16)
    xn = _rms_scale(x_ref[...], g_ref[...]).astype(BF16)
    n_out = w_ref.shape[1]
    for c in range(n_out // PROJ_COLS):
        lo = c * PROJ_COLS
        res = jnp.dot(xn, w_ref[:, lo:lo + PROJ_COLS], preferred_element_type=F32).astype(o_ref.dtype)
        for k in range(PROJ_COLS // LANES):
            o_ref[lo // LANES + k] = res[:, k * LANES:(k + 1) * LANES]


def _norm_proj(x2d, gain, w, to_cast=()):
    t, d = x2d.shape
    n = w.shape[1]
    assert t % PROJ_TOKENS == 0 and n % PROJ_COLS == 0
    cast_in, cast_out, cast_shape, cast_args = _cast_specs(to_cast, t // PROJ_TOKENS)
    out, *cast = pl.pallas_call(
        functools.partial(_norm_proj_kernel, n_cast=len(cast_args)),
        grid=(t // PROJ_TOKENS,),
        in_specs=[pl.BlockSpec((PROJ_TOKENS, d), lambda i: (i, 0)),
                  pl.BlockSpec((1, d), lambda i: (0, 0)),
                  pl.BlockSpec((d, n), lambda i: (0, 0), pipeline_mode=pl.Buffered(1))] + cast_in,
        out_specs=[pl.BlockSpec((n // LANES, PROJ_TOKENS, LANES), lambda i: (0, i, 0))] + cast_out,
        out_shape=[jax.ShapeDtypeStruct((n // LANES, t, LANES), BF16)] + cast_shape,
        compiler_params=pltpu.CompilerParams(dimension_semantics=("arbitrary",),
                                             vmem_limit_bytes=VMEM_LIMIT_BYTES),
        name="norm_proj",
    )(x2d, gain.reshape(1, d), w, *cast_args)
    return out, cast


def _norm_proj_groups_kernel(x_ref, g_ref, *rest, chunks):
    n_proj = len(chunks)
    w_refs, out_refs, (xs_ref, xp_ref) = rest[:n_proj], rest[n_proj:-2], rest[-2:]
    tm = x_ref.shape[0]
    x = x_ref[...]
    xs = x * lax.rsqrt(jnp.mean(x * x, axis=-1, keepdims=True) + RMS_EPS)
    for cb in range(N_PAIRS):
        xs_ref[cb] = xs[:, cb * LANES:(cb + 1) * LANES]
    first_out = [sum(chunks[:p]) * N_GROUPS for p in range(n_proj)]
    for g, (_, dil) in enumerate(DILATED_GROUPS):
        n = tm // dil
        if dil > 1:
            for r in range(dil):
                for cb in range(N_PAIRS):
                    xp_ref[r * n:(r + 1) * n, cb * LANES:(cb + 1) * LANES] = (
                        xs_ref[cb, pl.ds(r, n, stride=dil), :])
        for p in range(n_proj):
            rows = xs if dil == 1 else xp_ref[...]
            xn = (rows * g_ref[p:p + 1, :]).astype(BF16)
            for c in range(chunks[p]):
                col = (g * chunks[p] + c) * D_MODEL
                res = jnp.dot(xn, w_refs[p][:, col:col + D_MODEL],
                              preferred_element_type=F32).astype(BF16)
                out = out_refs[first_out[p] + g * chunks[p] + c]
                for r in range(dil):
                    out[0, r] = res[r * n:(r + 1) * n]


def _norm_proj_groups(x2d, gains, ws, batch, seq):
    t, d = x2d.shape
    chunks = tuple(w.shape[1] // (N_GROUPS * d) for w in ws)
    tiles = seq // PROJ_TOKENS
    assert seq % PROJ_TOKENS == 0 and all(PROJ_TOKENS % (16 * dil) == 0 for _, dil in DILATED_GROUPS)
    out_shape, out_specs = [], []
    for n_chunks in chunks:
        for _, dil in DILATED_GROUPS:
            for _ in range(n_chunks):
                out_shape.append(jax.ShapeDtypeStruct((batch, dil, seq // dil, d), BF16))
                out_specs.append(pl.BlockSpec((1, dil, PROJ_TOKENS // dil, d), lambda b, j: (b, 0, j, 0)))
    outs = pl.pallas_call(
        functools.partial(_norm_proj_groups_kernel, chunks=chunks),
        grid=(batch, tiles),
        in_specs=([pl.BlockSpec((PROJ_TOKENS, d), lambda b, j: (b * tiles + j, 0)),
                   pl.BlockSpec((len(ws), d), lambda b, j: (0, 0))]
                  + [pl.BlockSpec(w.shape, lambda b, j: (0, 0), pipeline_mode=pl.Buffered(1)) for w in ws]),
        out_specs=out_specs,
        out_shape=out_shape,
        scratch_shapes=[pltpu.VMEM((d // LANES, PROJ_TOKENS, LANES), F32),
                        pltpu.VMEM((PROJ_TOKENS, d), F32)],
        compiler_params=pltpu.CompilerParams(dimension_semantics=("parallel", "parallel"),
                                             vmem_limit_bytes=VMEM_LIMIT_BYTES),
        name="norm_proj_groups",
    )(x2d, jnp.stack(gains), *ws)
    split, start = [], 0
    for n_chunks in chunks:
        split.append(outs[start:start + n_chunks * N_GROUPS])
        start += n_chunks * N_GROUPS
    return split


def _stick_kernel(q_ref, k_ref, v_ref, u_ref, o_ref, acc_ref, carry_ref):
    rows = 2 * STICK_QUERIES
    nq = q_ref.shape[2] // STICK_QUERIES

    def span(ref, idx):
        return ref[0, 0, idx * STICK_KEYS:(idx + 1) * STICK_KEYS, :]

    def weights(z, carry, diagonal):
        cost = jnp.maximum(z, 0.0) + jnp.log(1.0 + jnp.exp(-jnp.abs(z)))
        if diagonal:
            qrow = lax.broadcasted_iota(jnp.int32, z.shape, 0) % STICK_QUERIES
            kcol = lax.broadcasted_iota(jnp.int32, z.shape, 1)
            strict = kcol < qrow
            cost = jnp.where(strict, cost, 0.0)
        w = z - jnp.dot(cost.astype(BF16), u_ref[...], preferred_element_type=F32)
        a = jnp.concatenate([jnp.exp(w[:, :LANES] - carry), jnp.exp(w[:, LANES:] - carry)], axis=1)
        if diagonal:
            a = jnp.where(strict, a, 0.0)
        return a.astype(BF16), carry + jnp.sum(cost, axis=-1, keepdims=True)

    def walk(qi, spans, acc, carry):
        qs = _split_heads(span(q_ref, qi)) * QK_SCALE
        for j in spans:
            z = lax.dot_general(qs, span(k_ref, j), (((1,), (1,)), ((), ())),
                                preferred_element_type=F32)
            a, carry = weights(z, carry, j == qi)
            acc = acc + jnp.dot(a, span(v_ref, j), preferred_element_type=F32)
        return acc, carry

    alive = {}
    for qi in range(nq):
        zeros = jnp.zeros((rows, LANES), F32)
        acc, carry = walk(qi, range(qi, max(qi - 2, -1), -1), zeros, zeros)
        acc_ref[qi] = acc
        if qi >= 2:
            carry_ref[qi] = carry
            alive[qi] = jnp.min(carry, axis=0, keepdims=True)[0, 0] < STICK_UNDERFLOW

    for qi in range(2, nq):
        @pl.when(alive[qi])
        def _(qi=qi):
            acc, _ = walk(qi, range(qi - 2, -1, -1), acc_ref[qi], carry_ref[qi])
            acc_ref[qi] = acc

    for qi in range(nq):
        o_ref[0, qi * STICK_QUERIES:(qi + 1) * STICK_QUERIES, :] = (
            _merge_heads(acc_ref[qi]).astype(o_ref.dtype))


def _stick_attention(qkv, batch, seq):
    assert STICK_QUERIES == STICK_KEYS and seq % STICK_QUERIES == 0
    qkv = qkv.reshape(3 * N_PAIRS, batch, seq, LANES)
    assert STICK_KEYS == 2 * LANES
    j = jnp.arange(STICK_KEYS)[:, None]
    s = jnp.arange(STICK_KEYS)[None, :]
    uu = (j >= s).astype(BF16)
    return pl.pallas_call(
        _stick_kernel,
        grid=(batch, N_PAIRS),
        in_specs=[pl.BlockSpec((1, 1, seq, LANES), lambda b, p: (p, b, 0, 0)),
                  pl.BlockSpec((1, 1, seq, LANES), lambda b, p: (N_PAIRS + p, b, 0, 0)),
                  pl.BlockSpec((1, 1, seq, LANES), lambda b, p: (2 * N_PAIRS + p, b, 0, 0)),
                  pl.BlockSpec((STICK_KEYS, STICK_KEYS), lambda b, p: (0, 0))],
        out_specs=pl.BlockSpec((1, seq, LANES), lambda b, p: (b, 0, p)),
        out_shape=jax.ShapeDtypeStruct((batch, seq, D_MODEL), BF16),
        scratch_shapes=[pltpu.VMEM((seq // STICK_QUERIES, 2 * STICK_QUERIES, LANES), F32),
                        pltpu.VMEM((seq // STICK_QUERIES, 2 * STICK_QUERIES, LANES), F32)],
        compiler_params=pltpu.CompilerParams(dimension_semantics=("parallel", "parallel"),
                                             vmem_limit_bytes=VMEM_LIMIT_BYTES),
        name="stick_attention",
    )(qkv, qkv, qkv, uu)


def _relative_bucket(distance):
    max_exact = NUM_BUCKETS // 2
    n = jnp.maximum(distance, 0)
    large = max_exact + (jnp.log(jnp.maximum(n, 1).astype(F32) / max_exact)
                         / math.log(MAX_DISTANCE / max_exact)
                         * (NUM_BUCKETS - max_exact)).astype(jnp.int32)
    large = jnp.minimum(large, NUM_BUCKETS - 1)
    return jnp.where(n < max_exact, n, large)


def _band_bias(rel_bias, window, dilation):
    i = jnp.arange(Q_BLOCK)[:, None]
    m = jnp.arange(2 * Q_BLOCK)[None, :]
    rel = Q_BLOCK + i - m
    band = (rel >= 0) & (rel <= window // dilation)
    onehot = jax.nn.one_hot(_relative_bucket(rel * dilation), NUM_BUCKETS, dtype=F32)
    bias = jnp.einsum('qkb,bh->hqk', onehot, rel_bias.astype(F32), precision=lax.Precision.HIGHEST)
    bias = jnp.where(band[None], bias, NEG_INF)
    return bias.reshape(N_PAIRS, 2 * Q_BLOCK, 2 * Q_BLOCK)


def _band_attend(q_tile, kk, vv, bias):
    qs = _split_heads(q_tile) * QK_SCALE
    s = lax.dot_general(qs, kk, (((1,), (1,)), ((), ())), preferred_element_type=F32) + bias
    m = jnp.max(s, axis=-1, keepdims=True)
    p = jnp.exp(s - m)
    l = jnp.sum(p, axis=-1, keepdims=True)
    return jnp.dot(p.astype(BF16), vv, preferred_element_type=F32), m, l


def _dilated_kernel(*refs, with_prev):
    if with_prev:
        q_ref, kp_ref, kc_ref, vp_ref, vc_ref, bias_first_ref, bias_ref, o_ref, s_ref = refs
    else:
        q_ref, kc_ref, vc_ref, bias_ref, o_ref, s_ref = refs
    for sub in range(DILATED_SUBS):
        rows = slice(sub * Q_BLOCK, (sub + 1) * Q_BLOCK)
        stats_at = (lambda cols: (0, 0, rows, cols)) if with_prev else (lambda cols: (0, sub, slice(None), cols))
        s_ref[stats_at(slice(None))] = jnp.concatenate(
            [jnp.zeros((Q_BLOCK, LANES), F32), jnp.ones((Q_BLOCK, LANES), F32)], axis=1)
        for hp in range(N_PAIRS):
            sl = slice(hp * LANES, (hp + 1) * LANES)
            if not with_prev:
                q_tile, kk, vv = q_ref[0, sub, :, sl], kc_ref[0, sub, :, sl], vc_ref[0, sub, :, sl]
                bias = bias_ref[hp]
            elif sub == 0:
                q_tile = q_ref[0, 0, rows, sl]
                kk = jnp.concatenate([kp_ref[0, 0, :, sl], kc_ref[0, 0, rows, sl]], axis=0)
                vv = jnp.concatenate([vp_ref[0, 0, :, sl], vc_ref[0, 0, rows, sl]], axis=0)
                bias = bias_first_ref[0, hp]
            else:
                keys = slice((sub - 1) * Q_BLOCK, (sub + 1) * Q_BLOCK)
                q_tile, kk, vv = q_ref[0, 0, rows, sl], kc_ref[0, 0, keys, sl], vc_ref[0, 0, keys, sl]
                bias = bias_ref[0, hp]
            o, m, l = _band_attend(q_tile, kk, vv, bias)
            if with_prev:
                o_ref[0, 0, rows, sl] = _merge_heads(o).astype(o_ref.dtype)
            else:
                o_ref[0, sub, :, sl] = _merge_heads(o).astype(o_ref.dtype)
            for head, half in ((2 * hp, slice(0, Q_BLOCK)), (2 * hp + 1, slice(Q_BLOCK, None))):
                s_ref[stats_at(slice(head, head + 1))] = m[half]
                s_ref[stats_at(slice(LANES + head, LANES + head + 1))] = l[half]


def _dilated_group(q, k, v, rel_bias, group):
    window, dilation = DILATED_GROUPS[group]
    batch, _, strided, d = q.shape
    assert strided % Q_BLOCK == 0
    nb = strided // Q_BLOCK
    with_prev = nb > 1
    bias = _band_bias(rel_bias, window, dilation)
    if with_prev:
        assert nb % DILATED_SUBS == 0
        key = jnp.arange(2 * Q_BLOCK)[None, None, :]
        bias = jnp.stack([jnp.where(key < Q_BLOCK, NEG_INF, bias), bias])
        grid = (batch, dilation, nb // DILATED_SUBS)
        two = pl.BlockSpec((1, 1, DILATED_SUBS * Q_BLOCK, d), lambda b, r, i: (b, r, i, 0))
        prev = pl.BlockSpec((1, 1, Q_BLOCK, d),
                            lambda b, r, i: (b, r, jnp.maximum(DILATED_SUBS * i - 1, 0), 0))
        table = (1,) + bias.shape[1:]
        in_specs = [two, prev, two, prev, two,
                    pl.BlockSpec(table, lambda b, r, i: (jnp.minimum(i, 1), 0, 0, 0)),
                    pl.BlockSpec(table, lambda b, r, i: (1, 0, 0, 0))]
        args = [q, k, k, v, v, bias, bias]
        out_specs = [two, pl.BlockSpec((1, 1, DILATED_SUBS * Q_BLOCK, 2 * LANES),
                                       lambda b, r, i: (b, r, i, 0))]
    else:
        assert dilation % DILATED_SUBS == 0
        bias = bias[:, :, Q_BLOCK:]
        grid = (batch, dilation // DILATED_SUBS, 1)
        two = pl.BlockSpec((1, DILATED_SUBS, Q_BLOCK, d), lambda b, r, i: (b, r, 0, 0))
        in_specs = [two, two, two, pl.BlockSpec(bias.shape, lambda b, r, i: (0, 0, 0))]
        args = [q, k, v, bias]
        out_specs = [two, pl.BlockSpec((1, DILATED_SUBS, Q_BLOCK, 2 * LANES), lambda b, r, i: (b, r, 0, 0))]
    return pl.pallas_call(
        functools.partial(_dilated_kernel, with_prev=with_prev),
        grid=grid,
        in_specs=in_specs,
        out_specs=out_specs,
        out_shape=[jax.ShapeDtypeStruct((batch, dilation, strided, d), BF16),
                   jax.ShapeDtypeStruct((batch, dilation, strided, 2 * LANES), F32)],
        compiler_params=pltpu.CompilerParams(dimension_semantics=("parallel", "parallel", "parallel"),
                                             vmem_limit_bytes=VMEM_LIMIT_BYTES),
        name=f"dilated_group{group}",
    )(*args)


def _merge_groups(o_refs, s_refs, e_ref, o_tok_refs, s_tok_refs):
    outs, tops, dens = [], [], []
    for g, (_, dil) in enumerate(DILATED_GROUPS):
        if dil == 1:
            outs.append(o_refs[g][0, 0].astype(F32))
            tops.append(s_refs[g][0, 0, :, :LANES])
            dens.append(s_refs[g][0, 0, :, LANES:])
        else:
            n = o_refs[g].shape[2]
            for r in range(dil):
                rows = pl.ds(r, n, stride=dil)
                part = o_refs[g][0, r].astype(F32)
                for cb in range(N_PAIRS):
                    o_tok_refs[g][cb, rows, :] = part[:, cb * LANES:(cb + 1) * LANES]
                s_tok_refs[g][0, rows, :] = s_refs[g][0, r, :, :LANES]
                s_tok_refs[g][1, rows, :] = s_refs[g][0, r, :, LANES:]
            outs.append(jnp.concatenate([o_tok_refs[g][cb] for cb in range(N_PAIRS)], axis=1))
            tops.append(s_tok_refs[g][0])
            dens.append(s_tok_refs[g][1])
    top = functools.reduce(jnp.maximum, tops)
    es = [jnp.exp(m - top) for m in tops]
    inv = 1.0 / functools.reduce(jnp.add, [e * l for e, l in zip(es, dens)])
    merged = None
    for out, e in zip(outs, es):
        wt = e * inv
        hi = wt.astype(BF16)
        lo = (wt - hi.astype(F32)).astype(BF16)
        spread = jnp.dot(jnp.concatenate([hi, lo], axis=1), e_ref[...], preferred_element_type=F32)
        merged = spread * out if merged is None else merged + spread * out
    return merged.astype(BF16)


def _oproj_ffn_kernel(*refs, merge, n_cast):
    if merge:
        o_refs, s_refs, e_ref = refs[0:N_GROUPS], refs[N_GROUPS:2 * N_GROUPS], refs[2 * N_GROUPS]
        refs = refs[2 * N_GROUPS + 1:]
        strided_groups = [g for g, (_, dil) in enumerate(DILATED_GROUPS) if dil > 1]
        n_tok = len(strided_groups)
        o_tok_refs = dict(zip(strided_groups, refs[-2 * n_tok:-n_tok]))
        s_tok_refs = dict(zip(strided_groups, refs[-n_tok:]))
        refs = refs[:-2 * n_tok]
    else:
        o_ref, refs = refs[0], refs[1:]
    x_ref, wo_ref, g_ref, wgu_ref, wd_ref = refs[:5]
    cast_in, out_ref, cast_out = refs[5:5 + n_cast], refs[5 + n_cast], refs[6 + n_cast:6 + 2 * n_cast]
    xmid_ref, xn_ref, hid_ref = refs[6 + 2 * n_cast:]
    for src, dst in zip(cast_in, cast_out):
        dst[...] = src[...].astype(BF16)
    if merge:
        attn = _merge_groups(o_refs, s_refs, e_ref, o_tok_refs, s_tok_refs)
    else:
        attn = o_ref[...]
    h = jnp.dot(attn, wo_ref[...], preferred_element_type=F32)
    xm = x_ref[...] + _rms_scale(h, g_ref[1:2, :])
    xmid_ref[...] = xm
    xn_ref[...] = _rms_scale(xm, g_ref[2:3, :]).astype(BF16)
    for c in range(D_FF // FFN_COLS):
        lo = c * FFN_COLS
        xn = xn_ref[...]
        gate = jnp.dot(xn, wgu_ref[:, lo:lo + FFN_COLS], preferred_element_type=F32)
        up = jnp.dot(xn, wgu_ref[:, D_FF + lo:D_FF + lo + FFN_COLS], preferred_element_type=F32)
        hid_ref[:, lo:lo + FFN_COLS] = (gate * jax.nn.sigmoid(gate) * up).astype(BF16)
    down = jnp.dot(hid_ref[...], wd_ref[...], preferred_element_type=F32)
    out_ref[...] = xmid_ref[...] + _rms_scale(down, g_ref[3:4, :])


def _cast_specs(to_cast, n_steps):
    in_specs, out_specs, out_shape, args = [], [], [], []
    for w, layer in to_cast:
        rows, cols = w.shape[-2:]
        n_blocks = n_steps
        while n_blocks > 1 and (rows % n_blocks or (rows // n_blocks) % 16):
            n_blocks //= 2
        assert n_steps % n_blocks == 0
        steps_per_block = n_steps // n_blocks
        block = (rows // n_blocks, cols)
        if layer is None:
            in_specs.append(pl.BlockSpec(block, lambda i, spb=steps_per_block: (i // spb, 0)))
        else:
            in_specs.append(pl.BlockSpec((None,) + block,
                                         lambda i, spb=steps_per_block, li=layer: (li, i // spb, 0)))
        out_specs.append(pl.BlockSpec(block, lambda i, spb=steps_per_block: (i // spb, 0)))
        out_shape.append(jax.ShapeDtypeStruct((rows, cols), BF16))
        args.append(w)
    return in_specs, out_specs, out_shape, args


def _oproj_ffn(attn, x2d, w_o, gains, w_gate_up, w_down, seq, to_cast=()):
    t, d = x2d.shape
    tm = FFN_TOKENS
    assert t % tm == 0 and seq % tm == 0 and D_FF % FFN_COLS == 0
    merge = not isinstance(attn, jax.Array)
    cast_in, cast_out, cast_shape, cast_args = _cast_specs(to_cast, t // tm)
    scratch = [pltpu.VMEM((tm, d), F32), pltpu.VMEM((tm, d), BF16), pltpu.VMEM((tm, D_FF), BF16)]
    resident = lambda shape: pl.BlockSpec(shape, lambda i: (0, 0), pipeline_mode=pl.Buffered(1))
    tok = pl.BlockSpec((tm, d), lambda i: (i, 0))
    if merge:
        tiles = seq // tm
        assert all(tm % (16 * dil) == 0 for _, dil in DILATED_GROUPS)
        blk = lambda dil, cols: pl.BlockSpec((1, dil, tm // dil, cols),
                                             lambda i: (i // tiles, 0, i % tiles, 0))
        head = jnp.arange(2 * LANES)[:, None] % LANES
        col = jnp.arange(d)[None, :] // HEAD_DIM
        spread = (head == col).astype(BF16)
        attn_specs = ([blk(dil, d) for _, dil in DILATED_GROUPS]
                      + [blk(dil, 2 * LANES) for _, dil in DILATED_GROUPS]
                      + [resident(spread.shape)])
        attn_args = [o for o, _ in attn] + [s for _, s in attn] + [spread]
        n_strided = sum(dil > 1 for _, dil in DILATED_GROUPS)
        scratch += ([pltpu.VMEM((d // LANES, tm, LANES), F32)] * n_strided
                    + [pltpu.VMEM((2, tm, LANES), F32)] * n_strided)
    else:
        attn_specs = [tok]
        attn_args = [attn]
    out, *cast = pl.pallas_call(
        functools.partial(_oproj_ffn_kernel, merge=merge, n_cast=len(cast_args)),
        grid=(t // tm,),
        in_specs=attn_specs + [
            tok,
            resident(w_o.shape),
            resident(gains.shape),
            resident(w_gate_up.shape),
            resident(w_down.shape)] + cast_in,
        out_specs=[tok] + cast_out,
        out_shape=[jax.ShapeDtypeStruct((t, d), F32)] + cast_shape,
        scratch_shapes=scratch,
        compiler_params=pltpu.CompilerParams(dimension_semantics=("arbitrary",),
                                             vmem_limit_bytes=VMEM_LIMIT_BYTES),
        name="oproj_ffn_merge" if merge else "oproj_ffn",
    )(*attn_args, x2d, w_o, gains, w_gate_up, w_down, *cast_args)
    return out, cast


def kernel(x, norm_gains, w_qkv_a, w_o_a, g_kv, w_kv_b, w_q_b, w_o_b, rel_bias, w_gate_up, w_down):
    batch, seq, d = x.shape
    depth = norm_gains.shape[0]
    n_a = w_qkv_a.shape[0]
    t = batch * seq
    x2d = x.reshape(t, d)

    def layer_weights(layer):
        if layer < n_a:
            own = {"qkv": (w_qkv_a, layer), "o": (w_o_a, layer)}
        else:
            own = {"q": (w_q_b, layer - n_a), "o": (w_o_b, layer - n_a)}
            if layer == n_a:
                own["kv"] = (w_kv_b, None)
        return {**own, "gate_up": (w_gate_up, layer), "down": (w_down, layer)}

    pending = layer_weights(0)
    eager = ["qkv"] if n_a > 0 else list(pending)
    bf16 = {name: (w if idx is None else w[idx]).astype(BF16)
            for name, (w, idx) in ((name, pending.pop(name)) for name in eager)}
    shared_kv = None
    for layer in range(depth):
        g = norm_gains[layer]
        if layer < n_a:
            qkv, cast = _norm_proj(x2d, g[0], bf16["qkv"], to_cast=list(pending.values()))
            bf16.update(zip(pending, cast))
            pending = {}
            attn = _stick_attention(qkv, batch, seq).reshape(t, d)
        else:
            if shared_kv is None:
                q, shared_kv = _norm_proj_groups(x2d, [g[0], g_kv], [bf16["q"], bf16["kv"]], batch, seq)
            else:
                (q,) = _norm_proj_groups(x2d, [g[0]], [bf16["q"]], batch, seq)
            attn = [_dilated_group(q[grp], shared_kv[2 * grp], shared_kv[2 * grp + 1], rel_bias, grp)
                    for grp in range(N_GROUPS)]
        following = layer_weights(layer + 1) if layer + 1 < depth else {}
        x2d, cast = _oproj_ffn(attn, x2d, bf16["o"], g, bf16["gate_up"], bf16["down"], seq,
                               to_cast=list(following.values()))
        bf16 = dict(zip(following, cast))
    return x2d.reshape(batch, seq, d)
```

```python
import functools
import math

import jax
import jax.numpy as jnp
from jax import lax
from jax.experimental import pallas as pl
from jax.experimental.pallas import tpu as pltpu

D_MODEL = 1024
N_HEADS = 16
HEAD_DIM = D_MODEL // N_HEADS
LANES = 128
N_PAIRS = D_MODEL // LANES
Q_BLOCK = 128
DILATED_GROUPS = ((128, 1), (512, 4), (2048, 16))
N_GROUPS = len(DILATED_GROUPS)
NUM_BUCKETS = 32
MAX_DISTANCE = 2048
D_FF = -(-8 * D_MODEL // (3 * 256)) * 256
RMS_EPS = 1e-6
NEG_INF = -1e30
QK_SCALE = HEAD_DIM ** -0.5

VMEM_LIMIT_BYTES = 56 * 1024 * 1024
PROJ_TOKENS = 512
PROJ_COLS = 512
FFN_TOKENS = 512
FFN_COLS = 256
DILATED_SUBS = 4
STICK_PAIRS = 2
STICK_QUERIES = 256
STICK_KEYS = 256
STICK_UNDERFLOW = 128.0

F32 = jnp.float32
BF16 = jnp.bfloat16


def _rms_scale(x, gain_row):
    ms = jnp.mean(x * x, axis=-1, keepdims=True)
    return x * lax.rsqrt(ms + RMS_EPS) * gain_row


def _split_heads(tile):
    lane = lax.broadcasted_iota(jnp.int32, tile.shape, 1)
    zero = jnp.zeros_like(tile)
    return jnp.concatenate([jnp.where(lane < HEAD_DIM, tile, zero),
                            jnp.where(lane >= HEAD_DIM, tile, zero)], axis=0)


def _merge_heads(stacked):
    rows = stacked.shape[0] // 2
    lane = lax.broadcasted_iota(jnp.int32, (rows, LANES), 1)
    return jnp.where(lane < HEAD_DIM, stacked[:rows], stacked[rows:])


def _norm_proj_kernel(x_ref, g_ref, w_ref, *rest, n_cast):
    cast_in, o_ref, cast_out = rest[:n_cast], rest[n_cast], rest[n_cast + 1:]
    for src, dst in zip(cast_in, cast_out):
        dst[...] = src[...].astype(BF16)
    xn = _rms_scale(x_ref[...], g_ref[...]).astype(BF16)
    n_out = w_ref.shape[1]
    for c in range(n_out // PROJ_COLS):
        lo = c * PROJ_COLS
        res = jnp.dot(xn, w_ref[:, lo:lo + PROJ_COLS], preferred_element_type=F32).astype(o_ref.dtype)
        for k in range(PROJ_COLS // LANES):
            o_ref[lo // LANES + k] = res[:, k * LANES:(k + 1) * LANES]


def _norm_proj(x2d, gain, w, to_cast=()):
    t, d = x2d.shape
    n = w.shape[1]
    assert t % PROJ_TOKENS == 0 and n % PROJ_COLS == 0
    cast_in, cast_out, cast_shape, cast_args = _cast_specs(to_cast, t // PROJ_TOKENS)
    out, *cast = pl.pallas_call(
        functools.partial(_norm_proj_kernel, n_cast=len(cast_args)),
        grid=(t // PROJ_TOKENS,),
        in_specs=[pl.BlockSpec((PROJ_TOKENS, d), lambda i: (i, 0)),
                  pl.BlockSpec((1, d), lambda i: (0, 0)),
                  pl.BlockSpec((d, n), lambda i: (0, 0), pipeline_mode=pl.Buffered(1))] + cast_in,
        out_specs=[pl.BlockSpec((n // LANES, PROJ_TOKENS, LANES), lambda i: (0, i, 0))] + cast_out,
        out_shape=[jax.ShapeDtypeStruct((n // LANES, t, LANES), BF16)] + cast_shape,
        compiler_params=pltpu.CompilerParams(dimension_semantics=("arbitrary",),
                                             vmem_limit_bytes=VMEM_LIMIT_BYTES),
        name="norm_proj",
    )(x2d, gain.reshape(1, d), w, *cast_args)
    return out, cast


def _norm_proj_groups_kernel(x_ref, g_ref, *rest, chunks):
    n_proj = len(chunks)
    w_refs, out_refs, (xs_ref, xp_ref) = rest[:n_proj], rest[n_proj:-2], rest[-2:]
    tm = x_ref.shape[0]
    x = x_ref[...]
    xs = x * lax.rsqrt(jnp.mean(x * x, axis=-1, keepdims=True) + RMS_EPS)
    for cb in range(N_PAIRS):
        xs_ref[cb] = xs[:, cb * LANES:(cb + 1) * LANES]
    first_out = [sum(chunks[:p]) * N_GROUPS for p in range(n_proj)]
    for g, (_, dil) in enumerate(DILATED_GROUPS):
        n = tm // dil
        if dil > 1:
            for r in range(dil):
                for cb in range(N_PAIRS):
                    xp_ref[r * n:(r + 1) * n, cb * LANES:(cb + 1) * LANES] = (
                        xs_ref[cb, pl.ds(r, n, stride=dil), :])
        for p in range(n_proj):
            rows = xs if dil == 1 else xp_ref[...]
            xn = (rows * g_ref[p:p + 1, :]).astype(BF16)
            for c in range(chunks[p]):
                col = (g * chunks[p] + c) * D_MODEL
                res = jnp.dot(xn, w_refs[p][:, col:col + D_MODEL],
                              preferred_element_type=F32).astype(BF16)
                out = out_refs[first_out[p] + g * chunks[p] + c]
                for r in range(dil):
                    out[0, r] = res[r * n:(r + 1) * n]


def _norm_proj_groups(x2d, gains, ws, batch, seq):
    t, d = x2d.shape
    chunks = tuple(w.shape[1] // (N_GROUPS * d) for w in ws)
    tiles = seq // PROJ_TOKENS
    assert seq % PROJ_TOKENS == 0 and all(PROJ_TOKENS % (16 * dil) == 0 for _, dil in DILATED_GROUPS)
    out_shape, out_specs = [], []
    for n_chunks in chunks:
        for _, dil in DILATED_GROUPS:
            for _ in range(n_chunks):
                out_shape.append(jax.ShapeDtypeStruct((batch, dil, seq // dil, d), BF16))
                out_specs.append(pl.BlockSpec((1, dil, PROJ_TOKENS // dil, d), lambda b, j: (b, 0, j, 0)))
    outs = pl.pallas_call(
        functools.partial(_norm_proj_groups_kernel, chunks=chunks),
        grid=(batch, tiles),
        in_specs=([pl.BlockSpec((PROJ_TOKENS, d), lambda b, j: (b * tiles + j, 0)),
                   pl.BlockSpec((len(ws), d), lambda b, j: (0, 0))]
                  + [pl.BlockSpec(w.shape, lambda b, j: (0, 0), pipeline_mode=pl.Buffered(1)) for w in ws]),
        out_specs=out_specs,
        out_shape=out_shape,
        scratch_shapes=[pltpu.VMEM((d // LANES, PROJ_TOKENS, LANES), F32),
                        pltpu.VMEM((PROJ_TOKENS, d), F32)],
        compiler_params=pltpu.CompilerParams(dimension_semantics=("parallel", "parallel"),
                                             vmem_limit_bytes=VMEM_LIMIT_BYTES),
        name="norm_proj_groups",
    )(x2d, jnp.stack(gains), *ws)
    split, start = [], 0
    for n_chunks in chunks:
        split.append(outs[start:start + n_chunks * N_GROUPS])
        start += n_chunks * N_GROUPS
    return split


def _stick_kernel(q_ref, k_ref, v_ref, u_ref, o_ref, acc_ref, carry_ref):
    rows = 2 * STICK_QUERIES
    nq = q_ref.shape[2] // STICK_QUERIES

    def span(ref, pair, idx):
        return ref[pair, 0, idx * STICK_KEYS:(idx + 1) * STICK_KEYS, :]

    def weights(z, carry, diagonal):
        cost = jnp.maximum(z, 0.0) + jnp.log(1.0 + jnp.exp(-jnp.abs(z)))
        if diagonal:
            qrow = lax.broadcasted_iota(jnp.int32, z.shape, 0) % STICK_QUERIES
            kcol = lax.broadcasted_iota(jnp.int32, z.shape, 1)
            strict = kcol < qrow
            cost = jnp.where(strict, cost, 0.0)
        w = z - jnp.dot(cost.astype(BF16), u_ref[...], preferred_element_type=F32)
        a = jnp.concatenate([jnp.exp(w[:, :LANES] - carry), jnp.exp(w[:, LANES:] - carry)], axis=1)
        if diagonal:
            a = jnp.where(strict, a, 0.0)
        return a.astype(BF16), carry + jnp.sum(cost, axis=-1, keepdims=True)

    def walk(pair, qi, spans, acc, carry):
        qs = _split_heads(span(q_ref, pair, qi)) * QK_SCALE
        for j in spans:
            z = lax.dot_general(qs, span(k_ref, pair, j), (((1,), (1,)), ((), ())),
                                preferred_element_type=F32)
            a, carry = weights(z, carry, j == qi)
            acc = acc + jnp.dot(a, span(v_ref, pair, j), preferred_element_type=F32)
        return acc, carry

    alive = {}
    for qi in range(nq):
        for pair in range(STICK_PAIRS):
            zeros = jnp.zeros((rows, LANES), F32)
            acc, carry = walk(pair, qi, range(qi, max(qi - 2, -1), -1), zeros, zeros)
            acc_ref[pair, qi] = acc
            if qi >= 2:
                carry_ref[pair, qi] = carry
                alive[pair, qi] = jnp.min(carry, axis=0, keepdims=True)[0, 0] < STICK_UNDERFLOW

    for pair, qi in alive:
        @pl.when(alive[pair, qi])
        def _(pair=pair, qi=qi):
            acc, _ = walk(pair, qi, range(qi - 2, -1, -1), acc_ref[pair, qi], carry_ref[pair, qi])
            acc_ref[pair, qi] = acc

    for qi in range(nq):
        for pair in range(STICK_PAIRS):
            o_ref[0, qi * STICK_QUERIES:(qi + 1) * STICK_QUERIES, pair * LANES:(pair + 1) * LANES] = (
                _merge_heads(acc_ref[pair, qi]).astype(o_ref.dtype))


def _stick_attention(qkv, batch, seq):
    assert STICK_QUERIES == STICK_KEYS and seq % STICK_QUERIES == 0
    qkv = qkv.reshape(3 * N_PAIRS, batch, seq, LANES)
    assert STICK_KEYS == 2 * LANES
    j = jnp.arange(STICK_KEYS)[:, None]
    s = jnp.arange(STICK_KEYS)[None, :]
    uu = (j >= s).astype(BF16)
    assert N_PAIRS % STICK_PAIRS == 0
    steps = N_PAIRS // STICK_PAIRS
    return pl.pallas_call(
        _stick_kernel,
        grid=(batch, steps),
        in_specs=[pl.BlockSpec((STICK_PAIRS, 1, seq, LANES), lambda b, p: (p, b, 0, 0)),
                  pl.BlockSpec((STICK_PAIRS, 1, seq, LANES), lambda b, p: (steps + p, b, 0, 0)),
                  pl.BlockSpec((STICK_PAIRS, 1, seq, LANES), lambda b, p: (2 * steps + p, b, 0, 0)),
                  pl.BlockSpec((STICK_KEYS, STICK_KEYS), lambda b, p: (0, 0))],
        out_specs=pl.BlockSpec((1, seq, STICK_PAIRS * LANES), lambda b, p: (b, 0, p)),
        out_shape=jax.ShapeDtypeStruct((batch, seq, D_MODEL), BF16),
        scratch_shapes=[pltpu.VMEM((STICK_PAIRS, seq // STICK_QUERIES, 2 * STICK_QUERIES, LANES),
                                   F32)] * 2,
        compiler_params=pltpu.CompilerParams(dimension_semantics=("parallel", "parallel"),
                                             vmem_limit_bytes=VMEM_LIMIT_BYTES),
        name="stick_attention",
    )(qkv, qkv, qkv, uu)


def _relative_bucket(distance):
    max_exact = NUM_BUCKETS // 2
    n = jnp.maximum(distance, 0)
    large = max_exact + (jnp.log(jnp.maximum(n, 1).astype(F32) / max_exact)
                         / math.log(MAX_DISTANCE / max_exact)
                         * (NUM_BUCKETS - max_exact)).astype(jnp.int32)
    large = jnp.minimum(large, NUM_BUCKETS - 1)
    return jnp.where(n < max_exact, n, large)


def _band_bias(rel_bias, window, dilation):
    i = jnp.arange(Q_BLOCK)[:, None]
    m = jnp.arange(2 * Q_BLOCK)[None, :]
    rel = Q_BLOCK + i - m
    band = (rel >= 0) & (rel <= window // dilation)
    onehot = jax.nn.one_hot(_relative_bucket(rel * dilation), NUM_BUCKETS, dtype=F32)
    bias = jnp.einsum('qkb,bh->hqk', onehot, rel_bias.astype(F32), precision=lax.Precision.HIGHEST)
    bias = jnp.where(band[None], bias, NEG_INF)
    return bias.reshape(N_PAIRS, 2 * Q_BLOCK, 2 * Q_BLOCK)


def _band_attend(q_tile, kk, vv, bias):
    qs = _split_heads(q_tile) * QK_SCALE
    s = lax.dot_general(qs, kk, (((1,), (1,)), ((), ())), preferred_element_type=F32) + bias
    m = jnp.max(s, axis=-1, keepdims=True)
    p = jnp.exp(s - m)
    l = jnp.sum(p, axis=-1, keepdims=True)
    return jnp.dot(p.astype(BF16), vv, preferred_element_type=F32), m, l


def _dilated_kernel(*refs, with_prev):
    if with_prev:
        q_ref, kp_ref, kc_ref, vp_ref, vc_ref, bias_first_ref, bias_ref, o_ref, s_ref = refs
    else:
        q_ref, kc_ref, vc_ref, bias_ref, o_ref, s_ref = refs
    for sub in range(DILATED_SUBS):
        rows = slice(sub * Q_BLOCK, (sub + 1) * Q_BLOCK)
        stats_at = (lambda cols: (0, 0, rows, cols)) if with_prev else (lambda cols: (0, sub, slice(None), cols))
        s_ref[stats_at(slice(None))] = jnp.concatenate(
            [jnp.zeros((Q_BLOCK, LANES), F32), jnp.ones((Q_BLOCK, LANES), F32)], axis=1)
        for hp in range(N_PAIRS):
            sl = slice(hp * LANES, (hp + 1) * LANES)
            if not with_prev:
                q_tile, kk, vv = q_ref[0, sub, :, sl], kc_ref[0, sub, :, sl], vc_ref[0, sub, :, sl]
                bias = bias_ref[hp]
            elif sub == 0:
                q_tile = q_ref[0, 0, rows, sl]
                kk = jnp.concatenate([kp_ref[0, 0, :, sl], kc_ref[0, 0, rows, sl]], axis=0)
                vv = jnp.concatenate([vp_ref[0, 0, :, sl], vc_ref[0, 0, rows, sl]], axis=0)
                bias = bias_first_ref[0, hp]
            else:
                keys = slice((sub - 1) * Q_BLOCK, (sub + 1) * Q_BLOCK)
                q_tile, kk, vv = q_ref[0, 0, rows, sl], kc_ref[0, 0, keys, sl], vc_ref[0, 0, keys, sl]
                bias = bias_ref[0, hp]
            o, m, l = _band_attend(q_tile, kk, vv, bias)
            if with_prev:
                o_ref[0, 0, rows, sl] = _merge_heads(o).astype(o_ref.dtype)
            else:
                o_ref[0, sub, :, sl] = _merge_heads(o).astype(o_ref.dtype)
            for head, half in ((2 * hp, slice(0, Q_BLOCK)), (2 * hp + 1, slice(Q_BLOCK, None))):
                s_ref[stats_at(slice(head, head + 1))] = m[half]
                s_ref[stats_at(slice(LANES + head, LANES + head + 1))] = l[half]


def _dilated_group(q, k, v, rel_bias, group):
    window, dilation = DILATED_GROUPS[group]
    batch, _, strided, d = q.shape
    assert strided % Q_BLOCK == 0
    nb = strided // Q_BLOCK
    with_prev = nb > 1
    bias = _band_bias(rel_bias, window, dilation)
    if with_prev:
        assert nb % DILATED_SUBS == 0
        key = jnp.arange(2 * Q_BLOCK)[None, None, :]
        bias = jnp.stack([jnp.where(key < Q_BLOCK, NEG_INF, bias), bias])
        grid = (batch, dilation, nb // DILATED_SUBS)
        two = pl.BlockSpec((1, 1, DILATED_SUBS * Q_BLOCK, d), lambda b, r, i: (b, r, i, 0))
        prev = pl.BlockSpec((1, 1, Q_BLOCK, d),
                            lambda b, r, i: (b, r, jnp.maximum(DILATED_SUBS * i - 1, 0), 0))
        table = (1,) + bias.shape[1:]
        in_specs = [two, prev, two, prev, two,
                    pl.BlockSpec(table, lambda b, r, i: (jnp.minimum(i, 1), 0, 0, 0)),
                    pl.BlockSpec(table, lambda b, r, i: (1, 0, 0, 0))]
        args = [q, k, k, v, v, bias, bias]
        out_specs = [two, pl.BlockSpec((1, 1, DILATED_SUBS * Q_BLOCK, 2 * LANES),
                                       lambda b, r, i: (b, r, i, 0))]
    else:
        assert dilation % DILATED_SUBS == 0
        bias = bias[:, :, Q_BLOCK:]
        grid = (batch, dilation // DILATED_SUBS, 1)
        two = pl.BlockSpec((1, DILATED_SUBS, Q_BLOCK, d), lambda b, r, i: (b, r, 0, 0))
        in_specs = [two, two, two, pl.BlockSpec(bias.shape, lambda b, r, i: (0, 0, 0))]
        args = [q, k, v, bias]
        out_specs = [two, pl.BlockSpec((1, DILATED_SUBS, Q_BLOCK, 2 * LANES), lambda b, r, i: (b, r, 0, 0))]
    return pl.pallas_call(
        functools.partial(_dilated_kernel, with_prev=with_prev),
        grid=grid,
        in_specs=in_specs,
        out_specs=out_specs,
        out_shape=[jax.ShapeDtypeStruct((batch, dilation, strided, d), BF16),
                   jax.ShapeDtypeStruct((batch, dilation, strided, 2 * LANES), F32)],
        compiler_params=pltpu.CompilerParams(dimension_semantics=("parallel", "parallel", "parallel"),
                                             vmem_limit_bytes=VMEM_LIMIT_BYTES),
        name=f"dilated_group{group}",
    )(*args)


def _merge_groups(o_refs, s_refs, e_ref, o_tok_refs, s_tok_refs):
    outs, tops, dens = [], [], []
    for g, (_, dil) in enumerate(DILATED_GROUPS):
        if dil == 1:
            outs.append(o_refs[g][0, 0].astype(F32))
            tops.append(s_refs[g][0, 0, :, :LANES])
            dens.append(s_refs[g][0, 0, :, LANES:])
        else:
            n = o_refs[g].shape[2]
            for r in range(dil):
                rows = pl.ds(r, n, stride=dil)
                part = o_refs[g][0, r].astype(F32)
                for cb in range(N_PAIRS):
                    o_tok_refs[g][cb, rows, :] = part[:, cb * LANES:(cb + 1) * LANES]
                s_tok_refs[g][0, rows, :] = s_refs[g][0, r, :, :LANES]
                s_tok_refs[g][1, rows, :] = s_refs[g][0, r, :, LANES:]
            outs.append(jnp.concatenate([o_tok_refs[g][cb] for cb in range(N_PAIRS)], axis=1))
            tops.append(s_tok_refs[g][0])
            dens.append(s_tok_refs[g][1])
    top = functools.reduce(jnp.maximum, tops)
    es = [jnp.exp(m - top) for m in tops]
    inv = 1.0 / functools.reduce(jnp.add, [e * l for e, l in zip(es, dens)])
    merged = None
    for out, e in zip(outs, es):
        wt = e * inv
        hi = wt.astype(BF16)
        lo = (wt - hi.astype(F32)).astype(BF16)
        spread = jnp.dot(jnp.concatenate([hi, lo], axis=1), e_ref[...], preferred_element_type=F32)
        merged = spread * out if merged is None else merged + spread * out
    return merged.astype(BF16)


def _oproj_ffn_kernel(*refs, merge, n_cast):
    if merge:
        o_refs, s_refs, e_ref = refs[0:N_GROUPS], refs[N_GROUPS:2 * N_GROUPS], refs[2 * N_GROUPS]
        refs = refs[2 * N_GROUPS + 1:]
        strided_groups = [g for g, (_, dil) in enumerate(DILATED_GROUPS) if dil > 1]
        n_tok = len(strided_groups)
        o_tok_refs = dict(zip(strided_groups, refs[-2 * n_tok:-n_tok]))
        s_tok_refs = dict(zip(strided_groups, refs[-n_tok:]))
        refs = refs[:-2 * n_tok]
    else:
        o_ref, refs = refs[0], refs[1:]
    x_ref, wo_ref, g_ref, wgu_ref, wd_ref = refs[:5]
    cast_in, out_ref, cast_out = refs[5:5 + n_cast], refs[5 + n_cast], refs[6 + n_cast:6 + 2 * n_cast]
    xmid_ref, xn_ref, hid_ref = refs[6 + 2 * n_cast:]
    for src, dst in zip(cast_in, cast_out):
        dst[...] = src[...].astype(BF16)
    if merge:
        attn = _merge_groups(o_refs, s_refs, e_ref, o_tok_refs, s_tok_refs)
    else:
        attn = o_ref[...]
    h = jnp.dot(attn, wo_ref[...], preferred_element_type=F32)
    xm = x_ref[...] + _rms_scale(h, g_ref[1:2, :])
    xmid_ref[...] = xm
    xn_ref[...] = _rms_scale(xm, g_ref[2:3, :]).astype(BF16)
    for c in range(D_FF // FFN_COLS):
        lo = c * FFN_COLS
        xn = xn_ref[...]
        gate = jnp.dot(xn, wgu_ref[:, lo:lo + FFN_COLS], preferred_element_type=F32)
        up = jnp.dot(xn, wgu_ref[:, D_FF + lo:D_FF + lo + FFN_COLS], preferred_element_type=F32)
        hid_ref[:, lo:lo + FFN_COLS] = (gate * jax.nn.sigmoid(gate) * up).astype(BF16)
    down = jnp.dot(hid_ref[...], wd_ref[...], preferred_element_type=F32)
    out_ref[...] = xmid_ref[...] + _rms_scale(down, g_ref[3:4, :])


def _cast_specs(to_cast, n_steps):
    in_specs, out_specs, out_shape, args = [], [], [], []
    for w, layer in to_cast:
        rows, cols = w.shape[-2:]
        n_blocks = n_steps
        while n_blocks > 1 and (rows % n_blocks or (rows // n_blocks) % 16):
            n_blocks //= 2
        assert n_steps % n_blocks == 0
        steps_per_block = n_steps // n_blocks
        block = (rows // n_blocks, cols)
        if layer is None:
            in_specs.append(pl.BlockSpec(block, lambda i, spb=steps_per_block: (i // spb, 0)))
        else:
            in_specs.append(pl.BlockSpec((None,) + block,
                                         lambda i, spb=steps_per_block, li=layer: (li, i // spb, 0)))
        out_specs.append(pl.BlockSpec(block, lambda i, spb=steps_per_block: (i // spb, 0)))
        out_shape.append(jax.ShapeDtypeStruct((rows, cols), BF16))
        args.append(w)
    return in_specs, out_specs, out_shape, args


def _oproj_ffn(attn, x2d, w_o, gains, w_gate_up, w_down, seq, to_cast=()):
    t, d = x2d.shape
    tm = FFN_TOKENS
    assert t % tm == 0 and seq % tm == 0 and D_FF % FFN_COLS == 0
    merge = not isinstance(attn, jax.Array)
    cast_in, cast_out, cast_shape, cast_args = _cast_specs(to_cast, t // tm)
    scratch = [pltpu.VMEM((tm, d), F32), pltpu.VMEM((tm, d), BF16), pltpu.VMEM((tm, D_FF), BF16)]
    resident = lambda shape: pl.BlockSpec(shape, lambda i: (0, 0), pipeline_mode=pl.Buffered(1))
    tok = pl.BlockSpec((tm, d), lambda i: (i, 0))
    if merge:
        tiles = seq // tm
        assert all(tm % (16 * dil) == 0 for _, dil in DILATED_GROUPS)
        blk = lambda dil, cols: pl.BlockSpec((1, dil, tm // dil, cols),
                                             lambda i: (i // tiles, 0, i % tiles, 0))
        head = jnp.arange(2 * LANES)[:, None] % LANES
        col = jnp.arange(d)[None, :] // HEAD_DIM
        spread = (head == col).astype(BF16)
        attn_specs = ([blk(dil, d) for _, dil in DILATED_GROUPS]
                      + [blk(dil, 2 * LANES) for _, dil in DILATED_GROUPS]
                      + [resident(spread.shape)])
        attn_args = [o for o, _ in attn] + [s for _, s in attn] + [spread]
        n_strided = sum(dil > 1 for _, dil in DILATED_GROUPS)
        scratch += ([pltpu.VMEM((d // LANES, tm, LANES), F32)] * n_strided
                    + [pltpu.VMEM((2, tm, LANES), F32)] * n_strided)
    else:
        attn_specs = [tok]
        attn_args = [attn]
    out, *cast = pl.pallas_call(
        functools.partial(_oproj_ffn_kernel, merge=merge, n_cast=len(cast_args)),
        grid=(t // tm,),
        in_specs=attn_specs + [
            tok,
            resident(w_o.shape),
            resident(gains.shape),
            resident(w_gate_up.shape),
            resident(w_down.shape)] + cast_in,
        out_specs=[tok] + cast_out,
        out_shape=[jax.ShapeDtypeStruct((t, d), F32)] + cast_shape,
        scratch_shapes=scratch,
        compiler_params=pltpu.CompilerParams(dimension_semantics=("arbitrary",),
                                             vmem_limit_bytes=VMEM_LIMIT_BYTES),
        name="oproj_ffn_merge" if merge else "oproj_ffn",
    )(*attn_args, x2d, w_o, gains, w_gate_up, w_down, *cast_args)
    return out, cast


def kernel(x, norm_gains, w_qkv_a, w_o_a, g_kv, w_kv_b, w_q_b, w_o_b, rel_bias, w_gate_up, w_down):
    batch, seq, d = x.shape
    depth = norm_gains.shape[0]
    n_a = w_qkv_a.shape[0]
    t = batch * seq
    x2d = x.reshape(t, d)

    def layer_weights(layer):
        if layer < n_a:
            own = {"qkv": (w_qkv_a, layer), "o": (w_o_a, layer)}
        else:
            own = {"q": (w_q_b, layer - n_a), "o": (w_o_b, layer - n_a)}
            if layer == n_a:
                own["kv"] = (w_kv_b, None)
        return {**own, "gate_up": (w_gate_up, layer), "down": (w_down, layer)}

    pending = layer_weights(0)
    eager = ["qkv"] if n_a > 0 else list(pending)
    bf16 = {name: (w if idx is None else w[idx]).astype(BF16)
            for name, (w, idx) in ((name, pending.pop(name)) for name in eager)}
    shared_kv = None
    for layer in range(depth):
        g = norm_gains[layer]
        if layer < n_a:
            qkv, cast = _norm_proj(x2d, g[0], bf16["qkv"], to_cast=list(pending.values()))
            bf16.update(zip(pending, cast))
            pending = {}
            attn = _stick_attention(qkv, batch, seq).reshape(t, d)
        else:
            if shared_kv is None:
                q, shared_kv = _norm_proj_groups(x2d, [g[0], g_kv], [bf16["q"], bf16["kv"]], batch, seq)
            else:
                (q,) = _norm_proj_groups(x2d, [g[0]], [bf16["q"]], batch, seq)
            attn = [_dilated_group(q[grp], shared_kv[2 * grp], shared_kv[2 * grp + 1], rel_bias, grp)
                    for grp in range(N_GROUPS)]
        following = layer_weights(layer + 1) if layer + 1 < depth else {}
        x2d, cast = _oproj_ffn(attn, x2d, bf16["o"], g, bf16["gate_up"], bf16["down"], seq,
                               to_cast=list(following.values()))
        bf16 = dict(zip(following, cast))
    return x2d.reshape(batch, seq, d)
```

```python
import functools
import math

import jax
import jax.numpy as jnp
from jax import lax
from jax.experimental import pallas as pl
from jax.experimental.pallas import tpu as pltpu

D_MODEL = 1024
N_HEADS = 16
HEAD_DIM = D_MODEL // N_HEADS
LANES = 128
N_PAIRS = D_MODEL // LANES
Q_BLOCK = 128
DILATED_GROUPS = ((128, 1), (512, 4), (2048, 16))
N_GROUPS = len(DILATED_GROUPS)
NUM_BUCKETS = 32
MAX_DISTANCE = 2048
D_FF = -(-8 * D_MODEL // (3 * 256)) * 256
RMS_EPS = 1e-6
NEG_INF = -1e30
QK_SCALE = HEAD_DIM ** -0.5

VMEM_LIMIT_BYTES = 56 * 1024 * 1024
PROJ_TOKENS = 512
PROJ_COLS = 512
FFN_TOKENS = 512
FFN_COLS = 256
DILATED_SUBS = 4
STICK_PAIRS = 2
STICK_QUERIES = 256
STICK_KEYS = 256
STICK_UNDERFLOW = 128.0

F32 = jnp.float32
BF16 = jnp.bfloat16


def _rms_scale(x, gain_row):
    ms = jnp.mean(x * x, axis=-1, keepdims=True)
    return x * lax.rsqrt(ms + RMS_EPS) * gain_row


def _split_heads(tile):
    lane = lax.broadcasted_iota(jnp.int32, tile.shape, 1)
    zero = jnp.zeros_like(tile)
    return jnp.concatenate([jnp.where(lane < HEAD_DIM, tile, zero),
                            jnp.where(lane >= HEAD_DIM, tile, zero)], axis=0)


def _merge_heads(stacked):
    rows = stacked.shape[0] // 2
    lane = lax.broadcasted_iota(jnp.int32, (rows, LANES), 1)
    return jnp.where(lane < HEAD_DIM, stacked[:rows], stacked[rows:])


def _norm_proj_kernel(x_ref, g_ref, w_ref, *rest, n_cast):
    cast_in, o_ref, cast_out = rest[:n_cast], rest[n_cast], rest[n_cast + 1:]
    for src, dst in zip(cast_in, cast_out):
        dst[...] = src[...].astype(BF16)
    xn = _rms_scale(x_ref[...], g_ref[...]).astype(BF16)
    n_out = w_ref.shape[1]
    for c in range(n_out // PROJ_COLS):
        lo = c * PROJ_COLS
        res = jnp.dot(xn, w_ref[:, lo:lo + PROJ_COLS], preferred_element_type=F32).astype(o_ref.dtype)
        for k in range(PROJ_COLS // LANES):
            o_ref[lo // LANES + k] = res[:, k * LANES:(k + 1) * LANES]


def _norm_proj(x2d, gain, w, to_cast=()):
    t, d = x2d.shape
    n = w.shape[1]
    assert t % PROJ_TOKENS == 0 and n % PROJ_COLS == 0
    cast_in, cast_out, cast_shape, cast_args = _cast_specs(to_cast, t // PROJ_TOKENS)
    out, *cast = pl.pallas_call(
        functools.partial(_norm_proj_kernel, n_cast=len(cast_args)),
        grid=(t // PROJ_TOKENS,),
        in_specs=[pl.BlockSpec((PROJ_TOKENS, d), lambda i: (i, 0)),
                  pl.BlockSpec((1, d), lambda i: (0, 0)),
                  pl.BlockSpec((d, n), lambda i: (0, 0), pipeline_mode=pl.Buffered(1))] + cast_in,
        out_specs=[pl.BlockSpec((n // LANES, PROJ_TOKENS, LANES), lambda i: (0, i, 0))] + cast_out,
        out_shape=[jax.ShapeDtypeStruct((n // LANES, t, LANES), BF16)] + cast_shape,
        compiler_params=pltpu.CompilerParams(dimension_semantics=("arbitrary",),
                                             vmem_limit_bytes=VMEM_LIMIT_BYTES),
        name="norm_proj",
    )(x2d, gain.reshape(1, d), w, *cast_args)
    return out, cast


def _norm_proj_groups_kernel(x_ref, g_ref, *rest, chunks):
    n_proj = len(chunks)
    w_refs, out_refs, (xs_ref, xp_ref) = rest[:n_proj], rest[n_proj:-2], rest[-2:]
    tm = x_ref.shape[0]
    x = x_ref[...]
    xs = x * lax.rsqrt(jnp.mean(x * x, axis=-1, keepdims=True) + RMS_EPS)
    for cb in range(N_PAIRS):
        xs_ref[cb] = xs[:, cb * LANES:(cb + 1) * LANES]
    first_out = [sum(chunks[:p]) * N_GROUPS for p in range(n_proj)]
    for g, (_, dil) in enumerate(DILATED_GROUPS):
        n = tm // dil
        if dil > 1:
            for r in range(dil):
                for cb in range(N_PAIRS):
                    xp_ref[r * n:(r + 1) * n, cb * LANES:(cb + 1) * LANES] = (
                        xs_ref[cb, pl.ds(r, n, stride=dil), :])
        for p in range(n_proj):
            rows = xs if dil == 1 else xp_ref[...]
            xn = (rows * g_ref[p:p + 1, :]).astype(BF16)
            for c in range(chunks[p]):
                col = (g * chunks[p] + c) * D_MODEL
                res = jnp.dot(xn, w_refs[p][:, col:col + D_MODEL],
                              preferred_element_type=F32).astype(BF16)
                out = out_refs[first_out[p] + g * chunks[p] + c]
                for r in range(dil):
                    out[0, r] = res[r * n:(r + 1) * n]


def _norm_proj_groups(x2d, gains, ws, batch, seq):
    t, d = x2d.shape
    chunks = tuple(w.shape[1] // (N_GROUPS * d) for w in ws)
    tiles = seq // PROJ_TOKENS
    assert seq % PROJ_TOKENS == 0 and all(PROJ_TOKENS % (16 * dil) == 0 for _, dil in DILATED_GROUPS)
    out_shape, out_specs = [], []
    for n_chunks in chunks:
        for _, dil in DILATED_GROUPS:
            for _ in range(n_chunks):
                out_shape.append(jax.ShapeDtypeStruct((batch, dil, seq // dil, d), BF16))
                out_specs.append(pl.BlockSpec((1, dil, PROJ_TOKENS // dil, d), lambda b, j: (b, 0, j, 0)))
    outs = pl.pallas_call(
        functools.partial(_norm_proj_groups_kernel, chunks=chunks),
        grid=(batch, tiles),
        in_specs=([pl.BlockSpec((PROJ_TOKENS, d), lambda b, j: (b * tiles + j, 0)),
                   pl.BlockSpec((len(ws), d), lambda b, j: (0, 0))]
                  + [pl.BlockSpec(w.shape, lambda b, j: (0, 0), pipeline_mode=pl.Buffered(1)) for w in ws]),
        out_specs=out_specs,
        out_shape=out_shape,
        scratch_shapes=[pltpu.VMEM((d // LANES, PROJ_TOKENS, LANES), F32),
                        pltpu.VMEM((PROJ_TOKENS, d), F32)],
        compiler_params=pltpu.CompilerParams(dimension_semantics=("parallel", "parallel"),
                                             vmem_limit_bytes=VMEM_LIMIT_BYTES),
        name="norm_proj_groups",
    )(x2d, jnp.stack(gains), *ws)
    split, start = [], 0
    for n_chunks in chunks:
        split.append(outs[start:start + n_chunks * N_GROUPS])
        start += n_chunks * N_GROUPS
    return split


def _stick_kernel(q_ref, k_ref, v_ref, u_ref, o_ref, acc_ref, carry_ref):
    rows = 2 * STICK_QUERIES
    nq = q_ref.shape[2] // STICK_QUERIES

    def span(ref, pair, idx):
        return ref[pair, 0, idx * STICK_KEYS:(idx + 1) * STICK_KEYS, :]

    def weights(z, carry, diagonal):
        cost = jnp.maximum(z, 0.0) + jnp.log(1.0 + jnp.exp(-jnp.abs(z)))
        if diagonal:
            qrow = lax.broadcasted_iota(jnp.int32, z.shape, 0) % STICK_QUERIES
            kcol = lax.broadcasted_iota(jnp.int32, z.shape, 1)
            strict = kcol < qrow
            cost = jnp.where(strict, cost, 0.0)
        w = z - jnp.dot(cost.astype(BF16), u_ref[...], preferred_element_type=F32)
        a = jnp.concatenate([jnp.exp(w[:, :LANES] - carry), jnp.exp(w[:, LANES:] - carry)], axis=1)
        if diagonal:
            a = jnp.where(strict, a, 0.0)
        return a.astype(BF16), carry + jnp.sum(cost, axis=-1, keepdims=True)

    def walk(pair, qi, spans, acc, carry):
        qs = _split_heads(span(q_ref, pair, qi)) * QK_SCALE
        for j in spans:
            z = lax.dot_general(qs, span(k_ref, pair, j), (((1,), (1,)), ((), ())),
                                preferred_element_type=F32)
            a, carry = weights(z, carry, j == qi)
            acc = acc + jnp.dot(a, span(v_ref, pair, j), preferred_element_type=F32)
        return acc, carry

    alive = {}
    for qi in range(nq):
        for pair in range(STICK_PAIRS):
            zeros = jnp.zeros((rows, LANES), F32)
            acc, carry = walk(pair, qi, range(qi, max(qi - 2, -1), -1), zeros, zeros)
            acc_ref[pair, qi] = acc
            if qi >= 2:
                carry_ref[pair, qi] = carry
                alive[pair, qi] = jnp.min(carry, axis=0, keepdims=True)[0, 0] < STICK_UNDERFLOW

    for pair, qi in alive:
        @pl.when(alive[pair, qi])
        def _(pair=pair, qi=qi):
            acc, _ = walk(pair, qi, range(qi - 2, -1, -1), acc_ref[pair, qi], carry_ref[pair, qi])
            acc_ref[pair, qi] = acc

    for qi in range(nq):
        for pair in range(STICK_PAIRS):
            o_ref[0, qi * STICK_QUERIES:(qi + 1) * STICK_QUERIES, pair * LANES:(pair + 1) * LANES] = (
                _merge_heads(acc_ref[pair, qi]).astype(o_ref.dtype))


def _stick_attention(qkv, batch, seq):
    assert STICK_QUERIES == STICK_KEYS and seq % STICK_QUERIES == 0
    qkv = qkv.reshape(3 * N_PAIRS, batch, seq, LANES)
    assert STICK_KEYS == 2 * LANES
    j = jnp.arange(STICK_KEYS)[:, None]
    s = jnp.arange(STICK_KEYS)[None, :]
    uu = (j >= s).astype(BF16)
    assert N_PAIRS % STICK_PAIRS == 0
    steps = N_PAIRS // STICK_PAIRS
    return pl.pallas_call(
        _stick_kernel,
        grid=(batch, steps),
        in_specs=[pl.BlockSpec((STICK_PAIRS, 1, seq, LANES), lambda b, p: (p, b, 0, 0)),
                  pl.BlockSpec((STICK_PAIRS, 1, seq, LANES), lambda b, p: (steps + p, b, 0, 0)),
                  pl.BlockSpec((STICK_PAIRS, 1, seq, LANES), lambda b, p: (2 * steps + p, b, 0, 0)),
                  pl.BlockSpec((STICK_KEYS, STICK_KEYS), lambda b, p: (0, 0))],
        out_specs=pl.BlockSpec((1, seq, STICK_PAIRS * LANES), lambda b, p: (b, 0, p)),
        out_shape=jax.ShapeDtypeStruct((batch, seq, D_MODEL), BF16),
        scratch_shapes=[pltpu.VMEM((STICK_PAIRS, seq // STICK_QUERIES, 2 * STICK_QUERIES, LANES),
                                   F32)] * 2,
        compiler_params=pltpu.CompilerParams(dimension_semantics=("parallel", "parallel"),
                                             vmem_limit_bytes=VMEM_LIMIT_BYTES),
        name="stick_attention",
    )(qkv, qkv, qkv, uu)


def _relative_bucket(distance):
    max_exact = NUM_BUCKETS // 2
    n = jnp.maximum(distance, 0)
    large = max_exact + (jnp.log(jnp.maximum(n, 1).astype(F32) / max_exact)
                         / math.log(MAX_DISTANCE / max_exact)
                         * (NUM_BUCKETS - max_exact)).astype(jnp.int32)
    large = jnp.minimum(large, NUM_BUCKETS - 1)
    return jnp.where(n < max_exact, n, large)


def _band_bias(rel_bias, window, dilation):
    i = jnp.arange(Q_BLOCK)[:, None]
    m = jnp.arange(2 * Q_BLOCK)[None, :]
    rel = Q_BLOCK + i - m
    band = (rel >= 0) & (rel <= window // dilation)
    onehot = jax.nn.one_hot(_relative_bucket(rel * dilation), NUM_BUCKETS, dtype=F32)
    bias = jnp.einsum('qkb,bh->hqk', onehot, rel_bias.astype(F32), precision=lax.Precision.HIGHEST)
    bias = jnp.where(band[None], bias, NEG_INF)
    return bias.reshape(N_PAIRS, 2 * Q_BLOCK, 2 * Q_BLOCK)


def _band_attend(q_tile, kk, vv, bias):
    qs = _split_heads(q_tile) * QK_SCALE
    s = lax.dot_general(qs, kk, (((1,), (1,)), ((), ())), preferred_element_type=F32) + bias
    m = jnp.max(s, axis=-1, keepdims=True)
    p = jnp.exp(s - m)
    l = jnp.sum(p, axis=-1, keepdims=True)
    return jnp.dot(p.astype(BF16), vv, preferred_element_type=F32), m, l


def _dilated_kernel(*refs, with_prev):
    if with_prev:
        q_ref, kp_ref, kc_ref, vp_ref, vc_ref, bias_first_ref, bias_ref, o_ref, s_ref = refs
    else:
        q_ref, kc_ref, vc_ref, bias_ref, o_ref, s_ref = refs
    for sub in range(DILATED_SUBS):
        rows = slice(sub * Q_BLOCK, (sub + 1) * Q_BLOCK)
        stats_at = (lambda cols: (0, 0, rows, cols)) if with_prev else (lambda cols: (0, sub, slice(None), cols))
        s_ref[stats_at(slice(None))] = jnp.concatenate(
            [jnp.zeros((Q_BLOCK, LANES), F32), jnp.ones((Q_BLOCK, LANES), F32)], axis=1)
        for hp in range(N_PAIRS):
            sl = slice(hp * LANES, (hp + 1) * LANES)
            if not with_prev:
                q_tile, kk, vv = q_ref[0, sub, :, sl], kc_ref[0, sub, :, sl], vc_ref[0, sub, :, sl]
                bias = bias_ref[hp]
            elif sub == 0:
                q_tile = q_ref[0, 0, rows, sl]
                kk = jnp.concatenate([kp_ref[0, 0, :, sl], kc_ref[0, 0, rows, sl]], axis=0)
                vv = jnp.concatenate([vp_ref[0, 0, :, sl], vc_ref[0, 0, rows, sl]], axis=0)
                bias = bias_first_ref[0, hp]
            else:
                keys = slice((sub - 1) * Q_BLOCK, (sub + 1) * Q_BLOCK)
                q_tile, kk, vv = q_ref[0, 0, rows, sl], kc_ref[0, 0, keys, sl], vc_ref[0, 0, keys, sl]
                bias = bias_ref[0, hp]
            o, m, l = _band_attend(q_tile, kk, vv, bias)
            if with_prev:
                o_ref[0, 0, rows, sl] = _merge_heads(o).astype(o_ref.dtype)
            else:
                o_ref[0, sub, :, sl] = _merge_heads(o).astype(o_ref.dtype)
            for head, half in ((2 * hp, slice(0, Q_BLOCK)), (2 * hp + 1, slice(Q_BLOCK, None))):
                s_ref[stats_at(slice(head, head + 1))] = m[half]
                s_ref[stats_at(slice(LANES + head, LANES + head + 1))] = l[half]


def _dilated_group(q, k, v, rel_bias, group):
    window, dilation = DILATED_GROUPS[group]
    batch, _, strided, d = q.shape
    assert strided % Q_BLOCK == 0
    nb = strided // Q_BLOCK
    with_prev = nb > 1
    bias = _band_bias(rel_bias, window, dilation)
    if with_prev:
        assert nb % DILATED_SUBS == 0
        key = jnp.arange(2 * Q_BLOCK)[None, None, :]
        bias = jnp.stack([jnp.where(key < Q_BLOCK, NEG_INF, bias), bias])
        grid = (batch, dilation, nb // DILATED_SUBS)
        two = pl.BlockSpec((1, 1, DILATED_SUBS * Q_BLOCK, d), lambda b, r, i: (b, r, i, 0))
        prev = pl.BlockSpec((1, 1, Q_BLOCK, d),
                            lambda b, r, i: (b, r, jnp.maximum(DILATED_SUBS * i - 1, 0), 0))
        table = (1,) + bias.shape[1:]
        in_specs = [two, prev, two, prev, two,
                    pl.BlockSpec(table, lambda b, r, i: (jnp.minimum(i, 1), 0, 0, 0)),
                    pl.BlockSpec(table, lambda b, r, i: (1, 0, 0, 0))]
        args = [q, k, k, v, v, bias, bias]
        out_specs = [two, pl.BlockSpec((1, 1, DILATED_SUBS * Q_BLOCK, 2 * LANES),
                                       lambda b, r, i: (b, r, i, 0))]
    else:
        assert dilation % DILATED_SUBS == 0
        bias = bias[:, :, Q_BLOCK:]
        grid = (batch, dilation // DILATED_SUBS, 1)
        two = pl.BlockSpec((1, DILATED_SUBS, Q_BLOCK, d), lambda b, r, i: (b, r, 0, 0))
        in_specs = [two, two, two, pl.BlockSpec(bias.shape, lambda b, r, i: (0, 0, 0))]
        args = [q, k, v, bias]
        out_specs = [two, pl.BlockSpec((1, DILATED_SUBS, Q_BLOCK, 2 * LANES), lambda b, r, i: (b, r, 0, 0))]
    return pl.pallas_call(
        functools.partial(_dilated_kernel, with_prev=with_prev),
        grid=grid,
        in_specs=in_specs,
        out_specs=out_specs,
        out_shape=[jax.ShapeDtypeStruct((batch, dilation, strided, d), BF16),
                   jax.ShapeDtypeStruct((batch, dilation, strided, 2 * LANES), F32)],
        compiler_params=pltpu.CompilerParams(dimension_semantics=("parallel", "parallel", "parallel"),
                                             vmem_limit_bytes=VMEM_LIMIT_BYTES),
        name=f"dilated_group{group}",
    )(*args)


def _merge_groups(o_refs, s_refs, e_ref, o_tok_refs, s_tok_refs):
    outs, tops, dens = [], [], []
    for g, (_, dil) in enumerate(DILATED_GROUPS):
        if dil == 1:
            outs.append(o_refs[g][0, 0].astype(F32))
            tops.append(s_refs[g][0, 0, :, :LANES])
            dens.append(s_refs[g][0, 0, :, LANES:])
        else:
            n = o_refs[g].shape[2]
            for r in range(dil):
                rows = pl.ds(r, n, stride=dil)
                part = o_refs[g][0, r].astype(F32)
                for cb in range(N_PAIRS):
                    o_tok_refs[g][cb, rows, :] = part[:, cb * LANES:(cb + 1) * LANES]
                s_tok_refs[g][0, rows, :] = s_refs[g][0, r, :, :LANES]
                s_tok_refs[g][1, rows, :] = s_refs[g][0, r, :, LANES:]
            outs.append(jnp.concatenate([o_tok_refs[g][cb] for cb in range(N_PAIRS)], axis=1))
            tops.append(s_tok_refs[g][0])
            dens.append(s_tok_refs[g][1])
    top = functools.reduce(jnp.maximum, tops)
    es = [jnp.exp(m - top) for m in tops]
    inv = 1.0 / functools.reduce(jnp.add, [e * l for e, l in zip(es, dens)])
    merged = None
    for out, e in zip(outs, es):
        wt = e * inv
        hi = wt.astype(BF16)
        lo = (wt - hi.astype(F32)).astype(BF16)
        spread = jnp.dot(jnp.concatenate([hi, lo], axis=1), e_ref[...], preferred_element_type=F32)
        merged = spread * out if merged is None else merged + spread * out
    return merged.astype(BF16)


def _oproj_ffn_kernel(*refs, merge, n_cast):
    if merge:
        o_refs, s_refs, e_ref = refs[0:N_GROUPS], refs[N_GROUPS:2 * N_GROUPS], refs[2 * N_GROUPS]
        refs = refs[2 * N_GROUPS + 1:]
        strided_groups = [g for g, (_, dil) in enumerate(DILATED_GROUPS) if dil > 1]
        n_tok = len(strided_groups)
        o_tok_refs = dict(zip(strided_groups, refs[-2 * n_tok:-n_tok]))
        s_tok_refs = dict(zip(strided_groups, refs[-n_tok:]))
        refs = refs[:-2 * n_tok]
    else:
        o_ref, refs = refs[0], refs[1:]
    x_ref, wo_ref, g_ref, wgu_ref, wd_ref = refs[:5]
    cast_in, out_ref, cast_out = refs[5:5 + n_cast], refs[5 + n_cast], refs[6 + n_cast:6 + 2 * n_cast]
    xmid_ref, xn_ref, hid_ref = refs[6 + 2 * n_cast:]
    for src, dst in zip(cast_in, cast_out):
        dst[...] = src[...].astype(BF16)
    if merge:
        attn = _merge_groups(o_refs, s_refs, e_ref, o_tok_refs, s_tok_refs)
    else:
        attn = o_ref[...]
    tm = x_ref.shape[0]
    halves = [slice(p * (tm // 2), (p + 1) * (tm // 2)) for p in range(2)]

    def swiglu_chunk(c, rows):
        lo = c * FFN_COLS
        xn = xn_ref[rows, :]
        gate = jnp.dot(xn, wgu_ref[:, lo:lo + FFN_COLS], preferred_element_type=F32)
        up = jnp.dot(xn, wgu_ref[:, D_FF + lo:D_FF + lo + FFN_COLS], preferred_element_type=F32)
        hid_ref[rows, lo:lo + FFN_COLS] = (gate * jax.nn.sigmoid(gate) * up).astype(BF16)

    hs = [jnp.dot(attn[rows], wo_ref[...], preferred_element_type=F32) for rows in halves]
    for rows, h in zip(halves, hs):
        xm = x_ref[rows, :] + _rms_scale(h, g_ref[1:2, :])
        xmid_ref[rows, :] = xm
        xn_ref[rows, :] = _rms_scale(xm, g_ref[2:3, :]).astype(BF16)
        swiglu_chunk(0, rows)
    for c in range(1, D_FF // FFN_COLS):
        swiglu_chunk(c, slice(None))
    for rows in halves:
        down = jnp.dot(hid_ref[rows, :], wd_ref[...], preferred_element_type=F32)
        out_ref[rows, :] = xmid_ref[rows, :] + _rms_scale(down, g_ref[3:4, :])


def _cast_specs(to_cast, n_steps):
    in_specs, out_specs, out_shape, args = [], [], [], []
    for w, layer in to_cast:
        rows, cols = w.shape[-2:]
        n_blocks = n_steps
        while n_blocks > 1 and (rows % n_blocks or (rows // n_blocks) % 16):
            n_blocks //= 2
        assert n_steps % n_blocks == 0
        steps_per_block = n_steps // n_blocks
        block = (rows // n_blocks, cols)
        if layer is None:
            in_specs.append(pl.BlockSpec(block, lambda i, spb=steps_per_block: (i // spb, 0)))
        else:
            in_specs.append(pl.BlockSpec((None,) + block,
                                         lambda i, spb=steps_per_block, li=layer: (li, i // spb, 0)))
        out_specs.append(pl.BlockSpec(block, lambda i, spb=steps_per_block: (i // spb, 0)))
        out_shape.append(jax.ShapeDtypeStruct((rows, cols), BF16))
        args.append(w)
    return in_specs, out_specs, out_shape, args


def _oproj_ffn(attn, x2d, w_o, gains, w_gate_up, w_down, seq, to_cast=()):
    t, d = x2d.shape
    tm = FFN_TOKENS
    assert t % tm == 0 and seq % tm == 0 and D_FF % FFN_COLS == 0
    merge = not isinstance(attn, jax.Array)
    cast_in, cast_out, cast_shape, cast_args = _cast_specs(to_cast, t // tm)
    scratch = [pltpu.VMEM((tm, d), F32), pltpu.VMEM((tm, d), BF16), pltpu.VMEM((tm, D_FF), BF16)]
    resident = lambda shape: pl.BlockSpec(shape, lambda i: (0, 0), pipeline_mode=pl.Buffered(1))
    tok = pl.BlockSpec((tm, d), lambda i: (i, 0))
    if merge:
        tiles = seq // tm
        assert all(tm % (16 * dil) == 0 for _, dil in DILATED_GROUPS)
        blk = lambda dil, cols: pl.BlockSpec((1, dil, tm // dil, cols),
                                             lambda i: (i // tiles, 0, i % tiles, 0))
        head = jnp.arange(2 * LANES)[:, None] % LANES
        col = jnp.arange(d)[None, :] // HEAD_DIM
        spread = (head == col).astype(BF16)
        attn_specs = ([blk(dil, d) for _, dil in DILATED_GROUPS]
                      + [blk(dil, 2 * LANES) for _, dil in DILATED_GROUPS]
                      + [resident(spread.shape)])
        attn_args = [o for o, _ in attn] + [s for _, s in attn] + [spread]
        n_strided = sum(dil > 1 for _, dil in DILATED_GROUPS)
        scratch += ([pltpu.VMEM((d // LANES, tm, LANES), F32)] * n_strided
                    + [pltpu.VMEM((2, tm, LANES), F32)] * n_strided)
    else:
        attn_specs = [tok]
        attn_args = [attn]
    out, *cast = pl.pallas_call(
        functools.partial(_oproj_ffn_kernel, merge=merge, n_cast=len(cast_args)),
        grid=(t // tm,),
        in_specs=attn_specs + [
            tok,
            resident(w_o.shape),
            resident(gains.shape),
            resident(w_gate_up.shape),
            resident(w_down.shape)] + cast_in,
        out_specs=[tok] + cast_out,
        out_shape=[jax.ShapeDtypeStruct((t, d), F32)] + cast_shape,
        scratch_shapes=scratch,
        compiler_params=pltpu.CompilerParams(dimension_semantics=("arbitrary",),
                                             vmem_limit_bytes=VMEM_LIMIT_BYTES),
        name="oproj_ffn_merge" if merge else "oproj_ffn",
    )(*attn_args, x2d, w_o, gains, w_gate_up, w_down, *cast_args)
    return out, cast


def kernel(x, norm_gains, w_qkv_a, w_o_a, g_kv, w_kv_b, w_q_b, w_o_b, rel_bias, w_gate_up, w_down):
    batch, seq, d = x.shape
    depth = norm_gains.shape[0]
    n_a = w_qkv_a.shape[0]
    t = batch * seq
    x2d = x.reshape(t, d)

    def layer_weights(layer):
        if layer < n_a:
            own = {"qkv": (w_qkv_a, layer), "o": (w_o_a, layer)}
        else:
            own = {"q": (w_q_b, layer - n_a), "o": (w_o_b, layer - n_a)}
            if layer == n_a:
                own["kv"] = (w_kv_b, None)
        return {**own, "gate_up": (w_gate_up, layer), "down": (w_down, layer)}

    pending = layer_weights(0)
    eager = ["qkv"] if n_a > 0 else list(pending)
    bf16 = {name: (w if idx is None else w[idx]).astype(BF16)
            for name, (w, idx) in ((name, pending.pop(name)) for name in eager)}
    shared_kv = None
    for layer in range(depth):
        g = norm_gains[layer]
        if layer < n_a:
            qkv, cast = _norm_proj(x2d, g[0], bf16["qkv"], to_cast=list(pending.values()))
            bf16.update(zip(pending, cast))
            pending = {}
            attn = _stick_attention(qkv, batch, seq).reshape(t, d)
        else:
            if shared_kv is None:
                q, shared_kv = _norm_proj_groups(x2d, [g[0], g_kv], [bf16["q"], bf16["kv"]], batch, seq)
            else:
                (q,) = _norm_proj_groups(x2d, [g[0]], [bf16["q"]], batch, seq)
            attn = [_dilated_group(q[grp], shared_kv[2 * grp], shared_kv[2 * grp + 1], rel_bias, grp)
                    for grp in range(N_GROUPS)]
        following = layer_weights(layer + 1) if layer + 1 < depth else {}
        x2d, cast = _oproj_ffn(attn, x2d, bf16["o"], g, bf16["gate_up"], bf16["down"], seq,
                               to_cast=list(following.values()))
        bf16 = dict(zip(following, cast))
    return x2d.reshape(batch, seq, d)
```

```python
import functools
import math

import jax
import jax.numpy as jnp
from jax import lax
from jax.experimental import pallas as pl
from jax.experimental.pallas import tpu as pltpu

D_MODEL = 1024
N_HEADS = 16
HEAD_DIM = D_MODEL // N_HEADS
LANES = 128
N_PAIRS = D_MODEL // LANES
Q_BLOCK = 128
DILATED_GROUPS = ((128, 1), (512, 4), (2048, 16))
N_GROUPS = len(DILATED_GROUPS)
NUM_BUCKETS = 32
MAX_DISTANCE = 2048
D_FF = -(-8 * D_MODEL // (3 * 256)) * 256
RMS_EPS = 1e-6
NEG_INF = -1e30
QK_SCALE = HEAD_DIM ** -0.5
LOG2_E = math.log2(math.e)
DILATED_Q_SCALE = QK_SCALE * LOG2_E

VMEM_LIMIT_BYTES = 56 * 1024 * 1024
PROJ_TOKENS = 512
PROJ_COLS = 512
FFN_TOKENS = 512
FFN_COLS = 256
DILATED_SUBS = 4
STICK_PAIRS = 2
STICK_QUERIES = 256
STICK_KEYS = 256
STICK_UNDERFLOW = 128.0

F32 = jnp.float32
BF16 = jnp.bfloat16


def _rms_scale(x, gain_row):
    ms = jnp.mean(x * x, axis=-1, keepdims=True)
    return x * lax.rsqrt(ms + RMS_EPS) * gain_row


def _split_heads(tile):
    lane = lax.broadcasted_iota(jnp.int32, tile.shape, 1)
    zero = jnp.zeros_like(tile)
    return jnp.concatenate([jnp.where(lane < HEAD_DIM, tile, zero),
                            jnp.where(lane >= HEAD_DIM, tile, zero)], axis=0)


def _merge_heads(stacked):
    rows = stacked.shape[0] // 2
    lane = lax.broadcasted_iota(jnp.int32, (rows, LANES), 1)
    return jnp.where(lane < HEAD_DIM, stacked[:rows], stacked[rows:])


def _norm_proj_kernel(x_ref, g_ref, w_ref, *rest, n_cast):
    cast_in, o_ref, cast_out = rest[:n_cast], rest[n_cast], rest[n_cast + 1:]
    for src, dst in zip(cast_in, cast_out):
        dst[...] = src[...].astype(BF16)
    xn = _rms_scale(x_ref[...], g_ref[...]).astype(BF16)
    n_out = w_ref.shape[1]
    for c in range(n_out // PROJ_COLS):
        lo = c * PROJ_COLS
        res = jnp.dot(xn, w_ref[:, lo:lo + PROJ_COLS], preferred_element_type=F32).astype(o_ref.dtype)
        for k in range(PROJ_COLS // LANES):
            o_ref[lo // LANES + k] = res[:, k * LANES:(k + 1) * LANES]


def _norm_proj(x2d, gain, w, to_cast=()):
    t, d = x2d.shape
    n = w.shape[1]
    assert t % PROJ_TOKENS == 0 and n % PROJ_COLS == 0
    cast_in, cast_out, cast_shape, cast_args = _cast_specs(to_cast, t // PROJ_TOKENS)
    out, *cast = pl.pallas_call(
        functools.partial(_norm_proj_kernel, n_cast=len(cast_args)),
        grid=(t // PROJ_TOKENS,),
        in_specs=[pl.BlockSpec((PROJ_TOKENS, d), lambda i: (i, 0)),
                  pl.BlockSpec((1, d), lambda i: (0, 0)),
                  pl.BlockSpec((d, n), lambda i: (0, 0), pipeline_mode=pl.Buffered(1))] + cast_in,
        out_specs=[pl.BlockSpec((n // LANES, PROJ_TOKENS, LANES), lambda i: (0, i, 0))] + cast_out,
        out_shape=[jax.ShapeDtypeStruct((n // LANES, t, LANES), BF16)] + cast_shape,
        compiler_params=pltpu.CompilerParams(dimension_semantics=("arbitrary",),
                                             vmem_limit_bytes=VMEM_LIMIT_BYTES),
        name="norm_proj",
    )(x2d, gain.reshape(1, d), w, *cast_args)
    return out, cast


def _norm_proj_groups_kernel(x_ref, g_ref, *rest, chunks):
    n_proj = len(chunks)
    w_refs, out_refs, (xs_ref, xp_ref) = rest[:n_proj], rest[n_proj:-2], rest[-2:]
    tm = x_ref.shape[0]
    x = x_ref[...]
    xs = x * lax.rsqrt(jnp.mean(x * x, axis=-1, keepdims=True) + RMS_EPS)
    for cb in range(N_PAIRS):
        xs_ref[cb] = xs[:, cb * LANES:(cb + 1) * LANES]
    first_out = [sum(chunks[:p]) * N_GROUPS for p in range(n_proj)]
    for g, (_, dil) in enumerate(DILATED_GROUPS):
        n = tm // dil
        if dil > 1:
            for r in range(dil):
                for cb in range(N_PAIRS):
                    xp_ref[r * n:(r + 1) * n, cb * LANES:(cb + 1) * LANES] = (
                        xs_ref[cb, pl.ds(r, n, stride=dil), :])
        for p in range(n_proj):
            rows = xs if dil == 1 else xp_ref[...]
            xn = (rows * g_ref[p:p + 1, :]).astype(BF16)
            for c in range(chunks[p]):
                col = (g * chunks[p] + c) * D_MODEL
                res = jnp.dot(xn, w_refs[p][:, col:col + D_MODEL],
                              preferred_element_type=F32).astype(BF16)
                out = out_refs[first_out[p] + g * chunks[p] + c]
                for r in range(dil):
                    out[0, r] = res[r * n:(r + 1) * n]


def _norm_proj_groups(x2d, gains, ws, batch, seq):
    t, d = x2d.shape
    chunks = tuple(w.shape[1] // (N_GROUPS * d) for w in ws)
    tiles = seq // PROJ_TOKENS
    assert seq % PROJ_TOKENS == 0 and all(PROJ_TOKENS % (16 * dil) == 0 for _, dil in DILATED_GROUPS)
    out_shape, out_specs = [], []
    for n_chunks in chunks:
        for _, dil in DILATED_GROUPS:
            for _ in range(n_chunks):
                out_shape.append(jax.ShapeDtypeStruct((batch, dil, seq // dil, d), BF16))
                out_specs.append(pl.BlockSpec((1, dil, PROJ_TOKENS // dil, d), lambda b, j: (b, 0, j, 0)))
    outs = pl.pallas_call(
        functools.partial(_norm_proj_groups_kernel, chunks=chunks),
        grid=(batch, tiles),
        in_specs=([pl.BlockSpec((PROJ_TOKENS, d), lambda b, j: (b * tiles + j, 0)),
                   pl.BlockSpec((len(ws), d), lambda b, j: (0, 0))]
                  + [pl.BlockSpec(w.shape, lambda b, j: (0, 0), pipeline_mode=pl.Buffered(1)) for w in ws]),
        out_specs=out_specs,
        out_shape=out_shape,
        scratch_shapes=[pltpu.VMEM((d // LANES, PROJ_TOKENS, LANES), F32),
                        pltpu.VMEM((PROJ_TOKENS, d), F32)],
        compiler_params=pltpu.CompilerParams(dimension_semantics=("parallel", "parallel"),
                                             vmem_limit_bytes=VMEM_LIMIT_BYTES),
        name="norm_proj_groups",
    )(x2d, jnp.stack(gains), *ws)
    split, start = [], 0
    for n_chunks in chunks:
        split.append(outs[start:start + n_chunks * N_GROUPS])
        start += n_chunks * N_GROUPS
    return split


def _stick_kernel(q_ref, k_ref, v_ref, u_ref, o_ref, acc_ref, carry_ref):
    rows = 2 * STICK_QUERIES
    nq = q_ref.shape[2] // STICK_QUERIES

    def span(ref, pair, idx):
        return ref[pair, 0, idx * STICK_KEYS:(idx + 1) * STICK_KEYS, :]

    def weights(z, carry, diagonal):
        cost = jnp.maximum(z, 0.0) + jnp.log(1.0 + jnp.exp(-jnp.abs(z)))
        if diagonal:
            qrow = lax.broadcasted_iota(jnp.int32, z.shape, 0) % STICK_QUERIES
            kcol = lax.broadcasted_iota(jnp.int32, z.shape, 1)
            strict = kcol < qrow
            cost = jnp.where(strict, cost, 0.0)
        w = z - jnp.dot(cost.astype(BF16), u_ref[...], preferred_element_type=F32)
        a = jnp.concatenate([jnp.exp(w[:, :LANES] - carry), jnp.exp(w[:, LANES:] - carry)], axis=1)
        if diagonal:
            a = jnp.where(strict, a, 0.0)
        return a.astype(BF16), carry + jnp.sum(cost, axis=-1, keepdims=True)

    def walk(pair, qi, spans, acc, carry):
        qs = _split_heads(span(q_ref, pair, qi)) * QK_SCALE
        for j in spans:
            z = lax.dot_general(qs, span(k_ref, pair, j), (((1,), (1,)), ((), ())),
                                preferred_element_type=F32)
            a, carry = weights(z, carry, j == qi)
            acc = acc + jnp.dot(a, span(v_ref, pair, j), preferred_element_type=F32)
        return acc, carry

    alive = {}
    for qi in range(nq):
        for pair in range(STICK_PAIRS):
            zeros = jnp.zeros((rows, LANES), F32)
            acc, carry = walk(pair, qi, range(qi, max(qi - 2, -1), -1), zeros, zeros)
            acc_ref[pair, qi] = acc
            if qi >= 2:
                carry_ref[pair, qi] = carry
                alive[pair, qi] = jnp.min(carry, axis=0, keepdims=True)[0, 0] < STICK_UNDERFLOW

    for pair, qi in alive:
        @pl.when(alive[pair, qi])
        def _(pair=pair, qi=qi):
            acc, _ = walk(pair, qi, range(qi - 2, -1, -1), acc_ref[pair, qi], carry_ref[pair, qi])
            acc_ref[pair, qi] = acc

    for qi in range(nq):
        for pair in range(STICK_PAIRS):
            o_ref[0, qi * STICK_QUERIES:(qi + 1) * STICK_QUERIES, pair * LANES:(pair + 1) * LANES] = (
                _merge_heads(acc_ref[pair, qi]).astype(o_ref.dtype))


def _stick_attention(qkv, batch, seq):
    assert STICK_QUERIES == STICK_KEYS and seq % STICK_QUERIES == 0
    qkv = qkv.reshape(3 * N_PAIRS, batch, seq, LANES)
    assert STICK_KEYS == 2 * LANES
    j = jnp.arange(STICK_KEYS)[:, None]
    s = jnp.arange(STICK_KEYS)[None, :]
    uu = (j >= s).astype(BF16)
    assert N_PAIRS % STICK_PAIRS == 0
    steps = N_PAIRS // STICK_PAIRS
    return pl.pallas_call(
        _stick_kernel,
        grid=(batch, steps),
        in_specs=[pl.BlockSpec((STICK_PAIRS, 1, seq, LANES), lambda b, p: (p, b, 0, 0)),
                  pl.BlockSpec((STICK_PAIRS, 1, seq, LANES), lambda b, p: (steps + p, b, 0, 0)),
                  pl.BlockSpec((STICK_PAIRS, 1, seq, LANES), lambda b, p: (2 * steps + p, b, 0, 0)),
                  pl.BlockSpec((STICK_KEYS, STICK_KEYS), lambda b, p: (0, 0))],
        out_specs=pl.BlockSpec((1, seq, STICK_PAIRS * LANES), lambda b, p: (b, 0, p)),
        out_shape=jax.ShapeDtypeStruct((batch, seq, D_MODEL), BF16),
        scratch_shapes=[pltpu.VMEM((STICK_PAIRS, seq // STICK_QUERIES, 2 * STICK_QUERIES, LANES),
                                   F32)] * 2,
        compiler_params=pltpu.CompilerParams(dimension_semantics=("parallel", "parallel"),
                                             vmem_limit_bytes=VMEM_LIMIT_BYTES),
        name="stick_attention",
    )(qkv, qkv, qkv, uu)


def _relative_bucket(distance):
    max_exact = NUM_BUCKETS // 2
    n = jnp.maximum(distance, 0)
    large = max_exact + (jnp.log(jnp.maximum(n, 1).astype(F32) / max_exact)
                         / math.log(MAX_DISTANCE / max_exact)
                         * (NUM_BUCKETS - max_exact)).astype(jnp.int32)
    large = jnp.minimum(large, NUM_BUCKETS - 1)
    return jnp.where(n < max_exact, n, large)


def _band_bias(rel_bias, window, dilation):
    i = jnp.arange(Q_BLOCK)[:, None]
    m = jnp.arange(2 * Q_BLOCK)[None, :]
    rel = Q_BLOCK + i - m
    band = (rel >= 0) & (rel <= window // dilation)
    onehot = jax.nn.one_hot(_relative_bucket(rel * dilation), NUM_BUCKETS, dtype=F32)
    bias = jnp.einsum('qkb,bh->hqk', onehot, rel_bias.astype(F32), precision=lax.Precision.HIGHEST)
    bias = jnp.where(band[None], bias, NEG_INF)
    return bias.reshape(N_PAIRS, 2 * Q_BLOCK, 2 * Q_BLOCK)


def _band_attend(q_tile, kk, vv, bias):
    qs = _split_heads(q_tile)
    s = lax.dot_general(qs, kk, (((1,), (1,)), ((), ())), preferred_element_type=F32) + bias
    m = jnp.max(s, axis=-1, keepdims=True)
    p = jnp.exp2(s - m)
    l = jnp.sum(p, axis=-1, keepdims=True)
    return jnp.dot(p.astype(BF16), vv, preferred_element_type=F32), m, l


def _dilated_kernel(*refs, with_prev):
    if with_prev:
        q_ref, kp_ref, kc_ref, vp_ref, vc_ref, bias_first_ref, bias_ref, o_ref, s_ref = refs
    else:
        q_ref, kc_ref, vc_ref, bias_ref, o_ref, s_ref = refs
    for sub in range(DILATED_SUBS):
        rows = slice(sub * Q_BLOCK, (sub + 1) * Q_BLOCK)
        stats_at = (lambda cols: (0, 0, rows, cols)) if with_prev else (lambda cols: (0, sub, slice(None), cols))
        s_ref[stats_at(slice(None))] = jnp.concatenate(
            [jnp.zeros((Q_BLOCK, LANES), F32), jnp.ones((Q_BLOCK, LANES), F32)], axis=1)
        for hp in range(N_PAIRS):
            sl = slice(hp * LANES, (hp + 1) * LANES)
            if not with_prev:
                q_tile, kk, vv = q_ref[0, sub, :, sl], kc_ref[0, sub, :, sl], vc_ref[0, sub, :, sl]
                bias = bias_ref[hp]
            elif sub == 0:
                q_tile = q_ref[0, 0, rows, sl]
                kk = jnp.concatenate([kp_ref[0, 0, :, sl], kc_ref[0, 0, rows, sl]], axis=0)
                vv = jnp.concatenate([vp_ref[0, 0, :, sl], vc_ref[0, 0, rows, sl]], axis=0)
                bias = bias_first_ref[0, hp]
            else:
                keys = slice((sub - 1) * Q_BLOCK, (sub + 1) * Q_BLOCK)
                q_tile, kk, vv = q_ref[0, 0, rows, sl], kc_ref[0, 0, keys, sl], vc_ref[0, 0, keys, sl]
                bias = bias_ref[0, hp]
            o, m, l = _band_attend(q_tile, kk, vv, bias)
            if with_prev:
                o_ref[0, 0, rows, sl] = _merge_heads(o).astype(o_ref.dtype)
            else:
                o_ref[0, sub, :, sl] = _merge_heads(o).astype(o_ref.dtype)
            for head, half in ((2 * hp, slice(0, Q_BLOCK)), (2 * hp + 1, slice(Q_BLOCK, None))):
                s_ref[stats_at(slice(head, head + 1))] = m[half]
                s_ref[stats_at(slice(LANES + head, LANES + head + 1))] = l[half]


def _dilated_group(q, k, v, rel_bias, group):
    window, dilation = DILATED_GROUPS[group]
    batch, _, strided, d = q.shape
    assert strided % Q_BLOCK == 0
    nb = strided // Q_BLOCK
    with_prev = nb > 1
    bias = _band_bias(rel_bias, window, dilation) * LOG2_E
    if with_prev:
        assert nb % DILATED_SUBS == 0
        key = jnp.arange(2 * Q_BLOCK)[None, None, :]
        bias = jnp.stack([jnp.where(key < Q_BLOCK, NEG_INF, bias), bias])
        grid = (batch, dilation, nb // DILATED_SUBS)
        two = pl.BlockSpec((1, 1, DILATED_SUBS * Q_BLOCK, d), lambda b, r, i: (b, r, i, 0))
        prev = pl.BlockSpec((1, 1, Q_BLOCK, d),
                            lambda b, r, i: (b, r, jnp.maximum(DILATED_SUBS * i - 1, 0), 0))
        table = (1,) + bias.shape[1:]
        in_specs = [two, prev, two, prev, two,
                    pl.BlockSpec(table, lambda b, r, i: (jnp.minimum(i, 1), 0, 0, 0)),
                    pl.BlockSpec(table, lambda b, r, i: (1, 0, 0, 0))]
        args = [q, k, k, v, v, bias, bias]
        out_specs = [two, pl.BlockSpec((1, 1, DILATED_SUBS * Q_BLOCK, 2 * LANES),
                                       lambda b, r, i: (b, r, i, 0))]
    else:
        assert dilation % DILATED_SUBS == 0
        bias = bias[:, :, Q_BLOCK:]
        grid = (batch, dilation // DILATED_SUBS, 1)
        two = pl.BlockSpec((1, DILATED_SUBS, Q_BLOCK, d), lambda b, r, i: (b, r, 0, 0))
        in_specs = [two, two, two, pl.BlockSpec(bias.shape, lambda b, r, i: (0, 0, 0))]
        args = [q, k, v, bias]
        out_specs = [two, pl.BlockSpec((1, DILATED_SUBS, Q_BLOCK, 2 * LANES), lambda b, r, i: (b, r, 0, 0))]
    return pl.pallas_call(
        functools.partial(_dilated_kernel, with_prev=with_prev),
        grid=grid,
        in_specs=in_specs,
        out_specs=out_specs,
        out_shape=[jax.ShapeDtypeStruct((batch, dilation, strided, d), BF16),
                   jax.ShapeDtypeStruct((batch, dilation, strided, 2 * LANES), F32)],
        compiler_params=pltpu.CompilerParams(dimension_semantics=("parallel", "parallel", "parallel"),
                                             vmem_limit_bytes=VMEM_LIMIT_BYTES),
        name=f"dilated_group{group}",
    )(*args)


def _merge_groups(o_refs, s_refs, e_ref, o_tok_refs, s_tok_refs):
    outs, tops, dens = [], [], []
    for g, (_, dil) in enumerate(DILATED_GROUPS):
        if dil == 1:
            outs.append(o_refs[g][0, 0].astype(F32))
            tops.append(s_refs[g][0, 0, :, :LANES])
            dens.append(s_refs[g][0, 0, :, LANES:])
        else:
            n = o_refs[g].shape[2]
            for r in range(dil):
                rows = pl.ds(r, n, stride=dil)
                part = o_refs[g][0, r].astype(F32)
                for cb in range(N_PAIRS):
                    o_tok_refs[g][cb, rows, :] = part[:, cb * LANES:(cb + 1) * LANES]
                s_tok_refs[g][0, rows, :] = s_refs[g][0, r, :, :LANES]
                s_tok_refs[g][1, rows, :] = s_refs[g][0, r, :, LANES:]
            outs.append(jnp.concatenate([o_tok_refs[g][cb] for cb in range(N_PAIRS)], axis=1))
            tops.append(s_tok_refs[g][0])
            dens.append(s_tok_refs[g][1])
    top = functools.reduce(jnp.maximum, tops)
    es = [jnp.exp2(m - top) for m in tops]
    inv = 1.0 / functools.reduce(jnp.add, [e * l for e, l in zip(es, dens)])
    merged = None
    for out, e in zip(outs, es):
        wt = e * inv
        hi = wt.astype(BF16)
        lo = (wt - hi.astype(F32)).astype(BF16)
        spread = jnp.dot(jnp.concatenate([hi, lo], axis=1), e_ref[...], preferred_element_type=F32)
        merged = spread * out if merged is None else merged + spread * out
    return merged.astype(BF16)


def _oproj_ffn_kernel(*refs, merge, n_cast):
    if merge:
        o_refs, s_refs, e_ref = refs[0:N_GROUPS], refs[N_GROUPS:2 * N_GROUPS], refs[2 * N_GROUPS]
        refs = refs[2 * N_GROUPS + 1:]
        strided_groups = [g for g, (_, dil) in enumerate(DILATED_GROUPS) if dil > 1]
        n_tok = len(strided_groups)
        o_tok_refs = dict(zip(strided_groups, refs[-2 * n_tok:-n_tok]))
        s_tok_refs = dict(zip(strided_groups, refs[-n_tok:]))
        refs = refs[:-2 * n_tok]
    else:
        o_ref, refs = refs[0], refs[1:]
    x_ref, wo_ref, g_ref, wgu_ref, wd_ref = refs[:5]
    cast_in, out_ref, cast_out = refs[5:5 + n_cast], refs[5 + n_cast], refs[6 + n_cast:6 + 2 * n_cast]
    xmid_ref, xn_ref, hid_ref = refs[6 + 2 * n_cast:]
    for src, dst in zip(cast_in, cast_out):
        dst[...] = src[...].astype(BF16)
    if merge:
        attn = _merge_groups(o_refs, s_refs, e_ref, o_tok_refs, s_tok_refs)
    else:
        attn = o_ref[...]
    tm = x_ref.shape[0]
    halves = [slice(p * (tm // 2), (p + 1) * (tm // 2)) for p in range(2)]

    def swiglu_chunk(c, rows):
        lo = c * FFN_COLS
        xn = xn_ref[rows, :]
        gate = jnp.dot(xn, wgu_ref[:, lo:lo + FFN_COLS], preferred_element_type=F32)
        up = jnp.dot(xn, wgu_ref[:, D_FF + lo:D_FF + lo + FFN_COLS], preferred_element_type=F32)
        hid_ref[rows, lo:lo + FFN_COLS] = (gate * jax.nn.sigmoid(gate) * up).astype(BF16)

    hs = [jnp.dot(attn[rows], wo_ref[...], preferred_element_type=F32) for rows in halves]
    for rows, h in zip(halves, hs):
        xm = x_ref[rows, :] + _rms_scale(h, g_ref[1:2, :])
        xmid_ref[rows, :] = xm
        xn_ref[rows, :] = _rms_scale(xm, g_ref[2:3, :]).astype(BF16)
        swiglu_chunk(0, rows)
    for c in range(1, D_FF // FFN_COLS):
        swiglu_chunk(c, slice(None))
    for rows in halves:
        down = jnp.dot(hid_ref[rows, :], wd_ref[...], preferred_element_type=F32)
        out_ref[rows, :] = xmid_ref[rows, :] + _rms_scale(down, g_ref[3:4, :])


def _cast_specs(to_cast, n_steps):
    in_specs, out_specs, out_shape, args = [], [], [], []
    for w, layer in to_cast:
        rows, cols = w.shape[-2:]
        n_blocks = n_steps
        while n_blocks > 1 and (rows % n_blocks or (rows // n_blocks) % 16):
            n_blocks //= 2
        assert n_steps % n_blocks == 0
        steps_per_block = n_steps // n_blocks
        block = (rows // n_blocks, cols)
        if layer is None:
            in_specs.append(pl.BlockSpec(block, lambda i, spb=steps_per_block: (i // spb, 0)))
        else:
            in_specs.append(pl.BlockSpec((None,) + block,
                                         lambda i, spb=steps_per_block, li=layer: (li, i // spb, 0)))
        out_specs.append(pl.BlockSpec(block, lambda i, spb=steps_per_block: (i // spb, 0)))
        out_shape.append(jax.ShapeDtypeStruct((rows, cols), BF16))
        args.append(w)
    return in_specs, out_specs, out_shape, args


def _oproj_ffn(attn, x2d, w_o, gains, w_gate_up, w_down, seq, to_cast=()):
    t, d = x2d.shape
    tm = FFN_TOKENS
    assert t % tm == 0 and seq % tm == 0 and D_FF % FFN_COLS == 0
    merge = not isinstance(attn, jax.Array)
    cast_in, cast_out, cast_shape, cast_args = _cast_specs(to_cast, t // tm)
    scratch = [pltpu.VMEM((tm, d), F32), pltpu.VMEM((tm, d), BF16), pltpu.VMEM((tm, D_FF), BF16)]
    resident = lambda shape: pl.BlockSpec(shape, lambda i: (0, 0), pipeline_mode=pl.Buffered(1))
    tok = pl.BlockSpec((tm, d), lambda i: (i, 0))
    if merge:
        tiles = seq // tm
        assert all(tm % (16 * dil) == 0 for _, dil in DILATED_GROUPS)
        blk = lambda dil, cols: pl.BlockSpec((1, dil, tm // dil, cols),
                                             lambda i: (i // tiles, 0, i % tiles, 0))
        head = jnp.arange(2 * LANES)[:, None] % LANES
        col = jnp.arange(d)[None, :] // HEAD_DIM
        spread = (head == col).astype(BF16)
        attn_specs = ([blk(dil, d) for _, dil in DILATED_GROUPS]
                      + [blk(dil, 2 * LANES) for _, dil in DILATED_GROUPS]
                      + [resident(spread.shape)])
        attn_args = [o for o, _ in attn] + [s for _, s in attn] + [spread]
        n_strided = sum(dil > 1 for _, dil in DILATED_GROUPS)
        scratch += ([pltpu.VMEM((d // LANES, tm, LANES), F32)] * n_strided
                    + [pltpu.VMEM((2, tm, LANES), F32)] * n_strided)
    else:
        attn_specs = [tok]
        attn_args = [attn]
    out, *cast = pl.pallas_call(
        functools.partial(_oproj_ffn_kernel, merge=merge, n_cast=len(cast_args)),
        grid=(t // tm,),
        in_specs=attn_specs + [
            tok,
            resident(w_o.shape),
            resident(gains.shape),
            resident(w_gate_up.shape),
            resident(w_down.shape)] + cast_in,
        out_specs=[tok] + cast_out,
        out_shape=[jax.ShapeDtypeStruct((t, d), F32)] + cast_shape,
        scratch_shapes=scratch,
        compiler_params=pltpu.CompilerParams(dimension_semantics=("arbitrary",),
                                             vmem_limit_bytes=VMEM_LIMIT_BYTES),
        name="oproj_ffn_merge" if merge else "oproj_ffn",
    )(*attn_args, x2d, w_o, gains, w_gate_up, w_down, *cast_args)
    return out, cast


def kernel(x, norm_gains, w_qkv_a, w_o_a, g_kv, w_kv_b, w_q_b, w_o_b, rel_bias, w_gate_up, w_down):
    batch, seq, d = x.shape
    depth = norm_gains.shape[0]
    n_a = w_qkv_a.shape[0]
    t = batch * seq
    x2d = x.reshape(t, d)

    def layer_weights(layer):
        if layer < n_a:
            own = {"qkv": (w_qkv_a, layer), "o": (w_o_a, layer)}
        else:
            own = {"q": (w_q_b, layer - n_a), "o": (w_o_b, layer - n_a)}
            if layer == n_a:
                own["kv"] = (w_kv_b, None)
        return {**own, "gate_up": (w_gate_up, layer), "down": (w_down, layer)}

    pending = layer_weights(0)
    eager = ["qkv"] if n_a > 0 else list(pending)
    bf16 = {name: (w if idx is None else w[idx]).astype(BF16)
            for name, (w, idx) in ((name, pending.pop(name)) for name in eager)}
    shared_kv = None
    for layer in range(depth):
        g = norm_gains[layer]
        if layer < n_a:
            qkv, cast = _norm_proj(x2d, g[0], bf16["qkv"], to_cast=list(pending.values()))
            bf16.update(zip(pending, cast))
            pending = {}
            attn = _stick_attention(qkv, batch, seq).reshape(t, d)
        else:
            q_gain = g[0] * DILATED_Q_SCALE
            if shared_kv is None:
                q, shared_kv = _norm_proj_groups(x2d, [q_gain, g_kv], [bf16["q"], bf16["kv"]], batch, seq)
            else:
                (q,) = _norm_proj_groups(x2d, [q_gain], [bf16["q"]], batch, seq)
            attn = [_dilated_group(q[grp], shared_kv[2 * grp], shared_kv[2 * grp + 1], rel_bias, grp)
                    for grp in range(N_GROUPS)]
        following = layer_weights(layer + 1) if layer + 1 < depth else {}
        x2d, cast = _oproj_ffn(attn, x2d, bf16["o"], g, bf16["gate_up"], bf16["down"], seq,
                               to_cast=list(following.values()))
        bf16 = dict(zip(following, cast))
    return x2d.reshape(batch, seq, d)
```

```python
import functools
import math

import jax
import jax.numpy as jnp
from jax import lax
from jax.experimental import pallas as pl
from jax.experimental.pallas import tpu as pltpu

D_MODEL = 1024
N_HEADS = 16
HEAD_DIM = D_MODEL // N_HEADS
LANES = 128
N_PAIRS = D_MODEL // LANES
Q_BLOCK = 128
DILATED_GROUPS = ((128, 1), (512, 4), (2048, 16))
N_GROUPS = len(DILATED_GROUPS)
NUM_BUCKETS = 32
MAX_DISTANCE = 2048
D_FF = -(-8 * D_MODEL // (3 * 256)) * 256
RMS_EPS = 1e-6
NEG_INF = -1e30
QK_SCALE = HEAD_DIM ** -0.5
LOG2_E = math.log2(math.e)
DILATED_Q_SCALE = QK_SCALE * LOG2_E

VMEM_LIMIT_BYTES = 56 * 1024 * 1024
PROJ_TOKENS = 512
PROJ_COLS = 512
FFN_TOKENS = 512
FFN_COLS = 256
MERGE_STRIDE = 4
DILATED_SUBS = 4
STICK_PAIRS = 2
STICK_QUERIES = 256
STICK_KEYS = 256
STICK_UNDERFLOW = 128.0

F32 = jnp.float32
BF16 = jnp.bfloat16


def _rms_scale(x, gain_row):
    ms = jnp.mean(x * x, axis=-1, keepdims=True)
    return x * lax.rsqrt(ms + RMS_EPS) * gain_row


def _split_heads(tile):
    lane = lax.broadcasted_iota(jnp.int32, tile.shape, 1)
    zero = jnp.zeros_like(tile)
    return jnp.concatenate([jnp.where(lane < HEAD_DIM, tile, zero),
                            jnp.where(lane >= HEAD_DIM, tile, zero)], axis=0)


def _merge_heads(stacked):
    rows = stacked.shape[0] // 2
    lane = lax.broadcasted_iota(jnp.int32, (rows, LANES), 1)
    return jnp.where(lane < HEAD_DIM, stacked[:rows], stacked[rows:])


def _norm_proj_kernel(x_ref, g_ref, w_ref, *rest, n_cast):
    cast_in, o_ref, cast_out = rest[:n_cast], rest[n_cast], rest[n_cast + 1:]
    for src, dst in zip(cast_in, cast_out):
        dst[...] = src[...].astype(BF16)
    xn = _rms_scale(x_ref[...], g_ref[...]).astype(BF16)
    n_out = w_ref.shape[1]
    for c in range(n_out // PROJ_COLS):
        lo = c * PROJ_COLS
        res = jnp.dot(xn, w_ref[:, lo:lo + PROJ_COLS], preferred_element_type=F32).astype(o_ref.dtype)
        for k in range(PROJ_COLS // LANES):
            o_ref[lo // LANES + k] = res[:, k * LANES:(k + 1) * LANES]


def _norm_proj(x2d, gain, w, to_cast=()):
    t, d = x2d.shape
    n = w.shape[1]
    assert t % PROJ_TOKENS == 0 and n % PROJ_COLS == 0
    cast_in, cast_out, cast_shape, cast_args = _cast_specs(to_cast, t // PROJ_TOKENS)
    out, *cast = pl.pallas_call(
        functools.partial(_norm_proj_kernel, n_cast=len(cast_args)),
        grid=(t // PROJ_TOKENS,),
        in_specs=[pl.BlockSpec((PROJ_TOKENS, d), lambda i: (i, 0)),
                  pl.BlockSpec((1, d), lambda i: (0, 0)),
                  pl.BlockSpec((d, n), lambda i: (0, 0), pipeline_mode=pl.Buffered(1))] + cast_in,
        out_specs=[pl.BlockSpec((n // LANES, PROJ_TOKENS, LANES), lambda i: (0, i, 0))] + cast_out,
        out_shape=[jax.ShapeDtypeStruct((n // LANES, t, LANES), BF16)] + cast_shape,
        compiler_params=pltpu.CompilerParams(dimension_semantics=("arbitrary",),
                                             vmem_limit_bytes=VMEM_LIMIT_BYTES),
        name="norm_proj",
    )(x2d, gain.reshape(1, d), w, *cast_args)
    return out, cast


def _norm_proj_groups_kernel(x_ref, g_ref, *rest, chunks):
    n_proj = len(chunks)
    w_refs, out_refs, slabs = rest[:n_proj], rest[n_proj:-3], rest[-3:]
    tm = x_ref.shape[0]
    x = x_ref[...]
    xs = x * lax.rsqrt(jnp.mean(x * x, axis=-1, keepdims=True) + RMS_EPS)
    src, src_dil = slabs[0], 1
    for cb in range(N_PAIRS):
        src[cb] = xs[:, cb * LANES:(cb + 1) * LANES]
    first_out = [sum(chunks[:p]) * N_GROUPS for p in range(n_proj)]
    for g, (_, dil) in enumerate(DILATED_GROUPS):
        n = tm // dil
        if dil > 1:
            ratio = dil // src_dil
            dst = slabs[1] if src is not slabs[1] else slabs[2]
            for r in range(dil):
                a, b = divmod(r, src_dil)
                for cb in range(N_PAIRS):
                    dst[cb, r * n:(r + 1) * n, :] = src[cb, pl.ds(b * (tm // src_dil) + a, n, stride=ratio), :]
            src, src_dil = dst, dil
            regrouped = jnp.concatenate([src[cb] for cb in range(N_PAIRS)], axis=1)
        for p in range(n_proj):
            rows = xs if dil == 1 else regrouped
            xn = (rows * g_ref[p:p + 1, :]).astype(BF16)
            for c in range(chunks[p]):
                col = (g * chunks[p] + c) * D_MODEL
                res = jnp.dot(xn, w_refs[p][:, col:col + D_MODEL],
                              preferred_element_type=F32).astype(BF16)
                out = out_refs[first_out[p] + g * chunks[p] + c]
                for r in range(dil):
                    out[0, r] = res[r * n:(r + 1) * n]


def _norm_proj_groups(x2d, gains, ws, batch, seq):
    t, d = x2d.shape
    chunks = tuple(w.shape[1] // (N_GROUPS * d) for w in ws)
    tiles = seq // PROJ_TOKENS
    assert seq % PROJ_TOKENS == 0 and all(PROJ_TOKENS % (16 * dil) == 0 for _, dil in DILATED_GROUPS)
    dilations = [dil for _, dil in DILATED_GROUPS]
    assert dilations[0] == 1 and all(b % a == 0 for a, b in zip(dilations, dilations[1:]))
    out_shape, out_specs = [], []
    for n_chunks in chunks:
        for _, dil in DILATED_GROUPS:
            for _ in range(n_chunks):
                out_shape.append(jax.ShapeDtypeStruct((batch, dil, seq // dil, d), BF16))
                out_specs.append(pl.BlockSpec((1, dil, PROJ_TOKENS // dil, d), lambda b, j: (b, 0, j, 0)))
    outs = pl.pallas_call(
        functools.partial(_norm_proj_groups_kernel, chunks=chunks),
        grid=(batch, tiles),
        in_specs=([pl.BlockSpec((PROJ_TOKENS, d), lambda b, j: (b * tiles + j, 0)),
                   pl.BlockSpec((len(ws), d), lambda b, j: (0, 0))]
                  + [pl.BlockSpec(w.shape, lambda b, j: (0, 0), pipeline_mode=pl.Buffered(1)) for w in ws]),
        out_specs=out_specs,
        out_shape=out_shape,
        scratch_shapes=[pltpu.VMEM((d // LANES, PROJ_TOKENS, LANES), F32)] * 3,
        compiler_params=pltpu.CompilerParams(dimension_semantics=("parallel", "parallel"),
                                             vmem_limit_bytes=VMEM_LIMIT_BYTES),
        name="norm_proj_groups",
    )(x2d, jnp.stack(gains), *ws)
    split, start = [], 0
    for n_chunks in chunks:
        split.append(outs[start:start + n_chunks * N_GROUPS])
        start += n_chunks * N_GROUPS
    return split


def _stick_kernel(q_ref, k_ref, v_ref, u_ref, o_ref, acc_ref, carry_ref):
    rows = 2 * STICK_QUERIES
    nq = q_ref.shape[2] // STICK_QUERIES

    def span(ref, pair, idx):
        return ref[pair, 0, idx * STICK_KEYS:(idx + 1) * STICK_KEYS, :]

    def weights(z, carry, diagonal):
        cost = jnp.maximum(z, 0.0) + jnp.log(1.0 + jnp.exp(-jnp.abs(z)))
        if diagonal:
            qrow = lax.broadcasted_iota(jnp.int32, z.shape, 0) % STICK_QUERIES
            kcol = lax.broadcasted_iota(jnp.int32, z.shape, 1)
            strict = kcol < qrow
            cost = jnp.where(strict, cost, 0.0)
        w = z - jnp.dot(cost.astype(BF16), u_ref[...], preferred_element_type=F32)
        a = jnp.concatenate([jnp.exp(w[:, :LANES] - carry), jnp.exp(w[:, LANES:] - carry)], axis=1)
        if diagonal:
            a = jnp.where(strict, a, 0.0)
        return a.astype(BF16), carry + jnp.sum(cost, axis=-1, keepdims=True)

    def walk(pair, qi, spans, acc, carry):
        qs = _split_heads(span(q_ref, pair, qi)) * QK_SCALE
        for j in spans:
            z = lax.dot_general(qs, span(k_ref, pair, j), (((1,), (1,)), ((), ())),
                                preferred_element_type=F32)
            a, carry = weights(z, carry, j == qi)
            acc = acc + jnp.dot(a, span(v_ref, pair, j), preferred_element_type=F32)
        return acc, carry

    alive = {}
    for qi in range(nq):
        for pair in range(STICK_PAIRS):
            zeros = jnp.zeros((rows, LANES), F32)
            acc, carry = walk(pair, qi, range(qi, max(qi - 2, -1), -1), zeros, zeros)
            acc_ref[pair, qi] = acc
            if qi >= 2:
                carry_ref[pair, qi] = carry
                alive[pair, qi] = jnp.min(carry, axis=0, keepdims=True)[0, 0] < STICK_UNDERFLOW

    for pair, qi in alive:
        @pl.when(alive[pair, qi])
        def _(pair=pair, qi=qi):
            acc, _ = walk(pair, qi, range(qi - 2, -1, -1), acc_ref[pair, qi], carry_ref[pair, qi])
            acc_ref[pair, qi] = acc

    for qi in range(nq):
        for pair in range(STICK_PAIRS):
            o_ref[0, qi * STICK_QUERIES:(qi + 1) * STICK_QUERIES, pair * LANES:(pair + 1) * LANES] = (
                _merge_heads(acc_ref[pair, qi]).astype(o_ref.dtype))


def _stick_attention(qkv, batch, seq):
    assert STICK_QUERIES == STICK_KEYS and seq % STICK_QUERIES == 0
    qkv = qkv.reshape(3 * N_PAIRS, batch, seq, LANES)
    assert STICK_KEYS == 2 * LANES
    j = jnp.arange(STICK_KEYS)[:, None]
    s = jnp.arange(STICK_KEYS)[None, :]
    uu = (j >= s).astype(BF16)
    assert N_PAIRS % STICK_PAIRS == 0
    steps = N_PAIRS // STICK_PAIRS
    return pl.pallas_call(
        _stick_kernel,
        grid=(batch, steps),
        in_specs=[pl.BlockSpec((STICK_PAIRS, 1, seq, LANES), lambda b, p: (p, b, 0, 0)),
                  pl.BlockSpec((STICK_PAIRS, 1, seq, LANES), lambda b, p: (steps + p, b, 0, 0)),
                  pl.BlockSpec((STICK_PAIRS, 1, seq, LANES), lambda b, p: (2 * steps + p, b, 0, 0)),
                  pl.BlockSpec((STICK_KEYS, STICK_KEYS), lambda b, p: (0, 0))],
        out_specs=pl.BlockSpec((1, seq, STICK_PAIRS * LANES), lambda b, p: (b, 0, p)),
        out_shape=jax.ShapeDtypeStruct((batch, seq, D_MODEL), BF16),
        scratch_shapes=[pltpu.VMEM((STICK_PAIRS, seq // STICK_QUERIES, 2 * STICK_QUERIES, LANES),
                                   F32)] * 2,
        compiler_params=pltpu.CompilerParams(dimension_semantics=("parallel", "parallel"),
                                             vmem_limit_bytes=VMEM_LIMIT_BYTES),
        name="stick_attention",
    )(qkv, qkv, qkv, uu)


def _relative_bucket(distance):
    max_exact = NUM_BUCKETS // 2
    n = jnp.maximum(distance, 0)
    large = max_exact + (jnp.log(jnp.maximum(n, 1).astype(F32) / max_exact)
                         / math.log(MAX_DISTANCE / max_exact)
                         * (NUM_BUCKETS - max_exact)).astype(jnp.int32)
    large = jnp.minimum(large, NUM_BUCKETS - 1)
    return jnp.where(n < max_exact, n, large)


def _band_bias(rel_bias, window, dilation):
    i = jnp.arange(Q_BLOCK)[:, None]
    m = jnp.arange(2 * Q_BLOCK)[None, :]
    rel = Q_BLOCK + i - m
    band = (rel >= 0) & (rel <= window // dilation)
    onehot = jax.nn.one_hot(_relative_bucket(rel * dilation), NUM_BUCKETS, dtype=F32)
    bias = jnp.einsum('qkb,bh->hqk', onehot, rel_bias.astype(F32), precision=lax.Precision.HIGHEST)
    bias = jnp.where(band[None], bias, NEG_INF)
    return bias.reshape(N_PAIRS, 2 * Q_BLOCK, 2 * Q_BLOCK)


def _band_attend(q_tile, kk, vv, bias):
    qs = _split_heads(q_tile)
    s = lax.dot_general(qs, kk, (((1,), (1,)), ((), ())), preferred_element_type=F32) + bias
    m = jnp.max(s, axis=-1, keepdims=True)
    p = jnp.exp2(s - m)
    l = jnp.sum(p, axis=-1, keepdims=True)
    return jnp.dot(p.astype(BF16), vv, preferred_element_type=F32), m, l


def _dilated_kernel(*refs, with_prev):
    if with_prev:
        q_ref, kp_ref, kc_ref, vp_ref, vc_ref, bias_first_ref, bias_ref, o_ref, s_ref = refs
    else:
        q_ref, kc_ref, vc_ref, bias_ref, o_ref, s_ref = refs
    for sub in range(DILATED_SUBS):
        rows = slice(sub * Q_BLOCK, (sub + 1) * Q_BLOCK)
        stats_at = (lambda cols: (0, 0, rows, cols)) if with_prev else (lambda cols: (0, sub, slice(None), cols))
        s_ref[stats_at(slice(None))] = jnp.concatenate(
            [jnp.zeros((Q_BLOCK, LANES), F32), jnp.ones((Q_BLOCK, LANES), F32)], axis=1)
        for hp in range(N_PAIRS):
            sl = slice(hp * LANES, (hp + 1) * LANES)
            if not with_prev:
                q_tile, kk, vv = q_ref[0, sub, :, sl], kc_ref[0, sub, :, sl], vc_ref[0, sub, :, sl]
                bias = bias_ref[hp]
            elif sub == 0:
                q_tile = q_ref[0, 0, rows, sl]
                kk = jnp.concatenate([kp_ref[0, 0, :, sl], kc_ref[0, 0, rows, sl]], axis=0)
                vv = jnp.concatenate([vp_ref[0, 0, :, sl], vc_ref[0, 0, rows, sl]], axis=0)
                bias = bias_first_ref[0, hp]
            else:
                keys = slice((sub - 1) * Q_BLOCK, (sub + 1) * Q_BLOCK)
                q_tile, kk, vv = q_ref[0, 0, rows, sl], kc_ref[0, 0, keys, sl], vc_ref[0, 0, keys, sl]
                bias = bias_ref[0, hp]
            o, m, l = _band_attend(q_tile, kk, vv, bias)
            if with_prev:
                o_ref[0, 0, rows, sl] = _merge_heads(o).astype(o_ref.dtype)
            else:
                o_ref[0, sub, :, sl] = _merge_heads(o).astype(o_ref.dtype)
            for head, half in ((2 * hp, slice(0, Q_BLOCK)), (2 * hp + 1, slice(Q_BLOCK, None))):
                s_ref[stats_at(slice(head, head + 1))] = m[half]
                s_ref[stats_at(slice(LANES + head, LANES + head + 1))] = l[half]


def _dilated_group(q, k, v, rel_bias, group):
    window, dilation = DILATED_GROUPS[group]
    batch, _, strided, d = q.shape
    assert strided % Q_BLOCK == 0
    nb = strided // Q_BLOCK
    with_prev = nb > 1
    bias = _band_bias(rel_bias, window, dilation) * LOG2_E
    if with_prev:
        assert nb % DILATED_SUBS == 0
        key = jnp.arange(2 * Q_BLOCK)[None, None, :]
        bias = jnp.stack([jnp.where(key < Q_BLOCK, NEG_INF, bias), bias])
        grid = (batch, dilation, nb // DILATED_SUBS)
        two = pl.BlockSpec((1, 1, DILATED_SUBS * Q_BLOCK, d), lambda b, r, i: (b, r, i, 0))
        prev = pl.BlockSpec((1, 1, Q_BLOCK, d),
                            lambda b, r, i: (b, r, jnp.maximum(DILATED_SUBS * i - 1, 0), 0))
        table = (1,) + bias.shape[1:]
        in_specs = [two, prev, two, prev, two,
                    pl.BlockSpec(table, lambda b, r, i: (jnp.minimum(i, 1), 0, 0, 0)),
                    pl.BlockSpec(table, lambda b, r, i: (1, 0, 0, 0))]
        args = [q, k, k, v, v, bias, bias]
        out_specs = [two, pl.BlockSpec((1, 1, DILATED_SUBS * Q_BLOCK, 2 * LANES),
                                       lambda b, r, i: (b, r, i, 0))]
    else:
        assert dilation % DILATED_SUBS == 0
        bias = bias[:, :, Q_BLOCK:]
        grid = (batch, dilation // DILATED_SUBS, 1)
        two = pl.BlockSpec((1, DILATED_SUBS, Q_BLOCK, d), lambda b, r, i: (b, r, 0, 0))
        in_specs = [two, two, two, pl.BlockSpec(bias.shape, lambda b, r, i: (0, 0, 0))]
        args = [q, k, v, bias]
        out_specs = [two, pl.BlockSpec((1, DILATED_SUBS, Q_BLOCK, 2 * LANES), lambda b, r, i: (b, r, 0, 0))]
    return pl.pallas_call(
        functools.partial(_dilated_kernel, with_prev=with_prev),
        grid=grid,
        in_specs=in_specs,
        out_specs=out_specs,
        out_shape=[jax.ShapeDtypeStruct((batch, dilation, strided, d), BF16),
                   jax.ShapeDtypeStruct((batch, dilation, strided, 2 * LANES), F32)],
        compiler_params=pltpu.CompilerParams(dimension_semantics=("parallel", "parallel", "parallel"),
                                             vmem_limit_bytes=VMEM_LIMIT_BYTES),
        name=f"dilated_group{group}",
    )(*args)


def _merge_groups(o_refs, s_refs, e_ref, o_tok_refs, s_tok_refs, mid_ref):
    outs, tops, dens = [], [], []
    for g, (_, dil) in enumerate(DILATED_GROUPS):
        if dil == 1:
            outs.append(o_refs[g][0, 0].astype(F32))
            tops.append(s_refs[g][0, 0, :, :LANES])
            dens.append(s_refs[g][0, 0, :, LANES:])
        else:
            n = o_refs[g].shape[2]
            tm = n * dil
            coarse = dil // MERGE_STRIDE if dil > MERGE_STRIDE and dil % MERGE_STRIDE == 0 else 1
            first = o_tok_refs[g] if coarse == 1 else mid_ref
            for r in range(dil):
                a, b = divmod(r, coarse)
                rows = pl.ds(b * (tm // coarse) + a, n, stride=dil // coarse)
                part = o_refs[g][0, r].astype(F32)
                for cb in range(N_PAIRS):
                    first[cb, rows, :] = part[:, cb * LANES:(cb + 1) * LANES]
                s_tok_refs[g][0, pl.ds(r, n, stride=dil), :] = s_refs[g][0, r, :, :LANES]
                s_tok_refs[g][1, pl.ds(r, n, stride=dil), :] = s_refs[g][0, r, :, LANES:]
            if coarse > 1:
                for r in range(coarse):
                    block = slice(r * (tm // coarse), (r + 1) * (tm // coarse))
                    for cb in range(N_PAIRS):
                        o_tok_refs[g][cb, pl.ds(r, tm // coarse, stride=coarse), :] = mid_ref[cb, block, :]
            outs.append(jnp.concatenate([o_tok_refs[g][cb] for cb in range(N_PAIRS)], axis=1))
            tops.append(s_tok_refs[g][0])
            dens.append(s_tok_refs[g][1])
    top = functools.reduce(jnp.maximum, tops)
    es = [jnp.exp2(m - top) for m in tops]
    inv = 1.0 / functools.reduce(jnp.add, [e * l for e, l in zip(es, dens)])
    merged = None
    for out, e in zip(outs, es):
        wt = e * inv
        hi = wt.astype(BF16)
        lo = (wt - hi.astype(F32)).astype(BF16)
        spread = jnp.dot(jnp.concatenate([hi, lo], axis=1), e_ref[...], preferred_element_type=F32)
        merged = spread * out if merged is None else merged + spread * out
    return merged.astype(BF16)


def _oproj_ffn_kernel(*refs, merge, n_cast):
    if merge:
        o_refs, s_refs, e_ref = refs[0:N_GROUPS], refs[N_GROUPS:2 * N_GROUPS], refs[2 * N_GROUPS]
        refs = refs[2 * N_GROUPS + 1:]
        strided_groups = [g for g, (_, dil) in enumerate(DILATED_GROUPS) if dil > 1]
        n_tok = len(strided_groups)
        mid_ref, refs = refs[-1], refs[:-1]
        o_tok_refs = dict(zip(strided_groups, refs[-2 * n_tok:-n_tok]))
        s_tok_refs = dict(zip(strided_groups, refs[-n_tok:]))
        refs = refs[:-2 * n_tok]
    else:
        o_ref, refs = refs[0], refs[1:]
    x_ref, wo_ref, g_ref, wgu_ref, wd_ref = refs[:5]
    cast_in, out_ref, cast_out = refs[5:5 + n_cast], refs[5 + n_cast], refs[6 + n_cast:6 + 2 * n_cast]
    xmid_ref, xn_ref, hid_ref = refs[6 + 2 * n_cast:]
    for src, dst in zip(cast_in, cast_out):
        dst[...] = src[...].astype(BF16)
    if merge:
        attn = _merge_groups(o_refs, s_refs, e_ref, o_tok_refs, s_tok_refs, mid_ref)
    else:
        attn = o_ref[...]
    tm = x_ref.shape[0]
    halves = [slice(p * (tm // 2), (p + 1) * (tm // 2)) for p in range(2)]

    def swiglu_chunk(c, rows):
        lo = c * FFN_COLS
        xn = xn_ref[rows, :]
        gate = jnp.dot(xn, wgu_ref[:, lo:lo + FFN_COLS], preferred_element_type=F32)
        up = jnp.dot(xn, wgu_ref[:, D_FF + lo:D_FF + lo + FFN_COLS], preferred_element_type=F32)
        hid_ref[rows, lo:lo + FFN_COLS] = (gate * jax.nn.sigmoid(gate) * up).astype(BF16)

    hs = [jnp.dot(attn[rows], wo_ref[...], preferred_element_type=F32) for rows in halves]
    for rows, h in zip(halves, hs):
        xm = x_ref[rows, :] + _rms_scale(h, g_ref[1:2, :])
        xmid_ref[rows, :] = xm
        xn_ref[rows, :] = _rms_scale(xm, g_ref[2:3, :]).astype(BF16)
        swiglu_chunk(0, rows)
    for c in range(1, D_FF // FFN_COLS):
        swiglu_chunk(c, slice(None))
    for rows in halves:
        down = jnp.dot(hid_ref[rows, :], wd_ref[...], preferred_element_type=F32)
        out_ref[rows, :] = xmid_ref[rows, :] + _rms_scale(down, g_ref[3:4, :])


def _cast_specs(to_cast, n_steps):
    in_specs, out_specs, out_shape, args = [], [], [], []
    for w, layer in to_cast:
        rows, cols = w.shape[-2:]
        n_blocks = n_steps
        while n_blocks > 1 and (rows % n_blocks or (rows // n_blocks) % 16):
            n_blocks //= 2
        assert n_steps % n_blocks == 0
        steps_per_block = n_steps // n_blocks
        block = (rows // n_blocks, cols)
        if layer is None:
            in_specs.append(pl.BlockSpec(block, lambda i, spb=steps_per_block: (i // spb, 0)))
        else:
            in_specs.append(pl.BlockSpec((None,) + block,
                                         lambda i, spb=steps_per_block, li=layer: (li, i // spb, 0)))
        out_specs.append(pl.BlockSpec(block, lambda i, spb=steps_per_block: (i // spb, 0)))
        out_shape.append(jax.ShapeDtypeStruct((rows, cols), BF16))
        args.append(w)
    return in_specs, out_specs, out_shape, args


def _oproj_ffn(attn, x2d, w_o, gains, w_gate_up, w_down, seq, to_cast=()):
    t, d = x2d.shape
    tm = FFN_TOKENS
    assert t % tm == 0 and seq % tm == 0 and D_FF % FFN_COLS == 0
    merge = not isinstance(attn, jax.Array)
    cast_in, cast_out, cast_shape, cast_args = _cast_specs(to_cast, t // tm)
    scratch = [pltpu.VMEM((tm, d), F32), pltpu.VMEM((tm, d), BF16), pltpu.VMEM((tm, D_FF), BF16)]
    resident = lambda shape: pl.BlockSpec(shape, lambda i: (0, 0), pipeline_mode=pl.Buffered(1))
    tok = pl.BlockSpec((tm, d), lambda i: (i, 0))
    if merge:
        tiles = seq // tm
        assert all(tm % (16 * dil) == 0 for _, dil in DILATED_GROUPS)
        blk = lambda dil, cols: pl.BlockSpec((1, dil, tm // dil, cols),
                                             lambda i: (i // tiles, 0, i % tiles, 0))
        head = jnp.arange(2 * LANES)[:, None] % LANES
        col = jnp.arange(d)[None, :] // HEAD_DIM
        spread = (head == col).astype(BF16)
        attn_specs = ([blk(dil, d) for _, dil in DILATED_GROUPS]
                      + [blk(dil, 2 * LANES) for _, dil in DILATED_GROUPS]
                      + [resident(spread.shape)])
        attn_args = [o for o, _ in attn] + [s for _, s in attn] + [spread]
        n_strided = sum(dil > 1 for _, dil in DILATED_GROUPS)
        scratch += ([pltpu.VMEM((d // LANES, tm, LANES), F32)] * n_strided
                    + [pltpu.VMEM((2, tm, LANES), F32)] * n_strided
                    + [pltpu.VMEM((d // LANES, tm, LANES), F32)])
    else:
        attn_specs = [tok]
        attn_args = [attn]
    out, *cast = pl.pallas_call(
        functools.partial(_oproj_ffn_kernel, merge=merge, n_cast=len(cast_args)),
        grid=(t // tm,),
        in_specs=attn_specs + [
            tok,
            resident(w_o.shape),
            resident(gains.shape),
            resident(w_gate_up.shape),
            resident(w_down.shape)] + cast_in,
        out_specs=[tok] + cast_out,
        out_shape=[jax.ShapeDtypeStruct((t, d), F32)] + cast_shape,
        scratch_shapes=scratch,
        compiler_params=pltpu.CompilerParams(dimension_semantics=("arbitrary",),
                                             vmem_limit_bytes=VMEM_LIMIT_BYTES),
        name="oproj_ffn_merge" if merge else "oproj_ffn",
    )(*attn_args, x2d, w_o, gains, w_gate_up, w_down, *cast_args)
    return out, cast


def kernel(x, norm_gains, w_qkv_a, w_o_a, g_kv, w_kv_b, w_q_b, w_o_b, rel_bias, w_gate_up, w_down):
    batch, seq, d = x.shape
    depth = norm_gains.shape[0]
    n_a = w_qkv_a.shape[0]
    t = batch * seq
    x2d = x.reshape(t, d)

    def layer_weights(layer):
        if layer < n_a:
            own = {"qkv": (w_qkv_a, layer), "o": (w_o_a, layer)}
        else:
            own = {"q": (w_q_b, layer - n_a), "o": (w_o_b, layer - n_a)}
            if layer == n_a:
                own["kv"] = (w_kv_b, None)
        return {**own, "gate_up": (w_gate_up, layer), "down": (w_down, layer)}

    pending = layer_weights(0)
    eager = ["qkv"] if n_a > 0 else list(pending)
    bf16 = {name: (w if idx is None else w[idx]).astype(BF16)
            for name, (w, idx) in ((name, pending.pop(name)) for name in eager)}
    shared_kv = None
    for layer in range(depth):
        g = norm_gains[layer]
        if layer < n_a:
            qkv, cast = _norm_proj(x2d, g[0], bf16["qkv"], to_cast=list(pending.values()))
            bf16.update(zip(pending, cast))
            pending = {}
            attn = _stick_attention(qkv, batch, seq).reshape(t, d)
        else:
            q_gain = g[0] * DILATED_Q_SCALE
            if shared_kv is None:
                q, shared_kv = _norm_proj_groups(x2d, [q_gain, g_kv], [bf16["q"], bf16["kv"]], batch, seq)
            else:
                (q,) = _norm_proj_groups(x2d, [q_gain], [bf16["q"]], batch, seq)
            attn = [_dilated_group(q[grp], shared_kv[2 * grp], shared_kv[2 * grp + 1], rel_bias, grp)
                    for grp in range(N_GROUPS)]
        following = layer_weights(layer + 1) if layer + 1 < depth else {}
        x2d, cast = _oproj_ffn(attn, x2d, bf16["o"], g, bf16["gate_up"], bf16["down"], seq,
                               to_cast=list(following.values()))
        bf16 = dict(zip(following, cast))
    return x2d.reshape(batch, seq, d)
```

```python
import functools
import math

import jax
import jax.numpy as jnp
from jax import lax
from jax.experimental import pallas as pl
from jax.experimental.pallas import tpu as pltpu

D_MODEL = 1024
N_HEADS = 16
HEAD_DIM = D_MODEL // N_HEADS
LANES = 128
N_PAIRS = D_MODEL // LANES
Q_BLOCK = 128
DILATED_GROUPS = ((128, 1), (512, 4), (2048, 16))
N_GROUPS = len(DILATED_GROUPS)
NUM_BUCKETS = 32
MAX_DISTANCE = 2048
D_FF = -(-8 * D_MODEL // (3 * 256)) * 256
RMS_EPS = 1e-6
NEG_INF = -1e30
QK_SCALE = HEAD_DIM ** -0.5
LOG2_E = math.log2(math.e)
DILATED_Q_SCALE = QK_SCALE * LOG2_E

VMEM_LIMIT_BYTES = 56 * 1024 * 1024
PROJ_TOKENS = 512
PROJ_COLS = 512
FFN_TOKENS = 512
FFN_COLS = 256
MERGE_STRIDE = 4
DILATED_SUBS = 8
DILATED_CLASSES = 4
STICK_PAIRS = 2
STICK_QUERIES = 256
STICK_KEYS = 256
STICK_UNDERFLOW = 128.0

F32 = jnp.float32
BF16 = jnp.bfloat16


def _rms_scale(x, gain_row):
    ms = jnp.mean(x * x, axis=-1, keepdims=True)
    return x * lax.rsqrt(ms + RMS_EPS) * gain_row


def _split_heads(tile):
    lane = lax.broadcasted_iota(jnp.int32, tile.shape, 1)
    zero = jnp.zeros_like(tile)
    return jnp.concatenate([jnp.where(lane < HEAD_DIM, tile, zero),
                            jnp.where(lane >= HEAD_DIM, tile, zero)], axis=0)


def _merge_heads(stacked):
    rows = stacked.shape[0] // 2
    lane = lax.broadcasted_iota(jnp.int32, (rows, LANES), 1)
    return jnp.where(lane < HEAD_DIM, stacked[:rows], stacked[rows:])


def _norm_proj_kernel(x_ref, g_ref, w_ref, *rest, n_cast):
    cast_in, o_ref, cast_out = rest[:n_cast], rest[n_cast], rest[n_cast + 1:]
    for src, dst in zip(cast_in, cast_out):
        dst[...] = src[...].astype(BF16)
    xn = _rms_scale(x_ref[...], g_ref[...]).astype(BF16)
    n_out = w_ref.shape[1]
    for c in range(n_out // PROJ_COLS):
        lo = c * PROJ_COLS
        res = jnp.dot(xn, w_ref[:, lo:lo + PROJ_COLS], preferred_element_type=F32).astype(o_ref.dtype)
        for k in range(PROJ_COLS // LANES):
            o_ref[lo // LANES + k] = res[:, k * LANES:(k + 1) * LANES]


def _norm_proj(x2d, gain, w, to_cast=()):
    t, d = x2d.shape
    n = w.shape[1]
    assert t % PROJ_TOKENS == 0 and n % PROJ_COLS == 0
    cast_in, cast_out, cast_shape, cast_args = _cast_specs(to_cast, t // PROJ_TOKENS)
    out, *cast = pl.pallas_call(
        functools.partial(_norm_proj_kernel, n_cast=len(cast_args)),
        grid=(t // PROJ_TOKENS,),
        in_specs=[pl.BlockSpec((PROJ_TOKENS, d), lambda i: (i, 0)),
                  pl.BlockSpec((1, d), lambda i: (0, 0)),
                  pl.BlockSpec((d, n), lambda i: (0, 0), pipeline_mode=pl.Buffered(1))] + cast_in,
        out_specs=[pl.BlockSpec((n // LANES, PROJ_TOKENS, LANES), lambda i: (0, i, 0))] + cast_out,
        out_shape=[jax.ShapeDtypeStruct((n // LANES, t, LANES), BF16)] + cast_shape,
        compiler_params=pltpu.CompilerParams(dimension_semantics=("arbitrary",),
                                             vmem_limit_bytes=VMEM_LIMIT_BYTES),
        name="norm_proj",
    )(x2d, gain.reshape(1, d), w, *cast_args)
    return out, cast


def _norm_proj_groups_kernel(x_ref, g_ref, *rest, chunks):
    n_proj = len(chunks)
    w_refs, out_refs, slabs = rest[:n_proj], rest[n_proj:-3], rest[-3:]
    tm = x_ref.shape[0]
    x = x_ref[...]
    xs = x * lax.rsqrt(jnp.mean(x * x, axis=-1, keepdims=True) + RMS_EPS)
    src, src_dil = slabs[0], 1
    for cb in range(N_PAIRS):
        src[cb] = xs[:, cb * LANES:(cb + 1) * LANES]
    first_out = [sum(chunks[:p]) * N_GROUPS for p in range(n_proj)]
    for g, (_, dil) in enumerate(DILATED_GROUPS):
        n = tm // dil
        if dil > 1:
            ratio = dil // src_dil
            dst = slabs[1] if src is not slabs[1] else slabs[2]
            for r in range(dil):
                a, b = divmod(r, src_dil)
                for cb in range(N_PAIRS):
                    dst[cb, r * n:(r + 1) * n, :] = src[cb, pl.ds(b * (tm // src_dil) + a, n, stride=ratio), :]
            src, src_dil = dst, dil
            regrouped = jnp.concatenate([src[cb] for cb in range(N_PAIRS)], axis=1)
        for p in range(n_proj):
            rows = xs if dil == 1 else regrouped
            xn = (rows * g_ref[p:p + 1, :]).astype(BF16)
            for c in range(chunks[p]):
                col = (g * chunks[p] + c) * D_MODEL
                res = jnp.dot(xn, w_refs[p][:, col:col + D_MODEL],
                              preferred_element_type=F32).astype(BF16)
                out = out_refs[first_out[p] + g * chunks[p] + c]
                for r in range(dil):
                    out[0, r] = res[r * n:(r + 1) * n]


def _norm_proj_groups(x2d, gains, ws, batch, seq):
    t, d = x2d.shape
    chunks = tuple(w.shape[1] // (N_GROUPS * d) for w in ws)
    tiles = seq // PROJ_TOKENS
    assert seq % PROJ_TOKENS == 0 and all(PROJ_TOKENS % (16 * dil) == 0 for _, dil in DILATED_GROUPS)
    dilations = [dil for _, dil in DILATED_GROUPS]
    assert dilations[0] == 1 and all(b % a == 0 for a, b in zip(dilations, dilations[1:]))
    out_shape, out_specs = [], []
    for n_chunks in chunks:
        for _, dil in DILATED_GROUPS:
            for _ in range(n_chunks):
                out_shape.append(jax.ShapeDtypeStruct((batch, dil, seq // dil, d), BF16))
                out_specs.append(pl.BlockSpec((1, dil, PROJ_TOKENS // dil, d), lambda b, j: (b, 0, j, 0)))
    outs = pl.pallas_call(
        functools.partial(_norm_proj_groups_kernel, chunks=chunks),
        grid=(batch, tiles),
        in_specs=([pl.BlockSpec((PROJ_TOKENS, d), lambda b, j: (b * tiles + j, 0)),
                   pl.BlockSpec((len(ws), d), lambda b, j: (0, 0))]
                  + [pl.BlockSpec(w.shape, lambda b, j: (0, 0), pipeline_mode=pl.Buffered(1)) for w in ws]),
        out_specs=out_specs,
        out_shape=out_shape,
        scratch_shapes=[pltpu.VMEM((d // LANES, PROJ_TOKENS, LANES), F32)] * 3,
        compiler_params=pltpu.CompilerParams(dimension_semantics=("parallel", "parallel"),
                                             vmem_limit_bytes=VMEM_LIMIT_BYTES),
        name="norm_proj_groups",
    )(x2d, jnp.stack(gains), *ws)
    split, start = [], 0
    for n_chunks in chunks:
        split.append(outs[start:start + n_chunks * N_GROUPS])
        start += n_chunks * N_GROUPS
    return split


def _stick_kernel(q_ref, k_ref, v_ref, u_ref, o_ref, acc_ref, carry_ref):
    rows = 2 * STICK_QUERIES
    nq = q_ref.shape[2] // STICK_QUERIES

    def span(ref, pair, idx):
        return ref[pair, 0, idx * STICK_KEYS:(idx + 1) * STICK_KEYS, :]

    def weights(z, carry, diagonal):
        cost = jnp.maximum(z, 0.0) + jnp.log(1.0 + jnp.exp(-jnp.abs(z)))
        if diagonal:
            qrow = lax.broadcasted_iota(jnp.int32, z.shape, 0) % STICK_QUERIES
            kcol = lax.broadcasted_iota(jnp.int32, z.shape, 1)
            strict = kcol < qrow
            cost = jnp.where(strict, cost, 0.0)
        w = z - jnp.dot(cost.astype(BF16), u_ref[...], preferred_element_type=F32)
        a = jnp.concatenate([jnp.exp(w[:, :LANES] - carry), jnp.exp(w[:, LANES:] - carry)], axis=1)
        if diagonal:
            a = jnp.where(strict, a, 0.0)
        return a.astype(BF16), carry + jnp.sum(cost, axis=-1, keepdims=True)

    def walk(pair, qi, spans, acc, carry):
        qs = _split_heads(span(q_ref, pair, qi)) * QK_SCALE
        for j in spans:
            z = lax.dot_general(qs, span(k_ref, pair, j), (((1,), (1,)), ((), ())),
                                preferred_element_type=F32)
            a, carry = weights(z, carry, j == qi)
            acc = acc + jnp.dot(a, span(v_ref, pair, j), preferred_element_type=F32)
        return acc, carry

    alive = {}
    for qi in range(nq):
        for pair in range(STICK_PAIRS):
            zeros = jnp.zeros((rows, LANES), F32)
            acc, carry = walk(pair, qi, range(qi, max(qi - 2, -1), -1), zeros, zeros)
            acc_ref[pair, qi] = acc
            if qi >= 2:
                carry_ref[pair, qi] = carry
                alive[pair, qi] = jnp.min(carry, axis=0, keepdims=True)[0, 0] < STICK_UNDERFLOW

    for pair, qi in alive:
        @pl.when(alive[pair, qi])
        def _(pair=pair, qi=qi):
            acc, _ = walk(pair, qi, range(qi - 2, -1, -1), acc_ref[pair, qi], carry_ref[pair, qi])
            acc_ref[pair, qi] = acc

    for qi in range(nq):
        for pair in range(STICK_PAIRS):
            o_ref[0, qi * STICK_QUERIES:(qi + 1) * STICK_QUERIES, pair * LANES:(pair + 1) * LANES] = (
                _merge_heads(acc_ref[pair, qi]).astype(o_ref.dtype))


def _stick_attention(qkv, batch, seq):
    assert STICK_QUERIES == STICK_KEYS and seq % STICK_QUERIES == 0
    qkv = qkv.reshape(3 * N_PAIRS, batch, seq, LANES)
    assert STICK_KEYS == 2 * LANES
    j = jnp.arange(STICK_KEYS)[:, None]
    s = jnp.arange(STICK_KEYS)[None, :]
    uu = (j >= s).astype(BF16)
    assert N_PAIRS % STICK_PAIRS == 0
    steps = N_PAIRS // STICK_PAIRS
    return pl.pallas_call(
        _stick_kernel,
        grid=(batch, steps),
        in_specs=[pl.BlockSpec((STICK_PAIRS, 1, seq, LANES), lambda b, p: (p, b, 0, 0)),
                  pl.BlockSpec((STICK_PAIRS, 1, seq, LANES), lambda b, p: (steps + p, b, 0, 0)),
                  pl.BlockSpec((STICK_PAIRS, 1, seq, LANES), lambda b, p: (2 * steps + p, b, 0, 0)),
                  pl.BlockSpec((STICK_KEYS, STICK_KEYS), lambda b, p: (0, 0))],
        out_specs=pl.BlockSpec((1, seq, STICK_PAIRS * LANES), lambda b, p: (b, 0, p)),
        out_shape=jax.ShapeDtypeStruct((batch, seq, D_MODEL), BF16),
        scratch_shapes=[pltpu.VMEM((STICK_PAIRS, seq // STICK_QUERIES, 2 * STICK_QUERIES, LANES),
                                   F32)] * 2,
        compiler_params=pltpu.CompilerParams(dimension_semantics=("parallel", "parallel"),
                                             vmem_limit_bytes=VMEM_LIMIT_BYTES),
        name="stick_attention",
    )(qkv, qkv, qkv, uu)


def _relative_bucket(distance):
    max_exact = NUM_BUCKETS // 2
    n = jnp.maximum(distance, 0)
    large = max_exact + (jnp.log(jnp.maximum(n, 1).astype(F32) / max_exact)
                         / math.log(MAX_DISTANCE / max_exact)
                         * (NUM_BUCKETS - max_exact)).astype(jnp.int32)
    large = jnp.minimum(large, NUM_BUCKETS - 1)
    return jnp.where(n < max_exact, n, large)


def _band_bias(rel_bias, window, dilation):
    i = jnp.arange(Q_BLOCK)[:, None]
    m = jnp.arange(2 * Q_BLOCK)[None, :]
    rel = Q_BLOCK + i - m
    band = (rel >= 0) & (rel <= window // dilation)
    onehot = jax.nn.one_hot(_relative_bucket(rel * dilation), NUM_BUCKETS, dtype=F32)
    bias = jnp.einsum('qkb,bh->hqk', onehot, rel_bias.astype(F32), precision=lax.Precision.HIGHEST)
    bias = jnp.where(band[None], bias, NEG_INF)
    return bias.reshape(N_PAIRS, 2 * Q_BLOCK, 2 * Q_BLOCK)


def _band_attend(q_tile, kk, vv, bias):
    qs = _split_heads(q_tile)
    s = lax.dot_general(qs, kk, (((1,), (1,)), ((), ())), preferred_element_type=F32) + bias
    m = jnp.max(s, axis=-1, keepdims=True)
    p = jnp.exp2(s - m)
    l = jnp.sum(p, axis=-1, keepdims=True)
    return jnp.dot(p.astype(BF16), vv, preferred_element_type=F32), m, l


def _dilated_kernel(*refs, with_prev, subs):
    if with_prev:
        q_ref, kp_ref, kc_ref, vp_ref, vc_ref, bias_first_ref, bias_ref, o_ref, s_ref = refs
    else:
        q_ref, kc_ref, vc_ref, bias_ref, o_ref, s_ref = refs
    for sub in range(subs):
        rows = slice(sub * Q_BLOCK, (sub + 1) * Q_BLOCK)
        stats_at = (lambda cols: (0, 0, rows, cols)) if with_prev else (lambda cols: (0, sub, slice(None), cols))
        s_ref[stats_at(slice(None))] = jnp.concatenate(
            [jnp.zeros((Q_BLOCK, LANES), F32), jnp.ones((Q_BLOCK, LANES), F32)], axis=1)
        for hp in range(N_PAIRS):
            sl = slice(hp * LANES, (hp + 1) * LANES)
            if not with_prev:
                q_tile, kk, vv = q_ref[0, sub, :, sl], kc_ref[0, sub, :, sl], vc_ref[0, sub, :, sl]
                bias = bias_ref[hp]
            elif sub == 0:
                q_tile = q_ref[0, 0, rows, sl]
                kk = jnp.concatenate([kp_ref[0, 0, :, sl], kc_ref[0, 0, rows, sl]], axis=0)
                vv = jnp.concatenate([vp_ref[0, 0, :, sl], vc_ref[0, 0, rows, sl]], axis=0)
                bias = bias_first_ref[0, hp]
            else:
                keys = slice((sub - 1) * Q_BLOCK, (sub + 1) * Q_BLOCK)
                q_tile, kk, vv = q_ref[0, 0, rows, sl], kc_ref[0, 0, keys, sl], vc_ref[0, 0, keys, sl]
                bias = bias_ref[0, hp]
            o, m, l = _band_attend(q_tile, kk, vv, bias)
            if with_prev:
                o_ref[0, 0, rows, sl] = _merge_heads(o).astype(o_ref.dtype)
            else:
                o_ref[0, sub, :, sl] = _merge_heads(o).astype(o_ref.dtype)
            for head, half in ((2 * hp, slice(0, Q_BLOCK)), (2 * hp + 1, slice(Q_BLOCK, None))):
                s_ref[stats_at(slice(head, head + 1))] = m[half]
                s_ref[stats_at(slice(LANES + head, LANES + head + 1))] = l[half]


def _dilated_group(q, k, v, rel_bias, group):
    window, dilation = DILATED_GROUPS[group]
    batch, _, strided, d = q.shape
    assert strided % Q_BLOCK == 0
    nb = strided // Q_BLOCK
    with_prev = nb > 1
    subs = min(DILATED_SUBS, nb) if with_prev else min(DILATED_CLASSES, dilation)
    bias = _band_bias(rel_bias, window, dilation) * LOG2_E
    if with_prev:
        assert nb % subs == 0
        key = jnp.arange(2 * Q_BLOCK)[None, None, :]
        bias = jnp.stack([jnp.where(key < Q_BLOCK, NEG_INF, bias), bias])
        grid = (batch, dilation, nb // subs)
        two = pl.BlockSpec((1, 1, subs * Q_BLOCK, d), lambda b, r, i: (b, r, i, 0))
        prev = pl.BlockSpec((1, 1, Q_BLOCK, d),
                            lambda b, r, i: (b, r, jnp.maximum(subs * i - 1, 0), 0))
        table = (1,) + bias.shape[1:]
        in_specs = [two, prev, two, prev, two,
                    pl.BlockSpec(table, lambda b, r, i: (jnp.minimum(i, 1), 0, 0, 0)),
                    pl.BlockSpec(table, lambda b, r, i: (1, 0, 0, 0))]
        args = [q, k, k, v, v, bias, bias]
        out_specs = [two, pl.BlockSpec((1, 1, subs * Q_BLOCK, 2 * LANES),
                                       lambda b, r, i: (b, r, i, 0))]
    else:
        assert dilation % subs == 0
        bias = bias[:, :, Q_BLOCK:]
        grid = (batch, dilation // subs, 1)
        two = pl.BlockSpec((1, subs, Q_BLOCK, d), lambda b, r, i: (b, r, 0, 0))
        in_specs = [two, two, two, pl.BlockSpec(bias.shape, lambda b, r, i: (0, 0, 0))]
        args = [q, k, v, bias]
        out_specs = [two, pl.BlockSpec((1, subs, Q_BLOCK, 2 * LANES), lambda b, r, i: (b, r, 0, 0))]
    return pl.pallas_call(
        functools.partial(_dilated_kernel, with_prev=with_prev, subs=subs),
        grid=grid,
        in_specs=in_specs,
        out_specs=out_specs,
        out_shape=[jax.ShapeDtypeStruct((batch, dilation, strided, d), BF16),
                   jax.ShapeDtypeStruct((batch, dilation, strided, 2 * LANES), F32)],
        compiler_params=pltpu.CompilerParams(dimension_semantics=("parallel", "parallel", "parallel"),
                                             vmem_limit_bytes=VMEM_LIMIT_BYTES),
        name=f"dilated_group{group}",
    )(*args)


def _merge_groups(o_refs, s_refs, e_ref, o_tok_refs, s_tok_refs, mid_ref):
    outs, tops, dens = [], [], []
    for g, (_, dil) in enumerate(DILATED_GROUPS):
        if dil == 1:
            outs.append(o_refs[g][0, 0].astype(F32))
            tops.append(s_refs[g][0, 0, :, :LANES])
            dens.append(s_refs[g][0, 0, :, LANES:])
        else:
            n = o_refs[g].shape[2]
            tm = n * dil
            coarse = dil // MERGE_STRIDE if dil > MERGE_STRIDE and dil % MERGE_STRIDE == 0 else 1
            first = o_tok_refs[g] if coarse == 1 else mid_ref
            for r in range(dil):
                a, b = divmod(r, coarse)
                rows = pl.ds(b * (tm // coarse) + a, n, stride=dil // coarse)
                part = o_refs[g][0, r].astype(F32)
                for cb in range(N_PAIRS):
                    first[cb, rows, :] = part[:, cb * LANES:(cb + 1) * LANES]
                s_tok_refs[g][0, pl.ds(r, n, stride=dil), :] = s_refs[g][0, r, :, :LANES]
                s_tok_refs[g][1, pl.ds(r, n, stride=dil), :] = s_refs[g][0, r, :, LANES:]
            if coarse > 1:
                for r in range(coarse):
                    block = slice(r * (tm // coarse), (r + 1) * (tm // coarse))
                    for cb in range(N_PAIRS):
                        o_tok_refs[g][cb, pl.ds(r, tm // coarse, stride=coarse), :] = mid_ref[cb, block, :]
            outs.append(jnp.concatenate([o_tok_refs[g][cb] for cb in range(N_PAIRS)], axis=1))
            tops.append(s_tok_refs[g][0])
            dens.append(s_tok_refs[g][1])
    top = functools.reduce(jnp.maximum, tops)
    es = [jnp.exp2(m - top) for m in tops]
    inv = 1.0 / functools.reduce(jnp.add, [e * l for e, l in zip(es, dens)])
    merged = None
    for out, e in zip(outs, es):
        wt = e * inv
        hi = wt.astype(BF16)
        lo = (wt - hi.astype(F32)).astype(BF16)
        spread = jnp.dot(jnp.concatenate([hi, lo], axis=1), e_ref[...], preferred_element_type=F32)
        merged = spread * out if merged is None else merged + spread * out
    return merged.astype(BF16)


def _oproj_ffn_kernel(*refs, merge, n_cast):
    if merge:
        o_refs, s_refs, e_ref = refs[0:N_GROUPS], refs[N_GROUPS:2 * N_GROUPS], refs[2 * N_GROUPS]
        refs = refs[2 * N_GROUPS + 1:]
        strided_groups = [g for g, (_, dil) in enumerate(DILATED_GROUPS) if dil > 1]
        n_tok = len(strided_groups)
        mid_ref, refs = refs[-1], refs[:-1]
        o_tok_refs = dict(zip(strided_groups, refs[-2 * n_tok:-n_tok]))
        s_tok_refs = dict(zip(strided_groups, refs[-n_tok:]))
        refs = refs[:-2 * n_tok]
    else:
        o_ref, refs = refs[0], refs[1:]
    x_ref, wo_ref, g_ref, wgu_ref, wd_ref = refs[:5]
    cast_in, out_ref, cast_out = refs[5:5 + n_cast], refs[5 + n_cast], refs[6 + n_cast:6 + 2 * n_cast]
    xmid_ref, xn_ref, hid_ref = refs[6 + 2 * n_cast:]
    for src, dst in zip(cast_in, cast_out):
        dst[...] = src[...].astype(BF16)
    if merge:
        attn = _merge_groups(o_refs, s_refs, e_ref, o_tok_refs, s_tok_refs, mid_ref)
    else:
        attn = o_ref[...]
    tm = x_ref.shape[0]
    halves = [slice(p * (tm // 2), (p + 1) * (tm // 2)) for p in range(2)]

    def swiglu_chunk(c, rows):
        lo = c * FFN_COLS
        xn = xn_ref[rows, :]
        gate = jnp.dot(xn, wgu_ref[:, lo:lo + FFN_COLS], preferred_element_type=F32)
        up = jnp.dot(xn, wgu_ref[:, D_FF + lo:D_FF + lo + FFN_COLS], preferred_element_type=F32)
        hid_ref[rows, lo:lo + FFN_COLS] = (gate * jax.nn.sigmoid(gate) * up).astype(BF16)

    hs = [jnp.dot(attn[rows], wo_ref[...], preferred_element_type=F32) for rows in halves]
    for rows, h in zip(halves, hs):
        xm = x_ref[rows, :] + _rms_scale(h, g_ref[1:2, :])
        xmid_ref[rows, :] = xm
        xn_ref[rows, :] = _rms_scale(xm, g_ref[2:3, :]).astype(BF16)
        swiglu_chunk(0, rows)
    for c in range(1, D_FF // FFN_COLS):
        swiglu_chunk(c, slice(None))
    for rows in halves:
        down = jnp.dot(hid_ref[rows, :], wd_ref[...], preferred_element_type=F32)
        out_ref[rows, :] = xmid_ref[rows, :] + _rms_scale(down, g_ref[3:4, :])


def _cast_specs(to_cast, n_steps):
    in_specs, out_specs, out_shape, args = [], [], [], []
    for w, layer in to_cast:
        rows, cols = w.shape[-2:]
        n_blocks = n_steps
        while n_blocks > 1 and (rows % n_blocks or (rows // n_blocks) % 16):
            n_blocks //= 2
        assert n_steps % n_blocks == 0
        steps_per_block = n_steps // n_blocks
        block = (rows // n_blocks, cols)
        if layer is None:
            in_specs.append(pl.BlockSpec(block, lambda i, spb=steps_per_block: (i // spb, 0)))
        else:
            in_specs.append(pl.BlockSpec((None,) + block,
                                         lambda i, spb=steps_per_block, li=layer: (li, i // spb, 0)))
        out_specs.append(pl.BlockSpec(block, lambda i, spb=steps_per_block: (i // spb, 0)))
        out_shape.append(jax.ShapeDtypeStruct((rows, cols), BF16))
        args.append(w)
    return in_specs, out_specs, out_shape, args


def _oproj_ffn(attn, x2d, w_o, gains, w_gate_up, w_down, seq, to_cast=()):
    t, d = x2d.shape
    tm = FFN_TOKENS
    assert t % tm == 0 and seq % tm == 0 and D_FF % FFN_COLS == 0
    merge = not isinstance(attn, jax.Array)
    cast_in, cast_out, cast_shape, cast_args = _cast_specs(to_cast, t // tm)
    scratch = [pltpu.VMEM((tm, d), F32), pltpu.VMEM((tm, d), BF16), pltpu.VMEM((tm, D_FF), BF16)]
    resident = lambda shape: pl.BlockSpec(shape, lambda i: (0, 0), pipeline_mode=pl.Buffered(1))
    tok = pl.BlockSpec((tm, d), lambda i: (i, 0))
    if merge:
        tiles = seq // tm
        assert all(tm % (16 * dil) == 0 for _, dil in DILATED_GROUPS)
        blk = lambda dil, cols: pl.BlockSpec((1, dil, tm // dil, cols),
                                             lambda i: (i // tiles, 0, i % tiles, 0))
        head = jnp.arange(2 * LANES)[:, None] % LANES
        col = jnp.arange(d)[None, :] // HEAD_DIM
        spread = (head == col).astype(BF16)
        attn_specs = ([blk(dil, d) for _, dil in DILATED_GROUPS]
                      + [blk(dil, 2 * LANES) for _, dil in DILATED_GROUPS]
                      + [resident(spread.shape)])
        attn_args = [o for o, _ in attn] + [s for _, s in attn] + [spread]
        n_strided = sum(dil > 1 for _, dil in DILATED_GROUPS)
        scratch += ([pltpu.VMEM((d // LANES, tm, LANES), F32)] * n_strided
                    + [pltpu.VMEM((2, tm, LANES), F32)] * n_strided
                    + [pltpu.VMEM((d // LANES, tm, LANES), F32)])
    else:
        attn_specs = [tok]
        attn_args = [attn]
    out, *cast = pl.pallas_call(
        functools.partial(_oproj_ffn_kernel, merge=merge, n_cast=len(cast_args)),
        grid=(t // tm,),
        in_specs=attn_specs + [
            tok,
            resident(w_o.shape),
            resident(gains.shape),
            resident(w_gate_up.shape),
            resident(w_down.shape)] + cast_in,
        out_specs=[tok] + cast_out,
        out_shape=[jax.ShapeDtypeStruct((t, d), F32)] + cast_shape,
        scratch_shapes=scratch,
        compiler_params=pltpu.CompilerParams(dimension_semantics=("arbitrary",),
                                             vmem_limit_bytes=VMEM_LIMIT_BYTES),
        name="oproj_ffn_merge" if merge else "oproj_ffn",
    )(*attn_args, x2d, w_o, gains, w_gate_up, w_down, *cast_args)
    return out, cast


def kernel(x, norm_gains, w_qkv_a, w_o_a, g_kv, w_kv_b, w_q_b, w_o_b, rel_bias, w_gate_up, w_down):
    batch, seq, d = x.shape
    depth = norm_gains.shape[0]
    n_a = w_qkv_a.shape[0]
    t = batch * seq
    x2d = x.reshape(t, d)

    def layer_weights(layer):
        if layer < n_a:
            own = {"qkv": (w_qkv_a, layer), "o": (w_o_a, layer)}
        else:
            own = {"q": (w_q_b, layer - n_a), "o": (w_o_b, layer - n_a)}
            if layer == n_a:
                own["kv"] = (w_kv_b, None)
        return {**own, "gate_up": (w_gate_up, layer), "down": (w_down, layer)}

    pending = layer_weights(0)
    eager = ["qkv"] if n_a > 0 else list(pending)
    bf16 = {name: (w if idx is None else w[idx]).astype(BF16)
            for name, (w, idx) in ((name, pending.pop(name)) for name in eager)}
    shared_kv = None
    for layer in range(depth):
        g = norm_gains[layer]
        if layer < n_a:
            qkv, cast = _norm_proj(x2d, g[0], bf16["qkv"], to_cast=list(pending.values()))
            bf16.update(zip(pending, cast))
            pending = {}
            attn = _stick_attention(qkv, batch, seq).reshape(t, d)
        else:
            q_gain = g[0] * DILATED_Q_SCALE
            if shared_kv is None:
                q, shared_kv = _norm_proj_groups(x2d, [q_gain, g_kv], [bf16["q"], bf16["kv"]], batch, seq)
            else:
                (q,) = _norm_proj_groups(x2d, [q_gain], [bf16["q"]], batch, seq)
            attn = [_dilated_group(q[grp], shared_kv[2 * grp], shared_kv[2 * grp + 1], rel_bias, grp)
                    for grp in range(N_GROUPS)]
        following = layer_weights(layer + 1) if layer + 1 < depth else {}
        x2d, cast = _oproj_ffn(attn, x2d, bf16["o"], g, bf16["gate_up"], bf16["down"], seq,
                               to_cast=list(following.values()))
        bf16 = dict(zip(following, cast))
    return x2d.reshape(batch, seq, d)
```

```python
import functools
import math

import jax
import jax.numpy as jnp
from jax import lax
from jax.experimental import pallas as pl
from jax.experimental.pallas import tpu as pltpu

D_MODEL = 1024
N_HEADS = 16
HEAD_DIM = D_MODEL // N_HEADS
LANES = 128
N_PAIRS = D_MODEL // LANES
Q_BLOCK = 128
DILATED_GROUPS = ((128, 1), (512, 4), (2048, 16))
N_GROUPS = len(DILATED_GROUPS)
NUM_BUCKETS = 32
MAX_DISTANCE = 2048
D_FF = -(-8 * D_MODEL // (3 * 256)) * 256
RMS_EPS = 1e-6
NEG_INF = -1e30
QK_SCALE = HEAD_DIM ** -0.5
LOG2_E = math.log2(math.e)
DILATED_Q_SCALE = QK_SCALE * LOG2_E

VMEM_LIMIT_BYTES = 56 * 1024 * 1024
PROJ_TOKENS = 512
PROJ_COLS = 512
FFN_TOKENS = 512
FFN_COLS = 256
MERGE_STRIDE = 4
DILATED_SUBS = 8
DILATED_CLASSES = 4
STICK_PAIRS = 2
STICK_QUERIES = 256
STICK_KEYS = 256
STICK_UNDERFLOW = 128.0

F32 = jnp.float32
BF16 = jnp.bfloat16


def _rms_scale(x, gain_row):
    ms = jnp.mean(x * x, axis=-1, keepdims=True)
    return x * lax.rsqrt(ms + RMS_EPS) * gain_row


def _split_heads(tile):
    lane = lax.broadcasted_iota(jnp.int32, tile.shape, 1)
    zero = jnp.zeros_like(tile)
    return jnp.concatenate([jnp.where(lane < HEAD_DIM, tile, zero),
                            jnp.where(lane >= HEAD_DIM, tile, zero)], axis=0)


def _merge_heads(stacked):
    rows = stacked.shape[0] // 2
    lane = lax.broadcasted_iota(jnp.int32, (rows, LANES), 1)
    return jnp.where(lane < HEAD_DIM, stacked[:rows], stacked[rows:])


def _norm_proj_kernel(x_ref, g_ref, w_ref, *rest, n_cast):
    cast_in, o_ref, cast_out = rest[:n_cast], rest[n_cast], rest[n_cast + 1:]
    for src, dst in zip(cast_in, cast_out):
        dst[...] = src[...].astype(BF16)
    xn = _rms_scale(x_ref[...], g_ref[...]).astype(BF16)
    n_out = w_ref.shape[1]
    for c in range(n_out // PROJ_COLS):
        lo = c * PROJ_COLS
        res = jnp.dot(xn, w_ref[:, lo:lo + PROJ_COLS], preferred_element_type=F32).astype(o_ref.dtype)
        for k in range(PROJ_COLS // LANES):
            o_ref[lo // LANES + k] = res[:, k * LANES:(k + 1) * LANES]


def _norm_proj(x2d, gain, w, to_cast=()):
    t, d = x2d.shape
    n = w.shape[1]
    assert t % PROJ_TOKENS == 0 and n % PROJ_COLS == 0
    cast_in, cast_out, cast_shape, cast_args = _cast_specs(to_cast, t // PROJ_TOKENS)
    out, *cast = pl.pallas_call(
        functools.partial(_norm_proj_kernel, n_cast=len(cast_args)),
        grid=(t // PROJ_TOKENS,),
        in_specs=[pl.BlockSpec((PROJ_TOKENS, d), lambda i: (i, 0)),
                  pl.BlockSpec((1, d), lambda i: (0, 0)),
                  pl.BlockSpec((d, n), lambda i: (0, 0), pipeline_mode=pl.Buffered(1))] + cast_in,
        out_specs=[pl.BlockSpec((n // LANES, PROJ_TOKENS, LANES), lambda i: (0, i, 0))] + cast_out,
        out_shape=[jax.ShapeDtypeStruct((n // LANES, t, LANES), BF16)] + cast_shape,
        compiler_params=pltpu.CompilerParams(dimension_semantics=("arbitrary",),
                                             vmem_limit_bytes=VMEM_LIMIT_BYTES),
        name="norm_proj",
    )(x2d, gain.reshape(1, d), w, *cast_args)
    return out, cast


def _norm_proj_groups_kernel(x_ref, g_ref, *rest, chunks):
    n_proj = len(chunks)
    w_refs, out_refs, slabs = rest[:n_proj], rest[n_proj:-3], rest[-3:]
    tm = x_ref.shape[0]
    x = x_ref[...]
    xs = x * lax.rsqrt(jnp.mean(x * x, axis=-1, keepdims=True) + RMS_EPS)
    src, src_dil = slabs[0], 1
    for cb in range(N_PAIRS):
        src[cb] = xs[:, cb * LANES:(cb + 1) * LANES]
    first_out = [sum(chunks[:p]) * N_GROUPS for p in range(n_proj)]
    for g, (_, dil) in enumerate(DILATED_GROUPS):
        n = tm // dil
        if dil > 1:
            ratio = dil // src_dil
            dst = slabs[1] if src is not slabs[1] else slabs[2]
            for r in range(dil):
                a, b = divmod(r, src_dil)
                for cb in range(N_PAIRS):
                    dst[cb, r * n:(r + 1) * n, :] = src[cb, pl.ds(b * (tm // src_dil) + a, n, stride=ratio), :]
            src, src_dil = dst, dil
            regrouped = jnp.concatenate([src[cb] for cb in range(N_PAIRS)], axis=1)
        for p in range(n_proj):
            rows = xs if dil == 1 else regrouped
            xn = (rows * g_ref[p:p + 1, :]).astype(BF16)
            for c in range(chunks[p]):
                col = (g * chunks[p] + c) * D_MODEL
                res = jnp.dot(xn, w_refs[p][:, col:col + D_MODEL],
                              preferred_element_type=F32).astype(BF16)
                out = out_refs[first_out[p] + g * chunks[p] + c]
                for r in range(dil):
                    out[0, r] = res[r * n:(r + 1) * n]


def _norm_proj_groups(x2d, gains, ws, batch, seq):
    t, d = x2d.shape
    chunks = tuple(w.shape[1] // (N_GROUPS * d) for w in ws)
    tiles = seq // PROJ_TOKENS
    assert seq % PROJ_TOKENS == 0 and all(PROJ_TOKENS % (16 * dil) == 0 for _, dil in DILATED_GROUPS)
    dilations = [dil for _, dil in DILATED_GROUPS]
    assert dilations[0] == 1 and all(b % a == 0 for a, b in zip(dilations, dilations[1:]))
    out_shape, out_specs = [], []
    for n_chunks in chunks:
        for _, dil in DILATED_GROUPS:
            for _ in range(n_chunks):
                out_shape.append(jax.ShapeDtypeStruct((batch, dil, seq // dil, d), BF16))
                out_specs.append(pl.BlockSpec((1, dil, PROJ_TOKENS // dil, d), lambda b, j: (b, 0, j, 0)))
    outs = pl.pallas_call(
        functools.partial(_norm_proj_groups_kernel, chunks=chunks),
        grid=(batch, tiles),
        in_specs=([pl.BlockSpec((PROJ_TOKENS, d), lambda b, j: (b * tiles + j, 0)),
                   pl.BlockSpec((len(ws), d), lambda b, j: (0, 0))]
                  + [pl.BlockSpec(w.shape, lambda b, j: (0, 0), pipeline_mode=pl.Buffered(1)) for w in ws]),
        out_specs=out_specs,
        out_shape=out_shape,
        scratch_shapes=[pltpu.VMEM((d // LANES, PROJ_TOKENS, LANES), F32)] * 3,
        compiler_params=pltpu.CompilerParams(dimension_semantics=("parallel", "parallel"),
                                             vmem_limit_bytes=VMEM_LIMIT_BYTES),
        name="norm_proj_groups",
    )(x2d, jnp.stack(gains), *ws)
    split, start = [], 0
    for n_chunks in chunks:
        split.append(outs[start:start + n_chunks * N_GROUPS])
        start += n_chunks * N_GROUPS
    return split


def _stick_kernel(q_ref, k_ref, v_ref, u_ref, o_ref, acc_ref, carry_ref):
    rows = 2 * STICK_QUERIES
    nq = q_ref.shape[2] // STICK_QUERIES

    def span(ref, pair, idx):
        return ref[pair, 0, idx * STICK_KEYS:(idx + 1) * STICK_KEYS, :]

    def weights(z, carry, diagonal):
        cost = jnp.maximum(z, 0.0) + jnp.log(1.0 + jnp.exp(-jnp.abs(z)))
        if diagonal:
            qrow = lax.broadcasted_iota(jnp.int32, z.shape, 0) % STICK_QUERIES
            kcol = lax.broadcasted_iota(jnp.int32, z.shape, 1)
            strict = kcol < qrow
            cost = jnp.where(strict, cost, 0.0)
        w = z - jnp.dot(cost.astype(BF16), u_ref[...], preferred_element_type=F32)
        a = jnp.concatenate([jnp.exp(w[:, :LANES] - carry), jnp.exp(w[:, LANES:] - carry)], axis=1)
        if diagonal:
            a = jnp.where(strict, a, 0.0)
        return a.astype(BF16), carry + jnp.sum(cost, axis=-1, keepdims=True)

    def walk(pair, qi, spans, acc, carry):
        qs = _split_heads(span(q_ref, pair, qi)) * QK_SCALE
        for j in spans:
            z = lax.dot_general(qs, span(k_ref, pair, j), (((1,), (1,)), ((), ())),
                                preferred_element_type=F32)
            a, carry = weights(z, carry, j == qi)
            acc = acc + jnp.dot(a, span(v_ref, pair, j), preferred_element_type=F32)
        return acc, carry

    alive = {}
    for qi in range(nq):
        for pair in range(STICK_PAIRS):
            zeros = jnp.zeros((rows, LANES), F32)
            acc, carry = walk(pair, qi, range(qi, max(qi - 2, -1), -1), zeros, zeros)
            acc_ref[pair, qi] = acc
            if qi >= 2:
                carry_ref[pair, qi] = carry
                alive[pair, qi] = jnp.min(carry, axis=0, keepdims=True)[0, 0] < STICK_UNDERFLOW

    for pair, qi in alive:
        @pl.when(alive[pair, qi])
        def _(pair=pair, qi=qi):
            acc, _ = walk(pair, qi, range(qi - 2, -1, -1), acc_ref[pair, qi], carry_ref[pair, qi])
            acc_ref[pair, qi] = acc

    for qi in range(nq):
        for pair in range(STICK_PAIRS):
            o_ref[0, qi * STICK_QUERIES:(qi + 1) * STICK_QUERIES, pair * LANES:(pair + 1) * LANES] = (
                _merge_heads(acc_ref[pair, qi]).astype(o_ref.dtype))


def _stick_attention(qkv, batch, seq):
    assert STICK_QUERIES == STICK_KEYS and seq % STICK_QUERIES == 0
    qkv = qkv.reshape(3 * N_PAIRS, batch, seq, LANES)
    assert STICK_KEYS == 2 * LANES
    j = jnp.arange(STICK_KEYS)[:, None]
    s = jnp.arange(STICK_KEYS)[None, :]
    uu = (j >= s).astype(BF16)
    assert N_PAIRS % STICK_PAIRS == 0
    steps = N_PAIRS // STICK_PAIRS
    return pl.pallas_call(
        _stick_kernel,
        grid=(batch, steps),
        in_specs=[pl.BlockSpec((STICK_PAIRS, 1, seq, LANES), lambda b, p: (p, b, 0, 0)),
                  pl.BlockSpec((STICK_PAIRS, 1, seq, LANES), lambda b, p: (steps + p, b, 0, 0)),
                  pl.BlockSpec((STICK_PAIRS, 1, seq, LANES), lambda b, p: (2 * steps + p, b, 0, 0)),
                  pl.BlockSpec((STICK_KEYS, STICK_KEYS), lambda b, p: (0, 0))],
        out_specs=pl.BlockSpec((1, seq, STICK_PAIRS * LANES), lambda b, p: (b, 0, p)),
        out_shape=jax.ShapeDtypeStruct((batch, seq, D_MODEL), BF16),
        scratch_shapes=[pltpu.VMEM((STICK_PAIRS, seq // STICK_QUERIES, 2 * STICK_QUERIES, LANES),
                                   F32)] * 2,
        compiler_params=pltpu.CompilerParams(dimension_semantics=("parallel", "parallel"),
                                             vmem_limit_bytes=VMEM_LIMIT_BYTES),
        name="stick_attention",
    )(qkv, qkv, qkv, uu)


def _relative_bucket(distance):
    max_exact = NUM_BUCKETS // 2
    n = jnp.maximum(distance, 0)
    large = max_exact + (jnp.log(jnp.maximum(n, 1).astype(F32) / max_exact)
                         / math.log(MAX_DISTANCE / max_exact)
                         * (NUM_BUCKETS - max_exact)).astype(jnp.int32)
    large = jnp.minimum(large, NUM_BUCKETS - 1)
    return jnp.where(n < max_exact, n, large)


def _band_bias(rel_bias, window, dilation):
    i = jnp.arange(Q_BLOCK)[:, None]
    m = jnp.arange(2 * Q_BLOCK)[None, :]
    rel = Q_BLOCK + i - m
    band = (rel >= 0) & (rel <= window // dilation)
    onehot = jax.nn.one_hot(_relative_bucket(rel * dilation), NUM_BUCKETS, dtype=F32)
    bias = jnp.einsum('qkb,bh->hqk', onehot, rel_bias.astype(F32), precision=lax.Precision.HIGHEST)
    bias = jnp.where(band[None], bias, NEG_INF)
    return bias.reshape(N_PAIRS, 2 * Q_BLOCK, 2 * Q_BLOCK)


def _band_attend(q_tile, kk, vv, bias):
    qs = _split_heads(q_tile)
    s = lax.dot_general(qs, kk, (((1,), (1,)), ((), ())), preferred_element_type=F32) + bias
    m = jnp.max(s, axis=-1, keepdims=True)
    p = jnp.exp2(s - m)
    l = jnp.sum(p, axis=-1, keepdims=True)
    return jnp.dot(p.astype(BF16), vv, preferred_element_type=F32), m, l


def _dilated_kernel(*refs, with_prev, subs, classes):
    if with_prev:
        q_ref, kp_ref, kc_ref, vp_ref, vc_ref, bias_first_ref, bias_ref, o_ref, s_ref = refs
    else:
        q_ref, kc_ref, vc_ref, bias_ref, o_ref, s_ref = refs
    for cls in range(classes):
        for sub in range(subs):
            rows = slice(sub * Q_BLOCK, (sub + 1) * Q_BLOCK)
            stats_at = ((lambda cols: (0, cls, rows, cols)) if with_prev
                        else (lambda cols: (0, sub, slice(None), cols)))
            s_ref[stats_at(slice(None))] = jnp.concatenate(
                [jnp.zeros((Q_BLOCK, LANES), F32), jnp.ones((Q_BLOCK, LANES), F32)], axis=1)
            for hp in range(N_PAIRS):
                sl = slice(hp * LANES, (hp + 1) * LANES)
                if not with_prev:
                    q_tile, kk, vv = q_ref[0, sub, :, sl], kc_ref[0, sub, :, sl], vc_ref[0, sub, :, sl]
                    bias = bias_ref[hp]
                elif sub == 0:
                    q_tile = q_ref[0, cls, rows, sl]
                    kk = jnp.concatenate([kp_ref[0, cls, :, sl], kc_ref[0, cls, rows, sl]], axis=0)
                    vv = jnp.concatenate([vp_ref[0, cls, :, sl], vc_ref[0, cls, rows, sl]], axis=0)
                    bias = bias_first_ref[0, hp]
                else:
                    keys = slice((sub - 1) * Q_BLOCK, (sub + 1) * Q_BLOCK)
                    q_tile, kk, vv = q_ref[0, cls, rows, sl], kc_ref[0, cls, keys, sl], vc_ref[0, cls, keys, sl]
                    bias = bias_ref[0, hp]
                o, m, l = _band_attend(q_tile, kk, vv, bias)
                if with_prev:
                    o_ref[0, cls, rows, sl] = _merge_heads(o).astype(o_ref.dtype)
                else:
                    o_ref[0, sub, :, sl] = _merge_heads(o).astype(o_ref.dtype)
                for head, half in ((2 * hp, slice(0, Q_BLOCK)), (2 * hp + 1, slice(Q_BLOCK, None))):
                    s_ref[stats_at(slice(head, head + 1))] = m[half]
                    s_ref[stats_at(slice(LANES + head, LANES + head + 1))] = l[half]


def _dilated_group(q, k, v, rel_bias, group):
    window, dilation = DILATED_GROUPS[group]
    batch, _, strided, d = q.shape
    assert strided % Q_BLOCK == 0
    nb = strided // Q_BLOCK
    with_prev = nb > 1
    subs = min(DILATED_SUBS, nb) if with_prev else min(DILATED_CLASSES, dilation)
    bias = _band_bias(rel_bias, window, dilation) * LOG2_E
    if with_prev:
        assert nb % subs == 0
        key = jnp.arange(2 * Q_BLOCK)[None, None, :]
        bias = jnp.stack([jnp.where(key < Q_BLOCK, NEG_INF, bias), bias])
        classes = min(dilation, DILATED_SUBS // subs)
        assert dilation % classes == 0
        grid = (batch, dilation // classes, nb // subs)
        two = pl.BlockSpec((1, classes, subs * Q_BLOCK, d), lambda b, r, i: (b, r, i, 0))
        prev = pl.BlockSpec((1, classes, Q_BLOCK, d),
                            lambda b, r, i: (b, r, jnp.maximum(subs * i - 1, 0), 0))
        table = (1,) + bias.shape[1:]
        in_specs = [two, prev, two, prev, two,
                    pl.BlockSpec(table, lambda b, r, i: (jnp.minimum(i, 1), 0, 0, 0)),
                    pl.BlockSpec(table, lambda b, r, i: (1, 0, 0, 0))]
        args = [q, k, k, v, v, bias, bias]
        out_specs = [two, pl.BlockSpec((1, classes, subs * Q_BLOCK, 2 * LANES),
                                       lambda b, r, i: (b, r, i, 0))]
    else:
        classes = 1
        assert dilation % subs == 0
        bias = bias[:, :, Q_BLOCK:]
        grid = (batch, dilation // subs, 1)
        two = pl.BlockSpec((1, subs, Q_BLOCK, d), lambda b, r, i: (b, r, 0, 0))
        in_specs = [two, two, two, pl.BlockSpec(bias.shape, lambda b, r, i: (0, 0, 0))]
        args = [q, k, v, bias]
        out_specs = [two, pl.BlockSpec((1, subs, Q_BLOCK, 2 * LANES), lambda b, r, i: (b, r, 0, 0))]
    return pl.pallas_call(
        functools.partial(_dilated_kernel, with_prev=with_prev, subs=subs, classes=classes),
        grid=grid,
        in_specs=in_specs,
        out_specs=out_specs,
        out_shape=[jax.ShapeDtypeStruct((batch, dilation, strided, d), BF16),
                   jax.ShapeDtypeStruct((batch, dilation, strided, 2 * LANES), F32)],
        compiler_params=pltpu.CompilerParams(dimension_semantics=("parallel", "parallel", "parallel"),
                                             vmem_limit_bytes=VMEM_LIMIT_BYTES),
        name=f"dilated_group{group}",
    )(*args)


def _merge_groups(o_refs, s_refs, e_ref, o_tok_refs, s_tok_refs, mid_ref):
    outs, tops, dens = [], [], []
    for g, (_, dil) in enumerate(DILATED_GROUPS):
        if dil == 1:
            outs.append(o_refs[g][0, 0].astype(F32))
            tops.append(s_refs[g][0, 0, :, :LANES])
            dens.append(s_refs[g][0, 0, :, LANES:])
        else:
            n = o_refs[g].shape[2]
            tm = n * dil
            coarse = dil // MERGE_STRIDE if dil > MERGE_STRIDE and dil % MERGE_STRIDE == 0 else 1
            first = o_tok_refs[g] if coarse == 1 else mid_ref
            for r in range(dil):
                a, b = divmod(r, coarse)
                rows = pl.ds(b * (tm // coarse) + a, n, stride=dil // coarse)
                part = o_refs[g][0, r].astype(F32)
                for cb in range(N_PAIRS):
                    first[cb, rows, :] = part[:, cb * LANES:(cb + 1) * LANES]
                s_tok_refs[g][0, pl.ds(r, n, stride=dil), :] = s_refs[g][0, r, :, :LANES]
                s_tok_refs[g][1, pl.ds(r, n, stride=dil), :] = s_refs[g][0, r, :, LANES:]
            if coarse > 1:
                for r in range(coarse):
                    block = slice(r * (tm // coarse), (r + 1) * (tm // coarse))
                    for cb in range(N_PAIRS):
                        o_tok_refs[g][cb, pl.ds(r, tm // coarse, stride=coarse), :] = mid_ref[cb, block, :]
            outs.append(jnp.concatenate([o_tok_refs[g][cb] for cb in range(N_PAIRS)], axis=1))
            tops.append(s_tok_refs[g][0])
            dens.append(s_tok_refs[g][1])
    top = functools.reduce(jnp.maximum, tops)
    es = [jnp.exp2(m - top) for m in tops]
    inv = 1.0 / functools.reduce(jnp.add, [e * l for e, l in zip(es, dens)])
    merged = None
    for out, e in zip(outs, es):
        wt = e * inv
        hi = wt.astype(BF16)
        lo = (wt - hi.astype(F32)).astype(BF16)
        spread = jnp.dot(jnp.concatenate([hi, lo], axis=1), e_ref[...], preferred_element_type=F32)
        merged = spread * out if merged is None else merged + spread * out
    return merged.astype(BF16)


def _oproj_ffn_kernel(*refs, merge, n_cast):
    if merge:
        o_refs, s_refs, e_ref = refs[0:N_GROUPS], refs[N_GROUPS:2 * N_GROUPS], refs[2 * N_GROUPS]
        refs = refs[2 * N_GROUPS + 1:]
        strided_groups = [g for g, (_, dil) in enumerate(DILATED_GROUPS) if dil > 1]
        n_tok = len(strided_groups)
        mid_ref, refs = refs[-1], refs[:-1]
        o_tok_refs = dict(zip(strided_groups, refs[-2 * n_tok:-n_tok]))
        s_tok_refs = dict(zip(strided_groups, refs[-n_tok:]))
        refs = refs[:-2 * n_tok]
    else:
        o_ref, refs = refs[0], refs[1:]
    x_ref, wo_ref, g_ref, wgu_ref, wd_ref = refs[:5]
    cast_in, out_ref, cast_out = refs[5:5 + n_cast], refs[5 + n_cast], refs[6 + n_cast:6 + 2 * n_cast]
    xmid_ref, xn_ref, hid_ref = refs[6 + 2 * n_cast:]
    for src, dst in zip(cast_in, cast_out):
        dst[...] = src[...].astype(BF16)
    if merge:
        attn = _merge_groups(o_refs, s_refs, e_ref, o_tok_refs, s_tok_refs, mid_ref)
    else:
        attn = o_ref[...]
    tm = x_ref.shape[0]
    halves = [slice(p * (tm // 2), (p + 1) * (tm // 2)) for p in range(2)]

    def swiglu_chunk(c, rows):
        lo = c * FFN_COLS
        xn = xn_ref[rows, :]
        gate = jnp.dot(xn, wgu_ref[:, lo:lo + FFN_COLS], preferred_element_type=F32)
        up = jnp.dot(xn, wgu_ref[:, D_FF + lo:D_FF + lo + FFN_COLS], preferred_element_type=F32)
        hid_ref[rows, lo:lo + FFN_COLS] = (gate * jax.nn.sigmoid(gate) * up).astype(BF16)

    hs = [jnp.dot(attn[rows], wo_ref[...], preferred_element_type=F32) for rows in halves]
    for rows, h in zip(halves, hs):
        xm = x_ref[rows, :] + _rms_scale(h, g_ref[1:2, :])
        xmid_ref[rows, :] = xm
        xn_ref[rows, :] = _rms_scale(xm, g_ref[2:3, :]).astype(BF16)
        swiglu_chunk(0, rows)
    for c in range(1, D_FF // FFN_COLS):
        swiglu_chunk(c, slice(None))
    for rows in halves:
        down = jnp.dot(hid_ref[rows, :], wd_ref[...], preferred_element_type=F32)
        out_ref[rows, :] = xmid_ref[rows, :] + _rms_scale(down, g_ref[3:4, :])


def _cast_specs(to_cast, n_steps):
    in_specs, out_specs, out_shape, args = [], [], [], []
    for w, layer in to_cast:
        rows, cols = w.shape[-2:]
        n_blocks = n_steps
        while n_blocks > 1 and (rows % n_blocks or (rows // n_blocks) % 16):
            n_blocks //= 2
        assert n_steps % n_blocks == 0
        steps_per_block = n_steps // n_blocks
        block = (rows // n_blocks, cols)
        if layer is None:
            in_specs.append(pl.BlockSpec(block, lambda i, spb=steps_per_block: (i // spb, 0)))
        else:
            in_specs.append(pl.BlockSpec((None,) + block,
                                         lambda i, spb=steps_per_block, li=layer: (li, i // spb, 0)))
        out_specs.append(pl.BlockSpec(block, lambda i, spb=steps_per_block: (i // spb, 0)))
        out_shape.append(jax.ShapeDtypeStruct((rows, cols), BF16))
        args.append(w)
    return in_specs, out_specs, out_shape, args


def _oproj_ffn(attn, x2d, w_o, gains, w_gate_up, w_down, seq, to_cast=()):
    t, d = x2d.shape
    tm = FFN_TOKENS
    assert t % tm == 0 and seq % tm == 0 and D_FF % FFN_COLS == 0
    merge = not isinstance(attn, jax.Array)
    cast_in, cast_out, cast_shape, cast_args = _cast_specs(to_cast, t // tm)
    scratch = [pltpu.VMEM((tm, d), F32), pltpu.VMEM((tm, d), BF16), pltpu.VMEM((tm, D_FF), BF16)]
    resident = lambda shape: pl.BlockSpec(shape, lambda i: (0, 0), pipeline_mode=pl.Buffered(1))
    tok = pl.BlockSpec((tm, d), lambda i: (i, 0))
    if merge:
        tiles = seq // tm
        assert all(tm % (16 * dil) == 0 for _, dil in DILATED_GROUPS)
        blk = lambda dil, cols: pl.BlockSpec((1, dil, tm // dil, cols),
                                             lambda i: (i // tiles, 0, i % tiles, 0))
        head = jnp.arange(2 * LANES)[:, None] % LANES
        col = jnp.arange(d)[None, :] // HEAD_DIM
        spread = (head == col).astype(BF16)
        attn_specs = ([blk(dil, d) for _, dil in DILATED_GROUPS]
                      + [blk(dil, 2 * LANES) for _, dil in DILATED_GROUPS]
                      + [resident(spread.shape)])
        attn_args = [o for o, _ in attn] + [s for _, s in attn] + [spread]
        n_strided = sum(dil > 1 for _, dil in DILATED_GROUPS)
        scratch += ([pltpu.VMEM((d // LANES, tm, LANES), F32)] * n_strided
                    + [pltpu.VMEM((2, tm, LANES), F32)] * n_strided
                    + [pltpu.VMEM((d // LANES, tm, LANES), F32)])
    else:
        attn_specs = [tok]
        attn_args = [attn]
    out, *cast = pl.pallas_call(
        functools.partial(_oproj_ffn_kernel, merge=merge, n_cast=len(cast_args)),
        grid=(t // tm,),
        in_specs=attn_specs + [
            tok,
            resident(w_o.shape),
            resident(gains.shape),
            resident(w_gate_up.shape),
            resident(w_down.shape)] + cast_in,
        out_specs=[tok] + cast_out,
        out_shape=[jax.ShapeDtypeStruct((t, d), F32)] + cast_shape,
        scratch_shapes=scratch,
        compiler_params=pltpu.CompilerParams(dimension_semantics=("arbitrary",),
                                             vmem_limit_bytes=VMEM_LIMIT_BYTES),
        name="oproj_ffn_merge" if merge else "oproj_ffn",
    )(*attn_args, x2d, w_o, gains, w_gate_up, w_down, *cast_args)
    return out, cast


def kernel(x, norm_gains, w_qkv_a, w_o_a, g_kv, w_kv_b, w_q_b, w_o_b, rel_bias, w_gate_up, w_down):
    batch, seq, d = x.shape
    depth = norm_gains.shape[0]
    n_a = w_qkv_a.shape[0]
    t = batch * seq
    x2d = x.reshape(t, d)

    def layer_weights(layer):
        if layer < n_a:
            own = {"qkv": (w_qkv_a, layer), "o": (w_o_a, layer)}
        else:
            own = {"q": (w_q_b, layer - n_a), "o": (w_o_b, layer - n_a)}
            if layer == n_a:
                own["kv"] = (w_kv_b, None)
        return {**own, "gate_up": (w_gate_up, layer), "down": (w_down, layer)}

    pending = layer_weights(0)
    eager = ["qkv"] if n_a > 0 else list(pending)
    bf16 = {name: (w if idx is None else w[idx]).astype(BF16)
            for name, (w, idx) in ((name, pending.pop(name)) for name in eager)}
    shared_kv = None
    for layer in range(depth):
        g = norm_gains[layer]
        if layer < n_a:
            qkv, cast = _norm_proj(x2d, g[0], bf16["qkv"], to_cast=list(pending.values()))
            bf16.update(zip(pending, cast))
            pending = {}
            attn = _stick_attention(qkv, batch, seq).reshape(t, d)
        else:
            q_gain = g[0] * DILATED_Q_SCALE
            if shared_kv is None:
                q, shared_kv = _norm_proj_groups(x2d, [q_gain, g_kv], [bf16["q"], bf16["kv"]], batch, seq)
            else:
                (q,) = _norm_proj_groups(x2d, [q_gain], [bf16["q"]], batch, seq)
            attn = [_dilated_group(q[grp], shared_kv[2 * grp], shared_kv[2 * grp + 1], rel_bias, grp)
                    for grp in range(N_GROUPS)]
        following = layer_weights(layer + 1) if layer + 1 < depth else {}
        x2d, cast = _oproj_ffn(attn, x2d, bf16["o"], g, bf16["gate_up"], bf16["down"], seq,
                               to_cast=list(following.values()))
        bf16 = dict(zip(following, cast))
    return x2d.reshape(batch, seq, d)
```

```python
import functools
import math

import jax
import jax.numpy as jnp
from jax import lax
from jax.experimental import pallas as pl
from jax.experimental.pallas import tpu as pltpu

D_MODEL = 1024
N_HEADS = 16
HEAD_DIM = D_MODEL // N_HEADS
LANES = 128
N_PAIRS = D_MODEL // LANES
Q_BLOCK = 128
DILATED_GROUPS = ((128, 1), (512, 4), (2048, 16))
N_GROUPS = len(DILATED_GROUPS)
NUM_BUCKETS = 32
MAX_DISTANCE = 2048
D_FF = -(-8 * D_MODEL // (3 * 256)) * 256
RMS_EPS = 1e-6
NEG_INF = -1e30
QK_SCALE = HEAD_DIM ** -0.5
LOG2_E = math.log2(math.e)
DILATED_Q_SCALE = QK_SCALE * LOG2_E

VMEM_LIMIT_BYTES = 56 * 1024 * 1024
PROJ_TOKENS = 512
PROJ_COLS = 512
FFN_TOKENS = 512
FFN_COLS = 256
MERGE_STRIDE = 4
DILATED_SUBS = 16
DILATED_CLASSES = 4
STICK_PAIRS = 2
STICK_QUERIES = 256
STICK_KEYS = 256
STICK_UNDERFLOW = 128.0

F32 = jnp.float32
BF16 = jnp.bfloat16


def _rms_scale(x, gain_row):
    ms = jnp.mean(x * x, axis=-1, keepdims=True)
    return x * lax.rsqrt(ms + RMS_EPS) * gain_row


def _split_heads(tile):
    lane = lax.broadcasted_iota(jnp.int32, tile.shape, 1)
    zero = jnp.zeros_like(tile)
    return jnp.concatenate([jnp.where(lane < HEAD_DIM, tile, zero),
                            jnp.where(lane >= HEAD_DIM, tile, zero)], axis=0)


def _merge_heads(stacked):
    rows = stacked.shape[0] // 2
    lane = lax.broadcasted_iota(jnp.int32, (rows, LANES), 1)
    return jnp.where(lane < HEAD_DIM, stacked[:rows], stacked[rows:])


def _norm_proj_kernel(x_ref, g_ref, w_ref, *rest, n_cast):
    cast_in, o_ref, cast_out = rest[:n_cast], rest[n_cast], rest[n_cast + 1:]
    for src, dst in zip(cast_in, cast_out):
        dst[...] = src[...].astype(BF16)
    xn = _rms_scale(x_ref[...], g_ref[...]).astype(BF16)
    n_out = w_ref.shape[1]
    for c in range(n_out // PROJ_COLS):
        lo = c * PROJ_COLS
        res = jnp.dot(xn, w_ref[:, lo:lo + PROJ_COLS], preferred_element_type=F32).astype(o_ref.dtype)
        for k in range(PROJ_COLS // LANES):
            o_ref[lo // LANES + k] = res[:, k * LANES:(k + 1) * LANES]


def _norm_proj(x2d, gain, w, to_cast=()):
    t, d = x2d.shape
    n = w.shape[1]
    assert t % PROJ_TOKENS == 0 and n % PROJ_COLS == 0
    cast_in, cast_out, cast_shape, cast_args = _cast_specs(to_cast, t // PROJ_TOKENS)
    out, *cast = pl.pallas_call(
        functools.partial(_norm_proj_kernel, n_cast=len(cast_args)),
        grid=(t // PROJ_TOKENS,),
        in_specs=[pl.BlockSpec((PROJ_TOKENS, d), lambda i: (i, 0)),
                  pl.BlockSpec((1, d), lambda i: (0, 0)),
                  pl.BlockSpec((d, n), lambda i: (0, 0), pipeline_mode=pl.Buffered(1))] + cast_in,
        out_specs=[pl.BlockSpec((n // LANES, PROJ_TOKENS, LANES), lambda i: (0, i, 0))] + cast_out,
        out_shape=[jax.ShapeDtypeStruct((n // LANES, t, LANES), BF16)] + cast_shape,
        compiler_params=pltpu.CompilerParams(dimension_semantics=("arbitrary",),
                                             vmem_limit_bytes=VMEM_LIMIT_BYTES),
        name="norm_proj",
    )(x2d, gain.reshape(1, d), w, *cast_args)
    return out, cast


def _norm_proj_groups_kernel(x_ref, g_ref, *rest, chunks):
    n_proj = len(chunks)
    w_refs, out_refs, slabs = rest[:n_proj], rest[n_proj:-3], rest[-3:]
    tm = x_ref.shape[0]
    x = x_ref[...]
    xs = x * lax.rsqrt(jnp.mean(x * x, axis=-1, keepdims=True) + RMS_EPS)
    src, src_dil = slabs[0], 1
    for cb in range(N_PAIRS):
        src[cb] = xs[:, cb * LANES:(cb + 1) * LANES]
    first_out = [sum(chunks[:p]) * N_GROUPS for p in range(n_proj)]
    for g, (_, dil) in enumerate(DILATED_GROUPS):
        n = tm // dil
        if dil > 1:
            ratio = dil // src_dil
            dst = slabs[1] if src is not slabs[1] else slabs[2]
            for r in range(dil):
                a, b = divmod(r, src_dil)
                for cb in range(N_PAIRS):
                    dst[cb, r * n:(r + 1) * n, :] = src[cb, pl.ds(b * (tm // src_dil) + a, n, stride=ratio), :]
            src, src_dil = dst, dil
            regrouped = jnp.concatenate([src[cb] for cb in range(N_PAIRS)], axis=1)
        for p in range(n_proj):
            rows = xs if dil == 1 else regrouped
            xn = (rows * g_ref[p:p + 1, :]).astype(BF16)
            for c in range(chunks[p]):
                col = (g * chunks[p] + c) * D_MODEL
                res = jnp.dot(xn, w_refs[p][:, col:col + D_MODEL],
                              preferred_element_type=F32).astype(BF16)
                out = out_refs[first_out[p] + g * chunks[p] + c]
                for r in range(dil):
                    out[0, r] = res[r * n:(r + 1) * n]


def _norm_proj_groups(x2d, gains, ws, batch, seq):
    t, d = x2d.shape
    chunks = tuple(w.shape[1] // (N_GROUPS * d) for w in ws)
    tiles = seq // PROJ_TOKENS
    assert seq % PROJ_TOKENS == 0 and all(PROJ_TOKENS % (16 * dil) == 0 for _, dil in DILATED_GROUPS)
    dilations = [dil for _, dil in DILATED_GROUPS]
    assert dilations[0] == 1 and all(b % a == 0 for a, b in zip(dilations, dilations[1:]))
    out_shape, out_specs = [], []
    for n_chunks in chunks:
        for _, dil in DILATED_GROUPS:
            for _ in range(n_chunks):
                out_shape.append(jax.ShapeDtypeStruct((batch, dil, seq // dil, d), BF16))
                out_specs.append(pl.BlockSpec((1, dil, PROJ_TOKENS // dil, d), lambda b, j: (b, 0, j, 0)))
    outs = pl.pallas_call(
        functools.partial(_norm_proj_groups_kernel, chunks=chunks),
        grid=(batch, tiles),
        in_specs=([pl.BlockSpec((PROJ_TOKENS, d), lambda b, j: (b * tiles + j, 0)),
                   pl.BlockSpec((len(ws), d), lambda b, j: (0, 0))]
                  + [pl.BlockSpec(w.shape, lambda b, j: (0, 0), pipeline_mode=pl.Buffered(1)) for w in ws]),
        out_specs=out_specs,
        out_shape=out_shape,
        scratch_shapes=[pltpu.VMEM((d // LANES, PROJ_TOKENS, LANES), F32)] * 3,
        compiler_params=pltpu.CompilerParams(dimension_semantics=("parallel", "parallel"),
                                             vmem_limit_bytes=VMEM_LIMIT_BYTES),
        name="norm_proj_groups",
    )(x2d, jnp.stack(gains), *ws)
    split, start = [], 0
    for n_chunks in chunks:
        split.append(outs[start:start + n_chunks * N_GROUPS])
        start += n_chunks * N_GROUPS
    return split


def _stick_kernel(q_ref, k_ref, v_ref, u_ref, o_ref, acc_ref, carry_ref):
    rows = 2 * STICK_QUERIES
    nq = q_ref.shape[2] // STICK_QUERIES

    def span(ref, pair, idx):
        return ref[pair, 0, idx * STICK_KEYS:(idx + 1) * STICK_KEYS, :]

    def weights(z, carry, diagonal):
        cost = jnp.maximum(z, 0.0) + jnp.log(1.0 + jnp.exp(-jnp.abs(z)))
        if diagonal:
            qrow = lax.broadcasted_iota(jnp.int32, z.shape, 0) % STICK_QUERIES
            kcol = lax.broadcasted_iota(jnp.int32, z.shape, 1)
            strict = kcol < qrow
            cost = jnp.where(strict, cost, 0.0)
        w = z - jnp.dot(cost.astype(BF16), u_ref[...], preferred_element_type=F32)
        a = jnp.concatenate([jnp.exp(w[:, :LANES] - carry), jnp.exp(w[:, LANES:] - carry)], axis=1)
        if diagonal:
            a = jnp.where(strict, a, 0.0)
        return a.astype(BF16), carry + jnp.sum(cost, axis=-1, keepdims=True)

    def walk(pair, qi, spans, acc, carry):
        qs = _split_heads(span(q_ref, pair, qi)) * QK_SCALE
        for j in spans:
            z = lax.dot_general(qs, span(k_ref, pair, j), (((1,), (1,)), ((), ())),
                                preferred_element_type=F32)
            a, carry = weights(z, carry, j == qi)
            acc = acc + jnp.dot(a, span(v_ref, pair, j), preferred_element_type=F32)
        return acc, carry

    alive = {}
    for qi in range(nq):
        for pair in range(STICK_PAIRS):
            zeros = jnp.zeros((rows, LANES), F32)
            acc, carry = walk(pair, qi, range(qi, max(qi - 2, -1), -1), zeros, zeros)
            acc_ref[pair, qi] = acc
            if qi >= 2:
                carry_ref[pair, qi] = carry
                alive[pair, qi] = jnp.min(carry, axis=0, keepdims=True)[0, 0] < STICK_UNDERFLOW

    for pair, qi in alive:
        @pl.when(alive[pair, qi])
        def _(pair=pair, qi=qi):
            acc, _ = walk(pair, qi, range(qi - 2, -1, -1), acc_ref[pair, qi], carry_ref[pair, qi])
            acc_ref[pair, qi] = acc

    for qi in range(nq):
        for pair in range(STICK_PAIRS):
            o_ref[0, qi * STICK_QUERIES:(qi + 1) * STICK_QUERIES, pair * LANES:(pair + 1) * LANES] = (
                _merge_heads(acc_ref[pair, qi]).astype(o_ref.dtype))


def _stick_attention(qkv, batch, seq):
    assert STICK_QUERIES == STICK_KEYS and seq % STICK_QUERIES == 0
    qkv = qkv.reshape(3 * N_PAIRS, batch, seq, LANES)
    assert STICK_KEYS == 2 * LANES
    j = jnp.arange(STICK_KEYS)[:, None]
    s = jnp.arange(STICK_KEYS)[None, :]
    uu = (j >= s).astype(BF16)
    assert N_PAIRS % STICK_PAIRS == 0
    steps = N_PAIRS // STICK_PAIRS
    return pl.pallas_call(
        _stick_kernel,
        grid=(batch, steps),
        in_specs=[pl.BlockSpec((STICK_PAIRS, 1, seq, LANES), lambda b, p: (p, b, 0, 0)),
                  pl.BlockSpec((STICK_PAIRS, 1, seq, LANES), lambda b, p: (steps + p, b, 0, 0)),
                  pl.BlockSpec((STICK_PAIRS, 1, seq, LANES), lambda b, p: (2 * steps + p, b, 0, 0)),
                  pl.BlockSpec((STICK_KEYS, STICK_KEYS), lambda b, p: (0, 0))],
        out_specs=pl.BlockSpec((1, seq, STICK_PAIRS * LANES), lambda b, p: (b, 0, p)),
        out_shape=jax.ShapeDtypeStruct((batch, seq, D_MODEL), BF16),
        scratch_shapes=[pltpu.VMEM((STICK_PAIRS, seq // STICK_QUERIES, 2 * STICK_QUERIES, LANES),
                                   F32)] * 2,
        compiler_params=pltpu.CompilerParams(dimension_semantics=("parallel", "parallel"),
                                             vmem_limit_bytes=VMEM_LIMIT_BYTES),
        name="stick_attention",
    )(qkv, qkv, qkv, uu)


def _relative_bucket(distance):
    max_exact = NUM_BUCKETS // 2
    n = jnp.maximum(distance, 0)
    large = max_exact + (jnp.log(jnp.maximum(n, 1).astype(F32) / max_exact)
                         / math.log(MAX_DISTANCE / max_exact)
                         * (NUM_BUCKETS - max_exact)).astype(jnp.int32)
    large = jnp.minimum(large, NUM_BUCKETS - 1)
    return jnp.where(n < max_exact, n, large)


def _band_bias(rel_bias, window, dilation):
    i = jnp.arange(Q_BLOCK)[:, None]
    m = jnp.arange(2 * Q_BLOCK)[None, :]
    rel = Q_BLOCK + i - m
    band = (rel >= 0) & (rel <= window // dilation)
    onehot = jax.nn.one_hot(_relative_bucket(rel * dilation), NUM_BUCKETS, dtype=F32)
    bias = jnp.einsum('qkb,bh->hqk', onehot, rel_bias.astype(F32), precision=lax.Precision.HIGHEST)
    bias = jnp.where(band[None], bias, NEG_INF)
    return bias.reshape(N_PAIRS, 2 * Q_BLOCK, 2 * Q_BLOCK)


def _band_attend(q_tile, kk, vv, bias):
    qs = _split_heads(q_tile)
    s = lax.dot_general(qs, kk, (((1,), (1,)), ((), ())), preferred_element_type=F32) + bias
    m = jnp.max(s, axis=-1, keepdims=True)
    p = jnp.exp2(s - m)
    l = jnp.sum(p, axis=-1, keepdims=True)
    return jnp.dot(p.astype(BF16), vv, preferred_element_type=F32), m, l


def _dilated_kernel(*refs, with_prev, subs, classes):
    if with_prev:
        q_ref, kp_ref, kc_ref, vp_ref, vc_ref, bias_first_ref, bias_ref, o_ref, s_ref = refs
    else:
        q_ref, kc_ref, vc_ref, bias_ref, o_ref, s_ref = refs
    for cls in range(classes):
        for sub in range(subs):
            rows = slice(sub * Q_BLOCK, (sub + 1) * Q_BLOCK)
            stats_at = ((lambda cols: (0, cls, rows, cols)) if with_prev
                        else (lambda cols: (0, sub, slice(None), cols)))
            s_ref[stats_at(slice(None))] = jnp.concatenate(
                [jnp.zeros((Q_BLOCK, LANES), F32), jnp.ones((Q_BLOCK, LANES), F32)], axis=1)
            for hp in range(N_PAIRS):
                sl = slice(hp * LANES, (hp + 1) * LANES)
                if not with_prev:
                    q_tile, kk, vv = q_ref[0, sub, :, sl], kc_ref[0, sub, :, sl], vc_ref[0, sub, :, sl]
                    bias = bias_ref[hp]
                elif sub == 0:
                    q_tile = q_ref[0, cls, rows, sl]
                    kk = jnp.concatenate([kp_ref[0, cls, :, sl], kc_ref[0, cls, rows, sl]], axis=0)
                    vv = jnp.concatenate([vp_ref[0, cls, :, sl], vc_ref[0, cls, rows, sl]], axis=0)
                    bias = bias_first_ref[0, hp]
                else:
                    keys = slice((sub - 1) * Q_BLOCK, (sub + 1) * Q_BLOCK)
                    q_tile, kk, vv = q_ref[0, cls, rows, sl], kc_ref[0, cls, keys, sl], vc_ref[0, cls, keys, sl]
                    bias = bias_ref[0, hp]
                o, m, l = _band_attend(q_tile, kk, vv, bias)
                if with_prev:
                    o_ref[0, cls, rows, sl] = _merge_heads(o).astype(o_ref.dtype)
                else:
                    o_ref[0, sub, :, sl] = _merge_heads(o).astype(o_ref.dtype)
                for head, half in ((2 * hp, slice(0, Q_BLOCK)), (2 * hp + 1, slice(Q_BLOCK, None))):
                    s_ref[stats_at(slice(head, head + 1))] = m[half]
                    s_ref[stats_at(slice(LANES + head, LANES + head + 1))] = l[half]


def _dilated_group(q, k, v, rel_bias, group):
    window, dilation = DILATED_GROUPS[group]
    batch, _, strided, d = q.shape
    assert strided % Q_BLOCK == 0
    nb = strided // Q_BLOCK
    with_prev = nb > 1
    subs = min(DILATED_SUBS, nb) if with_prev else min(DILATED_CLASSES, dilation)
    bias = _band_bias(rel_bias, window, dilation) * LOG2_E
    if with_prev:
        assert nb % subs == 0
        key = jnp.arange(2 * Q_BLOCK)[None, None, :]
        bias = jnp.stack([jnp.where(key < Q_BLOCK, NEG_INF, bias), bias])
        classes = min(dilation, DILATED_SUBS // subs)
        assert dilation % classes == 0
        grid = (batch, dilation // classes, nb // subs)
        two = pl.BlockSpec((1, classes, subs * Q_BLOCK, d), lambda b, r, i: (b, r, i, 0))
        prev = pl.BlockSpec((1, classes, Q_BLOCK, d),
                            lambda b, r, i: (b, r, jnp.maximum(subs * i - 1, 0), 0))
        table = (1,) + bias.shape[1:]
        in_specs = [two, prev, two, prev, two,
                    pl.BlockSpec(table, lambda b, r, i: (jnp.minimum(i, 1), 0, 0, 0)),
                    pl.BlockSpec(table, lambda b, r, i: (1, 0, 0, 0))]
        args = [q, k, k, v, v, bias, bias]
        out_specs = [two, pl.BlockSpec((1, classes, subs * Q_BLOCK, 2 * LANES),
                                       lambda b, r, i: (b, r, i, 0))]
    else:
        classes = 1
        assert dilation % subs == 0
        bias = bias[:, :, Q_BLOCK:]
        grid = (batch, dilation // subs, 1)
        two = pl.BlockSpec((1, subs, Q_BLOCK, d), lambda b, r, i: (b, r, 0, 0))
        in_specs = [two, two, two, pl.BlockSpec(bias.shape, lambda b, r, i: (0, 0, 0))]
        args = [q, k, v, bias]
        out_specs = [two, pl.BlockSpec((1, subs, Q_BLOCK, 2 * LANES), lambda b, r, i: (b, r, 0, 0))]
    return pl.pallas_call(
        functools.partial(_dilated_kernel, with_prev=with_prev, subs=subs, classes=classes),
        grid=grid,
        in_specs=in_specs,
        out_specs=out_specs,
        out_shape=[jax.ShapeDtypeStruct((batch, dilation, strided, d), BF16),
                   jax.ShapeDtypeStruct((batch, dilation, strided, 2 * LANES), F32)],
        compiler_params=pltpu.CompilerParams(dimension_semantics=("parallel", "parallel", "parallel"),
                                             vmem_limit_bytes=VMEM_LIMIT_BYTES),
        name=f"dilated_group{group}",
    )(*args)


def _merge_groups(o_refs, s_refs, e_ref, o_tok_refs, s_tok_refs, mid_ref):
    outs, tops, dens = [], [], []
    for g, (_, dil) in enumerate(DILATED_GROUPS):
        if dil == 1:
            outs.append(o_refs[g][0, 0].astype(F32))
            tops.append(s_refs[g][0, 0, :, :LANES])
            dens.append(s_refs[g][0, 0, :, LANES:])
        else:
            n = o_refs[g].shape[2]
            tm = n * dil
            coarse = dil // MERGE_STRIDE if dil > MERGE_STRIDE and dil % MERGE_STRIDE == 0 else 1
            first = o_tok_refs[g] if coarse == 1 else mid_ref
            for r in range(dil):
                a, b = divmod(r, coarse)
                rows = pl.ds(b * (tm // coarse) + a, n, stride=dil // coarse)
                part = o_refs[g][0, r].astype(F32)
                for cb in range(N_PAIRS):
                    first[cb, rows, :] = part[:, cb * LANES:(cb + 1) * LANES]
                s_tok_refs[g][0, pl.ds(r, n, stride=dil), :] = s_refs[g][0, r, :, :LANES]
                s_tok_refs[g][1, pl.ds(r, n, stride=dil), :] = s_refs[g][0, r, :, LANES:]
            if coarse > 1:
                for r in range(coarse):
                    block = slice(r * (tm // coarse), (r + 1) * (tm // coarse))
                    for cb in range(N_PAIRS):
                        o_tok_refs[g][cb, pl.ds(r, tm // coarse, stride=coarse), :] = mid_ref[cb, block, :]
            outs.append(jnp.concatenate([o_tok_refs[g][cb] for cb in range(N_PAIRS)], axis=1))
            tops.append(s_tok_refs[g][0])
            dens.append(s_tok_refs[g][1])
    top = functools.reduce(jnp.maximum, tops)
    es = [jnp.exp2(m - top) for m in tops]
    inv = 1.0 / functools.reduce(jnp.add, [e * l for e, l in zip(es, dens)])
    merged = None
    for out, e in zip(outs, es):
        wt = e * inv
        hi = wt.astype(BF16)
        lo = (wt - hi.astype(F32)).astype(BF16)
        spread = jnp.dot(jnp.concatenate([hi, lo], axis=1), e_ref[...], preferred_element_type=F32)
        merged = spread * out if merged is None else merged + spread * out
    return merged.astype(BF16)


def _oproj_ffn_kernel(*refs, merge, n_cast):
    if merge:
        o_refs, s_refs, e_ref = refs[0:N_GROUPS], refs[N_GROUPS:2 * N_GROUPS], refs[2 * N_GROUPS]
        refs = refs[2 * N_GROUPS + 1:]
        strided_groups = [g for g, (_, dil) in enumerate(DILATED_GROUPS) if dil > 1]
        n_tok = len(strided_groups)
        mid_ref, refs = refs[-1], refs[:-1]
        o_tok_refs = dict(zip(strided_groups, refs[-2 * n_tok:-n_tok]))
        s_tok_refs = dict(zip(strided_groups, refs[-n_tok:]))
        refs = refs[:-2 * n_tok]
    else:
        o_ref, refs = refs[0], refs[1:]
    x_ref, wo_ref, g_ref, wgu_ref, wd_ref = refs[:5]
    cast_in, out_ref, cast_out = refs[5:5 + n_cast], refs[5 + n_cast], refs[6 + n_cast:6 + 2 * n_cast]
    xmid_ref, xn_ref, hid_ref = refs[6 + 2 * n_cast:]
    for src, dst in zip(cast_in, cast_out):
        dst[...] = src[...].astype(BF16)
    if merge:
        attn = _merge_groups(o_refs, s_refs, e_ref, o_tok_refs, s_tok_refs, mid_ref)
    else:
        attn = o_ref[...]
    tm = x_ref.shape[0]
    halves = [slice(p * (tm // 2), (p + 1) * (tm // 2)) for p in range(2)]

    def swiglu_chunk(c, rows):
        lo = c * FFN_COLS
        xn = xn_ref[rows, :]
        gate = jnp.dot(xn, wgu_ref[:, lo:lo + FFN_COLS], preferred_element_type=F32)
        up = jnp.dot(xn, wgu_ref[:, D_FF + lo:D_FF + lo + FFN_COLS], preferred_element_type=F32)
        hid_ref[rows, lo:lo + FFN_COLS] = (gate * jax.nn.sigmoid(gate) * up).astype(BF16)

    hs = [jnp.dot(attn[rows], wo_ref[...], preferred_element_type=F32) for rows in halves]
    for rows, h in zip(halves, hs):
        xm = x_ref[rows, :] + _rms_scale(h, g_ref[1:2, :])
        xmid_ref[rows, :] = xm
        xn_ref[rows, :] = _rms_scale(xm, g_ref[2:3, :]).astype(BF16)
        swiglu_chunk(0, rows)
    for c in range(1, D_FF // FFN_COLS):
        swiglu_chunk(c, slice(None))
    for rows in halves:
        down = jnp.dot(hid_ref[rows, :], wd_ref[...], preferred_element_type=F32)
        out_ref[rows, :] = xmid_ref[rows, :] + _rms_scale(down, g_ref[3:4, :])


def _cast_specs(to_cast, n_steps):
    in_specs, out_specs, out_shape, args = [], [], [], []
    for w, layer in to_cast:
        rows, cols = w.shape[-2:]
        n_blocks = n_steps
        while n_blocks > 1 and (rows % n_blocks or (rows // n_blocks) % 16):
            n_blocks //= 2
        assert n_steps % n_blocks == 0
        steps_per_block = n_steps // n_blocks
        block = (rows // n_blocks, cols)
        if layer is None:
            in_specs.append(pl.BlockSpec(block, lambda i, spb=steps_per_block: (i // spb, 0)))
        else:
            in_specs.append(pl.BlockSpec((None,) + block,
                                         lambda i, spb=steps_per_block, li=layer: (li, i // spb, 0)))
        out_specs.append(pl.BlockSpec(block, lambda i, spb=steps_per_block: (i // spb, 0)))
        out_shape.append(jax.ShapeDtypeStruct((rows, cols), BF16))
        args.append(w)
    return in_specs, out_specs, out_shape, args


def _oproj_ffn(attn, x2d, w_o, gains, w_gate_up, w_down, seq, to_cast=()):
    t, d = x2d.shape
    tm = FFN_TOKENS
    assert t % tm == 0 and seq % tm == 0 and D_FF % FFN_COLS == 0
    merge = not isinstance(attn, jax.Array)
    cast_in, cast_out, cast_shape, cast_args = _cast_specs(to_cast, t // tm)
    scratch = [pltpu.VMEM((tm, d), F32), pltpu.VMEM((tm, d), BF16), pltpu.VMEM((tm, D_FF), BF16)]
    resident = lambda shape: pl.BlockSpec(shape, lambda i: (0, 0), pipeline_mode=pl.Buffered(1))
    tok = pl.BlockSpec((tm, d), lambda i: (i, 0))
    if merge:
        tiles = seq // tm
        assert all(tm % (16 * dil) == 0 for _, dil in DILATED_GROUPS)
        blk = lambda dil, cols: pl.BlockSpec((1, dil, tm // dil, cols),
                                             lambda i: (i // tiles, 0, i % tiles, 0))
        head = jnp.arange(2 * LANES)[:, None] % LANES
        col = jnp.arange(d)[None, :] // HEAD_DIM
        spread = (head == col).astype(BF16)
        attn_specs = ([blk(dil, d) for _, dil in DILATED_GROUPS]
                      + [blk(dil, 2 * LANES) for _, dil in DILATED_GROUPS]
                      + [resident(spread.shape)])
        attn_args = [o for o, _ in attn] + [s for _, s in attn] + [spread]
        n_strided = sum(dil > 1 for _, dil in DILATED_GROUPS)
        scratch += ([pltpu.VMEM((d // LANES, tm, LANES), F32)] * n_strided
                    + [pltpu.VMEM((2, tm, LANES), F32)] * n_strided
                    + [pltpu.VMEM((d // LANES, tm, LANES), F32)])
    else:
        attn_specs = [tok]
        attn_args = [attn]
    out, *cast = pl.pallas_call(
        functools.partial(_oproj_ffn_kernel, merge=merge, n_cast=len(cast_args)),
        grid=(t // tm,),
        in_specs=attn_specs + [
            tok,
            resident(w_o.shape),
            resident(gains.shape),
            resident(w_gate_up.shape),
            resident(w_down.shape)] + cast_in,
        out_specs=[tok] + cast_out,
        out_shape=[jax.ShapeDtypeStruct((t, d), F32)] + cast_shape,
        scratch_shapes=scratch,
        compiler_params=pltpu.CompilerParams(dimension_semantics=("arbitrary",),
                                             vmem_limit_bytes=VMEM_LIMIT_BYTES),
        name="oproj_ffn_merge" if merge else "oproj_ffn",
    )(*attn_args, x2d, w_o, gains, w_gate_up, w_down, *cast_args)
    return out, cast


def kernel(x, norm_gains, w_qkv_a, w_o_a, g_kv, w_kv_b, w_q_b, w_o_b, rel_bias, w_gate_up, w_down):
    batch, seq, d = x.shape
    depth = norm_gains.shape[0]
    n_a = w_qkv_a.shape[0]
    t = batch * seq
    x2d = x.reshape(t, d)

    def layer_weights(layer):
        if layer < n_a:
            own = {"qkv": (w_qkv_a, layer), "o": (w_o_a, layer)}
        else:
            own = {"q": (w_q_b, layer - n_a), "o": (w_o_b, layer - n_a)}
            if layer == n_a:
                own["kv"] = (w_kv_b, None)
        return {**own, "gate_up": (w_gate_up, layer), "down": (w_down, layer)}

    pending = layer_weights(0)
    eager = ["qkv"] if n_a > 0 else list(pending)
    bf16 = {name: (w if idx is None else w[idx]).astype(BF16)
            for name, (w, idx) in ((name, pending.pop(name)) for name in eager)}
    shared_kv = None
    for layer in range(depth):
        g = norm_gains[layer]
        if layer < n_a:
            qkv, cast = _norm_proj(x2d, g[0], bf16["qkv"], to_cast=list(pending.values()))
            bf16.update(zip(pending, cast))
            pending = {}
            attn = _stick_attention(qkv, batch, seq).reshape(t, d)
        else:
            q_gain = g[0] * DILATED_Q_SCALE
            if shared_kv is None:
                q, shared_kv = _norm_proj_groups(x2d, [q_gain, g_kv], [bf16["q"], bf16["kv"]], batch, seq)
            else:
                (q,) = _norm_proj_groups(x2d, [q_gain], [bf16["q"]], batch, seq)
            attn = [_dilated_group(q[grp], shared_kv[2 * grp], shared_kv[2 * grp + 1], rel_bias, grp)
                    for grp in range(N_GROUPS)]
        following = layer_weights(layer + 1) if layer + 1 < depth else {}
        x2d, cast = _oproj_ffn(attn, x2d, bf16["o"], g, bf16["gate_up"], bf16["down"], seq,
                               to_cast=list(following.values()))
        bf16 = dict(zip(following, cast))
    return x2d.reshape(batch, seq, d)
```
